```python
import jax
import jax.numpy as jnp
from jax import lax
import numpy as np

D_MODEL = 1024
BATCH = 8
SEQ = 2048
DEPTH = 4

CONV_K = 4
NORM_EPS = 1e-6
RG_WIDTH = 512
RG_BLOCKS = 8
RG_BLOCK = RG_WIDTH // RG_BLOCKS
RG_C = 8.0
ML_HEADS = 4
ML_DH = 128
ML_WIDTH = ML_HEADS * ML_DH
ML_CHUNK = 64
GD_HEADS = 4
GD_DK = 128
GD_DV = 128
GD_QK = GD_HEADS * GD_DK
GD_WIDTH = GD_HEADS * GD_DV
GD_CHUNK = 64
D_MIX = RG_WIDTH + ML_WIDTH + GD_WIDTH
IN_SIZES = (RG_WIDTH, RG_WIDTH,
            ML_WIDTH, ML_WIDTH, ML_WIDTH, ML_WIDTH, ML_WIDTH, ML_HEADS, ML_HEADS,
            GD_QK, GD_QK, GD_WIDTH, GD_WIDTH, GD_HEADS, GD_HEADS)
D_IN = sum(IN_SIZES)

kernel_name = "hymba_rglru_mlstm_gdn_trunk"


def rmsnorm(x, w):
    xf = x.astype(jnp.float32)
    r = lax.rsqrt(jnp.mean(xf * xf, axis=-1, keepdims=True) + NORM_EPS)
    return (xf * r).astype(x.dtype) * w


def _rms_f32(x):
    return x * lax.rsqrt(jnp.mean(x * x, axis=-1, keepdims=True) + NORM_EPS)


def _l2norm(x):
    return x * lax.rsqrt(jnp.sum(x * x, axis=-1, keepdims=True) + NORM_EPS)


def causal_dwconv(x, w):
    K = w.shape[0]
    S = x.shape[1]
    xp = jnp.pad(x, ((0, 0), (K - 1, 0), (0, 0)))
    y = xp[:, 0:S] * w[0]
    for k in range(1, K):
        y = y + xp[:, k:k + S] * w[k]
    return y


def _lin_combine(left, right):
    a_l, b_l = left
    a_r, b_r = right
    return a_l * a_r, a_r * b_l + b_r


def rglru_branch(xb, zb, conv_w, conv_b, gate_w, gate_b, lam):
    Bn, S, _ = xb.shape
    xc = causal_dwconv(xb, conv_w) + conv_b
    xblk = xc.reshape(Bn, S, RG_BLOCKS, RG_BLOCK)
    gates = jnp.einsum('bsnc,gncd->gbsnd', xblk, gate_w).reshape(2, Bn, S, RG_WIDTH)
    gates = gates + gate_b[:, None, None, :]
    r = jax.nn.sigmoid(gates[0])
    i = jax.nn.sigmoid(gates[1])
    log_a = -RG_C * r * jax.nn.softplus(-lam)
    a = jnp.exp(log_a)
    b = jnp.sqrt(-jnp.expm1(2.0 * log_a)) * (i * xc)
    _, h = lax.associative_scan(_lin_combine, (a, b), axis=1)
    return h * jax.nn.silu(zb)


def mlstm_branch(q, k, v, o_pre, z, i_pre, f_pre, gate_b, norm_w):
    Bn, S, _ = q.shape
    L = ML_CHUNK
    N = S // L

    def chunks(t):
        return t.reshape(Bn, N, L, ML_HEADS, ML_DH).transpose(1, 0, 3, 2, 4)

    def chunks_s(t):
        return t.reshape(Bn, N, L, ML_HEADS).transpose(1, 0, 3, 2)

    qc = chunks(q) * (ML_DH ** -0.5)
    kc = chunks(k)
    vc = chunks(v)
    li = chunks_s(i_pre + gate_b[0])
    lf = chunks_s(jax.nn.log_sigmoid(f_pre + gate_b[1]))
    bcum = jnp.cumsum(lf, axis=-1)
    causal = jnp.tril(jnp.ones((L, L), dtype=bool))
    dmat = jnp.where(causal, bcum[..., :, None] - bcum[..., None, :] + li[..., None, :], -jnp.inf)
    w_state = bcum[..., -1:] - bcum + li

    def step(carry, xs):
        C, n, m = carry
        dm, bq, ws, qi, ki, vi = xs
        m_inter = bq + m[..., None]
        m_t = jnp.maximum(m_inter, jnp.max(dm, axis=-1))
        p = jnp.exp(dm - m_t[..., None])
        s = jnp.einsum('bhtd,bhsd->bhts', qi, ki) * p
        sc = jnp.exp(m_inter - m_t)
        num = jnp.einsum('bhts,bhse->bhte', s, vi) + sc[..., None] * jnp.einsum('bhtd,bhde->bhte', qi, C)
        den = jnp.sum(s, axis=-1) + sc * jnp.einsum('bhtd,bhd->bht', qi, n)
        h = num / jnp.maximum(jnp.abs(den), jnp.exp(-m_t))[..., None]
        g = bq[..., -1]
        m_new = jnp.maximum(g + m, jnp.max(ws, axis=-1))
        dec = jnp.exp(g + m - m_new)
        wk = jnp.exp(ws - m_new[..., None])
        C = dec[..., None, None] * C + jnp.einsum('bhs,bhsd,bhse->bhde', wk, ki, vi)
        n = dec[..., None] * n + jnp.einsum('bhs,bhsd->bhd', wk, ki)
        return (C, n, m_new), h

    init = (jnp.zeros((Bn, ML_HEADS, ML_DH, ML_DH), jnp.float32),
            jnp.zeros((Bn, ML_HEADS, ML_DH), jnp.float32),
            jnp.zeros((Bn, ML_HEADS), jnp.float32))
    _, h = lax.scan(step, init, (dmat, bcum, w_state, qc, kc, vc))
    h = h.transpose(1, 0, 3, 2, 4).reshape(Bn, S, ML_HEADS, ML_DH)
    h = _rms_f32(h) * norm_w.reshape(ML_HEADS, ML_DH)
    h = h.reshape(Bn, S, ML_WIDTH)
    return h * jax.nn.sigmoid(o_pre) * jax.nn.silu(z)


def gdn_branch(q, k, v, z, a_pre, b_pre, conv_w, a_log, dt_bias, norm_w):
    Bn, S, _ = q.shape
    L = GD_CHUNK
    N = S // L
    qkv = jax.nn.silu(causal_dwconv(jnp.concatenate([q, k, v], axis=-1), conv_w))
    q, k, v = jnp.split(qkv, [GD_QK, 2 * GD_QK], axis=-1)

    def chunks(t, d):
        return t.reshape(Bn, N, L, GD_HEADS, d).transpose(0, 3, 1, 2, 4)

    def chunks_s(t):
        return t.reshape(Bn, N, L, GD_HEADS).transpose(0, 3, 1, 2)

    q = _l2norm(chunks(q, GD_DK)) * (GD_DK ** -0.5)
    k = _l2norm(chunks(k, GD_DK))
    v = chunks(v, GD_DV)
    beta = chunks_s(jax.nn.sigmoid(b_pre))
    g = chunks_s(-jnp.exp(a_log) * jax.nn.softplus(a_pre + dt_bias))
    gc = jnp.cumsum(g, axis=-1)
    incl = jnp.tril(jnp.ones((L, L), dtype=bool))
    strict = jnp.tril(jnp.ones((L, L), dtype=bool), k=-1)
    gam = jnp.exp(jnp.where(incl, gc[..., :, None] - gc[..., None, :], -jnp.inf))
    kb = k * beta[..., None]
    m_strict = jnp.where(strict, jnp.einsum('bhntd,bhnsd->bhnts', kb, k) * gam, 0.0)
    eye = jnp.eye(L, dtype=m_strict.dtype)
    t_inv = lax.linalg.triangular_solve(eye + m_strict, jnp.broadcast_to(eye, m_strict.shape),
                                        left_side=True, lower=True, unit_diagonal=True)
    u = t_inv @ (v * beta[..., None])
    w = t_inv @ (kb * jnp.exp(gc)[..., None])
    aqk = jnp.einsum('bhntd,bhnsd->bhnts', q, k) * gam
    q_dec = q * jnp.exp(gc)[..., None]
    g_last = gc[..., -1]
    k_dec = k * jnp.exp(g_last[..., None] - gc)[..., None]
    xs = (jnp.moveaxis(u, 2, 0), jnp.moveaxis(w, 2, 0), jnp.moveaxis(aqk, 2, 0),
          jnp.moveaxis(q_dec, 2, 0), jnp.moveaxis(k_dec, 2, 0), jnp.moveaxis(g_last, 2, 0))

    def step(state, xs_i):
        ui, wi, ai, qi, ki, gl = xs_i
        v_new = ui - wi @ state
        o = qi @ state + ai @ v_new
        state = state * jnp.exp(gl)[..., None, None] + jnp.swapaxes(ki, -1, -2) @ v_new
        return state, o

    _, o = lax.scan(step, jnp.zeros((Bn, GD_HEADS, GD_DK, GD_DV), jnp.float32), xs)
    o = o.transpose(1, 0, 3, 2, 4).reshape(Bn, S, GD_HEADS, GD_DV)
    o = _rms_f32(o) * norm_w * jax.nn.silu(z.reshape(Bn, S, GD_HEADS, GD_DV))
    return o.reshape(Bn, S, GD_WIDTH)


def setup_inputs(seed: int = 0) -> dict:
    key = jax.random.key(seed)
    ks = jax.random.split(key, 17)
    f32 = jnp.float32
    nrm = jax.random.normal
    x = nrm(ks[0], (BATCH, SEQ, D_MODEL), f32)
    norm_w = 1.0 + 0.02 * nrm(ks[1], (DEPTH, D_MODEL), f32)
    w_in = nrm(ks[2], (DEPTH, D_MODEL, D_IN), f32) * (D_MODEL ** -0.5)
    rg_conv_w = nrm(ks[3], (DEPTH, CONV_K, RG_WIDTH), f32) * (CONV_K ** -0.5)
    rg_conv_b = 0.01 * nrm(ks[4], (DEPTH, RG_WIDTH), f32)
    rg_gate_w = nrm(ks[5], (DEPTH, 2, RG_BLOCKS, RG_BLOCK, RG_BLOCK), f32) * (RG_BLOCK ** -0.5)
    rg_gate_b = 0.01 * nrm(ks[6], (DEPTH, 2, RG_WIDTH), f32)
    a_c = jax.random.uniform(ks[7], (DEPTH, RG_WIDTH), f32, minval=0.9, maxval=0.999)
    a0 = a_c ** (1.0 / RG_C)
    rg_lambda = jnp.log(a0) - jnp.log1p(-a0)
    ml_i_b = 0.1 * nrm(ks[8], (DEPTH, ML_HEADS), f32)
    ml_f_b = jnp.linspace(3.0, 6.0, ML_HEADS, dtype=f32)[None, :] + 0.1 * nrm(ks[9], (DEPTH, ML_HEADS), f32)
    ml_gate_b = jnp.stack([ml_i_b, ml_f_b], axis=1)
    ml_norm_w = 1.0 + 0.02 * nrm(ks[10], (DEPTH, ML_WIDTH), f32)
    gd_conv_w = nrm(ks[11], (DEPTH, CONV_K, 2 * GD_QK + GD_WIDTH), f32) * (CONV_K ** -0.5)
    gd_a_log = jnp.log(jax.random.uniform(ks[12], (DEPTH, GD_HEADS), f32, minval=1.0, maxval=16.0))
    dt = jnp.exp(jax.random.uniform(ks[13], (DEPTH, GD_HEADS), f32,
                                    minval=float(np.log(1e-3)), maxval=float(np.log(1e-1))))
    gd_dt_bias = dt + jnp.log(-jnp.expm1(-dt))
    gd_norm_w = 1.0 + 0.02 * nrm(ks[14], (DEPTH, GD_DV), f32)
    w_out = nrm(ks[15], (DEPTH, D_MIX, D_MODEL), f32) * (D_MIX ** -0.5)
    final_norm_w = 1.0 + 0.02 * nrm(ks[16], (D_MODEL,), f32)
    return {"x": x, "norm_w": norm_w, "w_in": w_in, "rg_conv_w": rg_conv_w, "rg_conv_b": rg_conv_b,
            "rg_gate_w": rg_gate_w, "rg_gate_b": rg_gate_b, "rg_lambda": rg_lambda,
            "ml_gate_b": ml_gate_b, "ml_norm_w": ml_norm_w, "gd_conv_w": gd_conv_w,
            "gd_a_log": gd_a_log, "gd_dt_bias": gd_dt_bias, "gd_norm_w": gd_norm_w,
            "w_out": w_out, "final_norm_w": final_norm_w}


def reference(x, norm_w, w_in, rg_conv_w, rg_conv_b, rg_gate_w, rg_gate_b, rg_lambda,
              ml_gate_b, ml_norm_w, gd_conv_w, gd_a_log, gd_dt_bias, gd_norm_w,
              w_out, final_norm_w):
    split_idx = [int(c) for c in np.cumsum(IN_SIZES)[:-1]]
    for l in range(DEPTH):
        hn = rmsnorm(x, norm_w[l])
        proj = (hn @ w_in[l]).astype(jnp.float32)
        (rg_x, rg_z, ml_q, ml_k, ml_v, ml_o, ml_z, ml_i, ml_f,
         gd_q, gd_k, gd_v, gd_z, gd_a, gd_b) = jnp.split(proj, split_idx, axis=-1)
        y_rg = rglru_branch(rg_x, rg_z, rg_conv_w[l], rg_conv_b[l], rg_gate_w[l], rg_gate_b[l], rg_lambda[l])
        y_ml = mlstm_branch(ml_q, ml_k, ml_v, ml_o, ml_z, ml_i, ml_f, ml_gate_b[l], ml_norm_w[l])
        y_gd = gdn_branch(gd_q, gd_k, gd_v, gd_z, gd_a, gd_b, gd_conv_w[l], gd_a_log[l],
                          gd_dt_bias[l], gd_norm_w[l])
        y = jnp.concatenate([y_rg, y_ml, y_gd], axis=-1)
        x = x + y.astype(x.dtype) @ w_out[l]
    return rmsnorm(x, final_norm_w)
```

```python
import functools

import jax
import jax.numpy as jnp
from jax import lax
from jax.experimental import pallas as pl
from jax.experimental.pallas import tpu as pltpu

F32 = jnp.float32
BF16 = jnp.bfloat16
HIGHEST = lax.Precision.HIGHEST

D_MODEL = 1024
CONV_K = 4
NORM_EPS = 1e-6
RG_WIDTH = 512
RG_BLOCKS = 8
RG_BLOCK = RG_WIDTH // RG_BLOCKS
RG_C = 8.0
ML_HEADS = 4
ML_DH = 128
ML_WIDTH = ML_HEADS * ML_DH
GD_HEADS = 4
GD_DK = 128
GD_DV = 128
GD_QK = GD_HEADS * GD_DK
GD_WIDTH = GD_HEADS * GD_DV
D_MIX = RG_WIDTH + ML_WIDTH + GD_WIDTH
CHUNK = 64

W = 512
OFF_RG_X, OFF_RG_Z = 0 * W, 1 * W
OFF_ML_Q, OFF_ML_K, OFF_ML_V, OFF_ML_O, OFF_ML_Z = 2 * W, 3 * W, 4 * W, 5 * W, 6 * W
OFF_GD_Q, OFF_GD_Z = 7 * W, 10 * W
OFF_GATES = 11 * W
LANES = 128
D_PROJ = OFF_GATES + LANES
LANE_ML_I, LANE_ML_F, LANE_GD_A, LANE_GD_B = 0, 4, 8, 12
SRC_ML_GATES = 2 * RG_WIDTH + 5 * ML_WIDTH
SRC_GD = SRC_ML_GATES + 2 * ML_HEADS
SRC_GD_GATES = SRC_GD + 2 * GD_QK + 2 * GD_WIDTH

VMEM_LIMIT = 56 * 1024 * 1024


def _dot(a, b):
    return jnp.dot(a.astype(BF16), b.astype(BF16), preferred_element_type=F32)


def _dot_nt(a, b):
    return lax.dot_general(a.astype(BF16), b.astype(BF16), (((1,), (1,)), ((), ())),
                           preferred_element_type=F32)


def _dot_tn(a, b):
    return lax.dot_general(a.astype(BF16), b.astype(BF16), (((0,), (0,)), ((), ())),
                           preferred_element_type=F32)


def _dot_f32(a, b):
    return jnp.dot(a, b, precision=HIGHEST, preferred_element_type=F32)


def _dot_nt_f32(a, b):
    return lax.dot_general(a, b, (((1,), (1,)), ((), ())), precision=HIGHEST,
                           preferred_element_type=F32)


def _softplus(x):
    return jnp.maximum(x, 0.0) + jnp.log1p(jnp.exp(-jnp.abs(x)))


def _sigmoid(x):
    return 1.0 / (1.0 + jnp.exp(-x))


def _silu(x):
    return x * _sigmoid(x)


def _inproj_kernel(x_ref, nw_ref, w_ref, p_ref):
    x = x_ref[...]
    r = lax.rsqrt(jnp.mean(x * x, axis=-1, keepdims=True) + NORM_EPS)
    hn = ((x * r) * nw_ref[...]).astype(BF16)
    for j in range(0, OFF_GATES, W):
        p_ref[:, j:j + W] = jnp.dot(hn, w_ref[:, j:j + W], preferred_element_type=F32)
    p_ref[:, OFF_GATES:] = jnp.dot(hn, w_ref[:, OFF_GATES:], preferred_element_type=F32)


def _inproj(x2, norm_w, w_big, tm):
    t = x2.shape[0]
    return pl.pallas_call(
        _inproj_kernel,
        grid=(t // tm,),
        in_specs=[pl.BlockSpec((tm, D_MODEL), lambda i: (i, 0)),
                  pl.BlockSpec((1, D_MODEL), lambda i: (0, 0)),
                  pl.BlockSpec((D_MODEL, D_PROJ), lambda i: (0, 0))],
        out_specs=pl.BlockSpec((tm, D_PROJ), lambda i: (i, 0)),
        out_shape=jax.ShapeDtypeStruct((t, D_PROJ), F32),
        compiler_params=pltpu.CompilerParams(dimension_semantics=("parallel",),
                                             vmem_limit_bytes=VMEM_LIMIT),
        name="inproj",
    )(x2, norm_w, w_big)


def _outproj_kernel(y_ref, w_ref, x_ref, fw_ref, o_ref, *, final_norm):
    xn = x_ref[...] + jnp.dot(y_ref[...], w_ref[...], preferred_element_type=F32)
    if final_norm:
        r = lax.rsqrt(jnp.mean(xn * xn, axis=-1, keepdims=True) + NORM_EPS)
        xn = (xn * r) * fw_ref[...]
    o_ref[...] = xn


def _outproj(y2, w_out, x2, final_w, tm, final_norm):
    t = x2.shape[0]
    return pl.pallas_call(
        functools.partial(_outproj_kernel, final_norm=final_norm),
        grid=(t // tm,),
        in_specs=[pl.BlockSpec((tm, D_MIX), lambda i: (i, 0)),
                  pl.BlockSpec((D_MIX, D_MODEL), lambda i: (0, 0)),
                  pl.BlockSpec((tm, D_MODEL), lambda i: (i, 0)),
                  pl.BlockSpec((1, D_MODEL), lambda i: (0, 0))],
        out_specs=pl.BlockSpec((tm, D_MODEL), lambda i: (i, 0)),
        out_shape=jax.ShapeDtypeStruct((t, D_MODEL), F32),
        compiler_params=pltpu.CompilerParams(dimension_semantics=("parallel",),
                                             vmem_limit_bytes=VMEM_LIMIT),
        name="outproj",
    )(y2, w_out, x2, final_w)


def _causal_conv(xe, cw_ref, n):
    acc = None
    for k in range(CONV_K):
        j = CONV_K - 1 - k
        xs = xe[8:, :] if j == 0 else pltpu.roll(xe, j, 0)[8:, :]
        term = xs * cw_ref[k:k + 1, :]
        acc = term if acc is None else acc + term
    return acc


def _mixer_kernel(p_ref, rgp_ref, rgw_ref, gp_ref, mlnw_ref, gdcw_ref, gdnw_ref, y_ref,
                  rg_tail, rg_h, ml_c, ml_m, gd_tail, gd_s):
    L = CHUNK

    @pl.when(pl.program_id(1) == 0)
    def _():
        rg_tail[...] = jnp.zeros_like(rg_tail)
        rg_h[...] = jnp.zeros_like(rg_h)
        ml_c[...] = jnp.zeros_like(ml_c)
        ml_m[...] = jnp.zeros_like(ml_m)
        gd_tail[...] = jnp.zeros_like(gd_tail)
        gd_s[...] = jnp.zeros_like(gd_s)

    row = lax.broadcasted_iota(jnp.int32, (L, L), 0)
    col = lax.broadcasted_iota(jnp.int32, (L, L), 1)
    incl = row >= col
    strict = row > col
    tril = incl.astype(F32)
    eye = (row == col).astype(F32)

    xb = p_ref[:, OFF_RG_X:OFF_RG_X + RG_WIDTH]
    zb = p_ref[:, OFF_RG_Z:OFF_RG_Z + RG_WIDTH]
    xe = jnp.concatenate([rg_tail[...], xb], axis=0)
    rg_tail[...] = xb[L - 8:, :]
    xc = _causal_conv(xe, rgp_ref, L) + rgp_ref[4:5, :]
    xcb = xc.astype(BF16)
    half = RG_WIDTH // 2
    g0 = jnp.dot(xcb[:, :half], rgw_ref[0], preferred_element_type=F32)
    g1 = jnp.dot(xcb[:, half:], rgw_ref[1], preferred_element_type=F32)
    gr = jnp.concatenate([g0[:, :half], g1[:, :half]], axis=1) + rgp_ref[5:6, :]
    gi = jnp.concatenate([g0[:, half:], g1[:, half:]], axis=1) + rgp_ref[6:7, :]
    rgate = _sigmoid(gr)
    igate = _sigmoid(gi)
    log_a = (-RG_C * rgate) * _softplus(-rgp_ref[7:8, :])
    a_run = jnp.exp(log_a)
    b_run = jnp.sqrt(-jnp.tanh(log_a) * (a_run * a_run + 1.0)) * (igate * xc)
    rows_c = lax.broadcasted_iota(jnp.int32, (L, RG_WIDTH), 0)
    d = 1
    while d < L:
        a_sh = pltpu.roll(a_run, d, 0)
        b_sh = pltpu.roll(b_run, d, 0)
        valid = rows_c >= d
        b_run = jnp.where(valid, a_run * b_sh + b_run, b_run)
        a_run = jnp.where(valid, a_run * a_sh, a_run)
        d *= 2
    h_rg = b_run + a_run * rg_h[0:1, :]
    rg_h[...] = jnp.broadcast_to(h_rg[L - 1:L, :], rg_h.shape)
    y_ref[:, 0:RG_WIDTH] = (h_rg * _silu(zb)).astype(y_ref.dtype)

    gates = p_ref[:, OFF_GATES:OFF_GATES + LANES] + gp_ref[0:1, :]
    lane = lax.broadcasted_iota(jnp.int32, (L, LANES), 1)
    log_f = -_softplus(-gates)
    g_gd = -jnp.exp(gp_ref[1:2, :]) * _softplus(gates)
    is_f = (lane >= LANE_ML_F) & (lane < LANE_ML_F + ML_HEADS)
    is_a = (lane >= LANE_GD_A) & (lane < LANE_GD_A + GD_HEADS)
    csum = _dot_f32(tril, jnp.where(is_f, log_f, jnp.where(is_a, g_gd, 0.0)))
    beta_all = _sigmoid(gates)
    sel = (lax.broadcasted_iota(jnp.int32, (16, LANES), 0)
           == lax.broadcasted_iota(jnp.int32, (16, LANES), 1)).astype(F32)
    gates_t = _dot_nt_f32(sel, gates)
    csum_t = _dot_nt_f32(sel, csum)

    ones_blk = jnp.ones((L, ML_DH), F32)
    for h in range(ML_HEADS):
        sl = slice(h * ML_DH, (h + 1) * ML_DH)
        q = p_ref[:, OFF_ML_Q + h * ML_DH:OFF_ML_Q + (h + 1) * ML_DH] * (ML_DH ** -0.5)
        k = p_ref[:, OFF_ML_K + h * ML_DH:OFF_ML_K + (h + 1) * ML_DH]
        v = p_ref[:, OFF_ML_V + h * ML_DH:OFF_ML_V + (h + 1) * ML_DH]
        o_pre = p_ref[:, OFF_ML_O + h * ML_DH:OFF_ML_O + (h + 1) * ML_DH]
        z = p_ref[:, OFF_ML_Z + h * ML_DH:OFF_ML_Z + (h + 1) * ML_DH]
        li_c = gates[:, LANE_ML_I + h:LANE_ML_I + h + 1]
        bq_c = csum[:, LANE_ML_F + h:LANE_ML_F + h + 1]
        r_row = (gates_t[LANE_ML_I + h:LANE_ML_I + h + 1, :]
                 - csum_t[LANE_ML_F + h:LANE_ML_F + h + 1, :])
        dm = jnp.where(incl, bq_c + r_row, -1e30)
        m_prev = ml_m[h][0:1, 0:1]
        m_inter = bq_c + m_prev
        m_t = jnp.maximum(m_inter, jnp.max(dm, axis=-1, keepdims=True))
        p = jnp.exp(dm - m_t)
        s = _dot_nt(q, k) * p
        sc = jnp.exp(m_inter - m_t)
        v_aug = jnp.concatenate([v, ones_blk], axis=1)
        c_aug = ml_c[h]
        na = _dot(s, v_aug) + sc * _dot(q, c_aug)
        num = na[:, :ML_DH]
        den = na[:, ML_DH:]
        hh = num / jnp.maximum(jnp.abs(den), jnp.exp(-m_t))
        g_last = bq_c[L - 1:L, :]
        ws = g_last - bq_c + li_c
        m_new = jnp.maximum(g_last + m_prev, jnp.max(ws, axis=0, keepdims=True))
        dec = jnp.exp(g_last + m_prev - m_new)
        wk = jnp.exp(ws - m_new)
        ml_c[h] = dec * c_aug + _dot_tn(k * wk, v_aug)
        ml_m[h] = jnp.broadcast_to(m_new, ml_m.shape[1:])
        hn = hh * lax.rsqrt(jnp.mean(hh * hh, axis=-1, keepdims=True) + NORM_EPS)
        hn = hn * mlnw_ref[0:1, sl]
        y_ml = hn * _sigmoid(o_pre) * _silu(z)
        y_ref[:, RG_WIDTH + h * ML_DH:RG_WIDTH + (h + 1) * ML_DH] = y_ml.astype(y_ref.dtype)

    qkv_in = p_ref[:, OFF_GD_Q:OFF_GD_Q + 2 * GD_QK + GD_WIDTH]
    xe = jnp.concatenate([gd_tail[...], qkv_in], axis=0)
    gd_tail[...] = qkv_in[L - 8:, :]
    qkv = _silu(_causal_conv(xe, gdcw_ref, L))
    for h in range(GD_HEADS):
        q = qkv[:, h * GD_DK:(h + 1) * GD_DK]
        k = qkv[:, GD_QK + h * GD_DK:GD_QK + (h + 1) * GD_DK]
        v = qkv[:, 2 * GD_QK + h * GD_DV:2 * GD_QK + (h + 1) * GD_DV]
        z = p_ref[:, OFF_GD_Z + h * GD_DV:OFF_GD_Z + (h + 1) * GD_DV]
        q = q * lax.rsqrt(jnp.sum(q * q, axis=-1, keepdims=True) + NORM_EPS) * (GD_DK ** -0.5)
        k = k * lax.rsqrt(jnp.sum(k * k, axis=-1, keepdims=True) + NORM_EPS)
        beta = beta_all[:, LANE_GD_B + h:LANE_GD_B + h + 1]
        gc_c = csum[:, LANE_GD_A + h:LANE_GD_A + h + 1]
        gc_r = csum_t[LANE_GD_A + h:LANE_GD_A + h + 1, :]
        gam = jnp.where(incl, jnp.exp(jnp.where(incl, gc_c - gc_r, 0.0)), 0.0)
        kb = k * beta
        neg_m = jnp.where(strict, -(_dot_nt(kb, k) * gam), 0.0)
        t_inv = eye + neg_m
        pw = neg_m
        span = 2
        while span < L:
            pw = _dot(pw, pw)
            t_inv = t_inv + _dot(t_inv, pw)
            span *= 2
        egc = jnp.exp(gc_c)
        u = _dot(t_inv, v * beta)
        w = _dot(t_inv, kb * egc)
        aqk = _dot_nt(q, k) * gam
        q_dec = q * egc
        g_last = gc_c[L - 1:L, :]
        k_dec = k * jnp.exp(g_last - gc_c)
        state = gd_s[h]
        v_new = u - _dot(w, state)
        o = _dot(q_dec, state) + _dot(aqk, v_new)
        gd_s[h] = state * jnp.exp(g_last) + _dot_tn(k_dec, v_new)
        on = o * lax.rsqrt(jnp.mean(o * o, axis=-1, keepdims=True) + NORM_EPS)
        y_gd = on * gdnw_ref[0:1, :] * _silu(z)
        y_ref[:, RG_WIDTH + ML_WIDTH + h * GD_DV:RG_WIDTH + ML_WIDTH + (h + 1) * GD_DV] = (
            y_gd.astype(y_ref.dtype))


def _mixers(p3, rgp, rgw, gp, mlnw, gdcw, gdnw):
    b, s, _ = p3.shape
    full = lambda shape: pl.BlockSpec(shape, lambda i, j: (0,) * len(shape))
    return pl.pallas_call(
        _mixer_kernel,
        grid=(b, s // CHUNK),
        in_specs=[pl.BlockSpec((None, CHUNK, D_PROJ), lambda i, j: (i, j, 0)),
                  full(rgp.shape), full(rgw.shape), full(gp.shape), full(mlnw.shape),
                  full(gdcw.shape), full(gdnw.shape)],
        out_specs=pl.BlockSpec((None, CHUNK, D_MIX), lambda i, j: (i, j, 0)),
        out_shape=jax.ShapeDtypeStruct((b, s, D_MIX), BF16),
        scratch_shapes=[pltpu.VMEM((8, RG_WIDTH), F32),
                        pltpu.VMEM((8, RG_WIDTH), F32),
                        pltpu.VMEM((ML_HEADS, ML_DH, 2 * ML_DH), F32),
                        pltpu.VMEM((ML_HEADS, 8, LANES), F32),
                        pltpu.VMEM((8, 2 * GD_QK + GD_WIDTH), F32),
                        pltpu.VMEM((GD_HEADS, GD_DK, GD_DV), F32)],
        compiler_params=pltpu.CompilerParams(dimension_semantics=("parallel", "arbitrary"),
                                             vmem_limit_bytes=VMEM_LIMIT),
        name="mixers",
    )(p3, rgp, rgw, gp, mlnw, gdcw, gdnw)


def _rearrange_w_in(w_in):
    pad = jnp.zeros((w_in.shape[0], LANES - 4 * ML_HEADS), w_in.dtype)
    return jnp.concatenate([
        w_in[:, :SRC_ML_GATES], w_in[:, SRC_GD:SRC_GD_GATES],
        w_in[:, SRC_ML_GATES:SRC_GD], w_in[:, SRC_GD_GATES:], pad], axis=1)


def _rg_gate_blockdiag(gate_w):
    nb = RG_BLOCKS // 2
    halves = []
    for hf in range(2):
        blk = gate_w[:, hf * nb:(hf + 1) * nb]
        bd = jnp.einsum('gncd,nm->gncmd', blk, jnp.eye(nb, dtype=gate_w.dtype))
        bd = bd.reshape(2, nb * RG_BLOCK, nb * RG_BLOCK)
        halves.append(jnp.concatenate([bd[0], bd[1]], axis=1))
    return jnp.stack(halves)


def _lane_row(pieces):
    row = jnp.zeros((LANES,), F32)
    for off, vals in pieces:
        row = lax.dynamic_update_slice(row, vals.astype(F32), (off,))
    return row


def kernel(x, norm_w, w_in, rg_conv_w, rg_conv_b, rg_gate_w, rg_gate_b, rg_lambda, ml_gate_b,
           ml_norm_w, gd_conv_w, gd_a_log, gd_dt_bias, gd_norm_w, w_out, final_norm_w):
    bsz, seq, _ = x.shape
    depth = w_in.shape[0]
    t = bsz * seq
    x2 = x.reshape(t, D_MODEL)
    for l in range(depth):
        w_big = _rearrange_w_in(w_in[l]).astype(BF16)
        rgp = jnp.concatenate([rg_conv_w[l], rg_conv_b[l][None], rg_gate_b[l], rg_lambda[l][None]],
                              axis=0)
        rgw = _rg_gate_blockdiag(rg_gate_w[l]).astype(BF16)
        gp = jnp.zeros((8, LANES), F32)
        gp = gp.at[0].set(_lane_row([(LANE_ML_I, ml_gate_b[l, 0]), (LANE_ML_F, ml_gate_b[l, 1]),
                                     (LANE_GD_A, gd_dt_bias[l])]))
        gp = gp.at[1].set(_lane_row([(LANE_GD_A, gd_a_log[l])]))
        p = _inproj(x2, norm_w[l][None], w_big, 256)
        y = _mixers(p.reshape(bsz, seq, D_PROJ), rgp, rgw, gp, ml_norm_w[l][None],
                    gd_conv_w[l], gd_norm_w[l][None])
        x2 = _outproj(y.reshape(t, D_MIX), w_out[l].astype(BF16), x2, final_norm_w[None],
                      512, l == depth - 1)
    return x2.reshape(bsz, seq, D_MODEL)
```

```python
import functools

import jax
import jax.numpy as jnp
from jax import lax
from jax.experimental import pallas as pl
from jax.experimental.pallas import tpu as pltpu

F32 = jnp.float32
BF16 = jnp.bfloat16

D_MODEL = 1024
CONV_K = 4
NORM_EPS = 1e-6
RG_WIDTH = 512
RG_BLOCKS = 8
RG_BLOCK = RG_WIDTH // RG_BLOCKS
RG_C = 8.0
ML_HEADS = 4
ML_DH = 128
ML_WIDTH = ML_HEADS * ML_DH
GD_HEADS = 4
GD_DK = 128
GD_DV = 128
GD_QK = GD_HEADS * GD_DK
GD_WIDTH = GD_HEADS * GD_DV
D_MIX = RG_WIDTH + ML_WIDTH + GD_WIDTH
GD_CHUNK = 64
PAIR = 2 * GD_CHUNK
SUBLANES = 8
LANES = 128

W = 512
OFF_RG_X, OFF_RG_Z = 0 * W, 1 * W
OFF_ML_Q, OFF_ML_K, OFF_ML_V, OFF_ML_O, OFF_ML_Z = 2 * W, 3 * W, 4 * W, 5 * W, 6 * W
OFF_GD_Q, OFF_GD_Z = 7 * W, 10 * W
OFF_GATES = 11 * W
D_PROJ = OFF_GATES + LANES
LANE_ML_I, LANE_ML_F, LANE_GD_A, LANE_GD_B = 0, 4, 8, 12
GATE_LANES = 16
SRC_ML_GATES = 2 * RG_WIDTH + 5 * ML_WIDTH
SRC_GD = SRC_ML_GATES + 2 * ML_HEADS
SRC_GD_GATES = SRC_GD + 2 * GD_QK + 2 * GD_WIDTH

VMEM_LIMIT = 56 * 1024 * 1024
TS_MIX = 256


def _dot(a, b):
    return jnp.dot(a.astype(BF16), b.astype(BF16), preferred_element_type=F32)


def _dot_nt(a, b):
    return lax.dot_general(a.astype(BF16), b.astype(BF16), (((1,), (1,)), ((), ())),
                           preferred_element_type=F32)


def _dot_tn(a, b):
    return lax.dot_general(a.astype(BF16), b.astype(BF16), (((0,), (0,)), ((), ())),
                           preferred_element_type=F32)


def _split3(x):
    hi = x.astype(BF16)
    r1 = x - hi.astype(F32)
    mid = r1.astype(BF16)
    lo = (r1 - mid.astype(F32)).astype(BF16)
    return hi, mid, lo


def _softplus(x):
    return jnp.maximum(x, 0.0) + jnp.log1p(jnp.exp(-jnp.abs(x)))


def _sigmoid(x):
    return 1.0 / (1.0 + jnp.exp(-x))


def _silu(x):
    return x * _sigmoid(x)


def _inproj_kernel(x_ref, nw_ref, w_ref, p_ref):
    x = x_ref[...]
    r = lax.rsqrt(jnp.mean(x * x, axis=-1, keepdims=True) + NORM_EPS)
    hn = ((x * r) * nw_ref[...]).astype(BF16)
    for j in range(0, OFF_GATES, W):
        p_ref[:, j:j + W] = jnp.dot(hn, w_ref[:, j:j + W], preferred_element_type=F32)
    p_ref[:, OFF_GATES:] = jnp.dot(hn, w_ref[:, OFF_GATES:], preferred_element_type=F32)


def _inproj(x2, norm_w, w_big, tm):
    t = x2.shape[0]
    return pl.pallas_call(
        _inproj_kernel,
        grid=(t // tm,),
        in_specs=[pl.BlockSpec((tm, D_MODEL), lambda i: (i, 0)),
                  pl.BlockSpec((1, D_MODEL), lambda i: (0, 0)),
                  pl.BlockSpec((D_MODEL, D_PROJ), lambda i: (0, 0))],
        out_specs=pl.BlockSpec((tm, D_PROJ), lambda i: (i, 0)),
        out_shape=jax.ShapeDtypeStruct((t, D_PROJ), F32),
        compiler_params=pltpu.CompilerParams(dimension_semantics=("parallel",),
                                             vmem_limit_bytes=VMEM_LIMIT),
        name="inproj",
    )(x2, norm_w, w_big)


def _outproj_kernel(y_ref, w_ref, x_ref, fw_ref, o_ref, *, final_norm):
    xn = x_ref[...] + jnp.dot(y_ref[...], w_ref[...], preferred_element_type=F32)
    if final_norm:
        r = lax.rsqrt(jnp.mean(xn * xn, axis=-1, keepdims=True) + NORM_EPS)
        xn = (xn * r) * fw_ref[...]
    o_ref[...] = xn


def _outproj(y2, w_out, x2, final_w, tm, final_norm):
    t = x2.shape[0]
    return pl.pallas_call(
        functools.partial(_outproj_kernel, final_norm=final_norm),
        grid=(t // tm,),
        in_specs=[pl.BlockSpec((tm, D_MIX), lambda i: (i, 0)),
                  pl.BlockSpec((D_MIX, D_MODEL), lambda i: (0, 0)),
                  pl.BlockSpec((tm, D_MODEL), lambda i: (i, 0)),
                  pl.BlockSpec((1, D_MODEL), lambda i: (0, 0))],
        out_specs=pl.BlockSpec((tm, D_MODEL), lambda i: (i, 0)),
        out_shape=jax.ShapeDtypeStruct((t, D_MODEL), F32),
        compiler_params=pltpu.CompilerParams(dimension_semantics=("parallel",),
                                             vmem_limit_bytes=VMEM_LIMIT),
        name="outproj",
    )(y2, w_out, x2, final_w)


def _causal_conv(xe, cw_ref):
    acc = None
    for k in range(CONV_K):
        j = CONV_K - 1 - k
        xs = xe[SUBLANES:, :] if j == 0 else pltpu.roll(xe, j, 0)[SUBLANES:, :]
        term = xs * cw_ref[k:k + 1, :]
        acc = term if acc is None else acc + term
    return acc


def _rglru(p_ref, rgp_ref, rgw_ref, y_ref, rg_tail, rg_h):
    ts = p_ref.shape[0]
    xb = p_ref[:, OFF_RG_X:OFF_RG_X + RG_WIDTH]
    zb = p_ref[:, OFF_RG_Z:OFF_RG_Z + RG_WIDTH]
    xe = jnp.concatenate([rg_tail[...], xb], axis=0)
    rg_tail[...] = xb[ts - SUBLANES:, :]
    xc = _causal_conv(xe, rgp_ref) + rgp_ref[4:5, :]
    xcb = xc.astype(BF16)
    half = RG_WIDTH // 2
    g0 = jnp.dot(xcb[:, :half], rgw_ref[0], preferred_element_type=F32)
    g1 = jnp.dot(xcb[:, half:], rgw_ref[1], preferred_element_type=F32)
    gr = jnp.concatenate([g0[:, :half], g1[:, :half]], axis=1) + rgp_ref[5:6, :]
    gi = jnp.concatenate([g0[:, half:], g1[:, half:]], axis=1) + rgp_ref[6:7, :]
    log_a = (-RG_C * _sigmoid(gr)) * _softplus(-rgp_ref[7:8, :])
    a = jnp.exp(log_a)
    b = jnp.sqrt(-jnp.tanh(log_a) * (a * a + 1.0)) * (_sigmoid(gi) * xc)
    ng = ts // SUBLANES
    a3 = a.reshape(ng, SUBLANES, RG_WIDTH)
    b3 = b.reshape(ng, SUBLANES, RG_WIDTH)
    sub = lax.broadcasted_iota(jnp.int32, (ng, SUBLANES, RG_WIDTH), 1)
    d = 1
    while d < SUBLANES:
        a_sh = pltpu.roll(a3, d, 1)
        b_sh = pltpu.roll(b3, d, 1)
        valid = sub >= d
        b3 = jnp.where(valid, a3 * b_sh + b3, b3)
        a3 = jnp.where(valid, a3 * a_sh, a3)
        d *= 2
    carry = rg_h[0:1, :]
    groups = []
    for g in range(ng):
        hg = b3[g] + a3[g] * carry
        groups.append(hg)
        carry = hg[SUBLANES - 1:SUBLANES, :]
    rg_h[...] = jnp.broadcast_to(carry, rg_h.shape)
    h = jnp.concatenate(groups, axis=0)
    y_ref[:, 0:RG_WIDTH] = (h * _silu(zb)).astype(y_ref.dtype)


def _gate_tables(p_ref, gp_ref, cs_ref):
    ts = p_ref.shape[0]
    gates = p_ref[:, OFF_GATES:OFF_GATES + LANES] + gp_ref[0:1, :]
    lane = lax.broadcasted_iota(jnp.int32, (ts, LANES), 1)
    is_i = lane < LANE_ML_F
    is_f = (lane >= LANE_ML_F) & (lane < LANE_ML_F + ML_HEADS)
    is_a = (lane >= LANE_GD_A) & (lane < LANE_GD_A + GD_HEADS)
    log_f = -_softplus(-gates)
    g_gd = -jnp.exp(gp_ref[1:2, :]) * _softplus(gates)
    beta = _sigmoid(gates)
    x = jnp.where(is_f, log_f, jnp.where(is_a, g_gd, 0.0))
    hi, mid, lo = _split3(x)
    packed = (hi.astype(F32) + pltpu.roll(mid.astype(F32), GATE_LANES, 1)
              + pltpu.roll(lo.astype(F32), 2 * GATE_LANES, 1))
    cs = jnp.dot(cs_ref[...], packed.astype(BF16), preferred_element_type=F32)
    cs = cs + pltpu.roll(cs, LANES - GATE_LANES, 1) + pltpu.roll(cs, LANES - 2 * GATE_LANES, 1)
    csum_gd = cs[:ts]
    csum_ml = cs[ts:]
    z = jnp.where(is_i, gates, jnp.where(is_f, csum_ml, jnp.where(is_a, csum_gd, 0.0)))
    zs = jnp.concatenate(_split3(z), axis=0)
    sel = (lax.broadcasted_iota(jnp.int32, (GATE_LANES, LANES), 0)
           == lax.broadcasted_iota(jnp.int32, (GATE_LANES, LANES), 1)).astype(BF16)
    zt = lax.dot_general(sel, zs, (((1,), (1,)), ((), ())), preferred_element_type=F32)
    rows = zt[:, :ts] + zt[:, ts:2 * ts] + zt[:, 2 * ts:]
    return gates, beta, csum_gd, csum_ml, rows


def _mlstm(p_ref, mlnw_ref, y_ref, ml_c, ml_m, gates, csum_ml, rows):
    ts = p_ref.shape[0]
    L = PAIR
    r_i = lax.broadcasted_iota(jnp.int32, (L, L), 0)
    c_i = lax.broadcasted_iota(jnp.int32, (L, L), 1)
    causal = r_i >= c_i
    ones_blk = jnp.ones((L, ML_DH), F32)
    units = [(h, c) for c in range(ts // L) for h in range(ML_HEADS)]

    st = {}
    for h, c in units:
        rs = slice(c * L, (c + 1) * L)
        q = p_ref[rs, OFF_ML_Q + h * ML_DH:OFF_ML_Q + (h + 1) * ML_DH] * (ML_DH ** -0.5)
        k = p_ref[rs, OFF_ML_K + h * ML_DH:OFF_ML_K + (h + 1) * ML_DH]
        li_c = gates[rs, LANE_ML_I + h:LANE_ML_I + h + 1]
        bq_c = csum_ml[rs, LANE_ML_F + h:LANE_ML_F + h + 1]
        r_row = (rows[LANE_ML_I + h:LANE_ML_I + h + 1, rs]
                 - rows[LANE_ML_F + h:LANE_ML_F + h + 1, rs])
        dm = jnp.where(causal, bq_c + r_row, -1e30)
        mx = jnp.max(dm, axis=-1, keepdims=True)
        g_last = bq_c[L - 1:L, :]
        ws = g_last - bq_c + li_c
        mws = jnp.max(ws, axis=0, keepdims=True)
        st[h, c] = dict(q=q, k=k, bq_c=bq_c, mx=mx, g_last=g_last, mws=mws,
                        pm=jnp.exp(dm - mx), wk=jnp.exp(ws - mws), qk=_dot_nt(q, k))
    for h, c in units:
        u = st[h, c]
        rs = slice(c * L, (c + 1) * L)
        v = p_ref[rs, OFF_ML_V + h * ML_DH:OFF_ML_V + (h + 1) * ML_DH]
        v_aug = jnp.concatenate([v, ones_blk], axis=1)
        u["intra"] = _dot(u["qk"] * u["pm"], v_aug)
        u["kv"] = _dot_tn(u["k"] * u["wk"], v_aug)

    c_aug = [ml_c[h] for h in range(ML_HEADS)]
    m_prev = [ml_m[h][0:1, 0:1] for h in range(ML_HEADS)]
    outs = [[] for _ in range(ML_HEADS)]
    for c in range(ts // L):
        qc = [_dot(st[h, c]["q"], c_aug[h]) for h in range(ML_HEADS)]
        for h in range(ML_HEADS):
            u = st[h, c]
            m_inter = u["bq_c"] + m_prev[h]
            m_t = jnp.maximum(m_inter, u["mx"])
            na = jnp.exp(u["mx"] - m_t) * u["intra"] + jnp.exp(m_inter - m_t) * qc[h]
            outs[h].append(na[:, :ML_DH] / jnp.maximum(jnp.abs(na[:, ML_DH:]), jnp.exp(-m_t)))
            m_new = jnp.maximum(u["g_last"] + m_prev[h], u["mws"])
            c_aug[h] = (jnp.exp(u["g_last"] + m_prev[h] - m_new) * c_aug[h]
                        + jnp.exp(u["mws"] - m_new) * u["kv"])
            m_prev[h] = m_new
    for h in range(ML_HEADS):
        sl = slice(h * ML_DH, (h + 1) * ML_DH)
        ml_c[h] = c_aug[h]
        ml_m[h] = jnp.broadcast_to(m_prev[h], ml_m.shape[1:])
        hh = jnp.concatenate(outs[h], axis=0)
        o_pre = p_ref[:, OFF_ML_O + h * ML_DH:OFF_ML_O + (h + 1) * ML_DH]
        z = p_ref[:, OFF_ML_Z + h * ML_DH:OFF_ML_Z + (h + 1) * ML_DH]
        hn = hh * lax.rsqrt(jnp.mean(hh * hh, axis=-1, keepdims=True) + NORM_EPS)
        y_ml = (hn * mlnw_ref[0:1, sl]) * _sigmoid(o_pre) * _silu(z)
        y_ref[:, RG_WIDTH + h * ML_DH:RG_WIDTH + (h + 1) * ML_DH] = y_ml.astype(y_ref.dtype)


def _gdn(p_ref, gdcw_ref, gdnw_ref, y_ref, gd_tail, gd_s, qkv_s, beta_all, csum_gd, rows):
    ts = p_ref.shape[0]
    L = PAIR
    C = GD_CHUNK
    qkv_in = p_ref[:, OFF_GD_Q:OFF_GD_Q + 2 * GD_QK + GD_WIDTH]
    xe = jnp.concatenate([gd_tail[...], qkv_in], axis=0)
    gd_tail[...] = qkv_in[ts - SUBLANES:, :]
    qkv_s[...] = _silu(_causal_conv(xe, gdcw_ref))

    r_i = lax.broadcasted_iota(jnp.int32, (L, L), 0)
    c_i = lax.broadcasted_iota(jnp.int32, (L, L), 1)
    same = (r_i >= C) == (c_i >= C)
    incl = same & (r_i >= c_i)
    strict = same & (r_i > c_i)
    eye = (r_i == c_i).astype(F32)
    second = lax.broadcasted_iota(jnp.int32, (L, 1), 0) >= C
    zeros_c = jnp.zeros((C, GD_DV), F32)
    units = [(h, j) for j in range(ts // L) for h in range(GD_HEADS)]

    st = {}
    for h, j in units:
        rs = slice(j * L, (j + 1) * L)
        q = qkv_s[rs, h * GD_DK:(h + 1) * GD_DK]
        k = qkv_s[rs, GD_QK + h * GD_DK:GD_QK + (h + 1) * GD_DK]
        q = q * lax.rsqrt(jnp.sum(q * q, axis=-1, keepdims=True) + NORM_EPS) * (GD_DK ** -0.5)
        k = k * lax.rsqrt(jnp.sum(k * k, axis=-1, keepdims=True) + NORM_EPS)
        beta = beta_all[rs, LANE_GD_B + h:LANE_GD_B + h + 1]
        gc_c = csum_gd[rs, LANE_GD_A + h:LANE_GD_A + h + 1]
        gc_r = rows[LANE_GD_A + h:LANE_GD_A + h + 1, rs]
        gam = jnp.where(incl, jnp.exp(jnp.where(incl, gc_c - gc_r, 0.0)), 0.0)
        kb = k * beta
        egc = jnp.exp(gc_c)
        g_first = gc_c[C - 1:C, :]
        g_second = gc_c[L - 1:L, :]
        st[h, j] = dict(k=k, kb=kb, beta=beta, gam=gam, egc=egc, q_dec=q * egc,
                        g_last=(g_first, g_second),
                        k_dec=k * jnp.exp(jnp.where(second, g_second, g_first) - gc_c),
                        kq=_dot_nt(jnp.concatenate([kb, q], axis=0), k))
    for key in units:
        u = st[key]
        u["aqk"] = u["kq"][L:] * u["gam"]
        pw = jnp.where(strict, -(u["kq"][:L] * u["gam"]), 0.0)
        u["t_inv"] = eye + pw
        u["pw"] = _dot(pw, pw)
    span = 4
    while span < C:
        for key in units:
            u = st[key]
            both = _dot(jnp.concatenate([u["pw"], u["t_inv"]], axis=0), u["pw"])
            u["pw"] = both[:L]
            u["t_inv"] = u["t_inv"] + both[L:]
        span *= 2
    for key in units:
        u = st[key]
        u["t_inv"] = u["t_inv"] + _dot(u["t_inv"], u["pw"])
    for h, j in units:
        u = st[h, j]
        rs = slice(j * L, (j + 1) * L)
        v = qkv_s[rs, 2 * GD_QK + h * GD_DV:2 * GD_QK + (h + 1) * GD_DV]
        uw = _dot(u["t_inv"], jnp.concatenate([v * u["beta"], u["kb"] * u["egc"]], axis=1))
        u["u"] = uw[:, :GD_DV]
        u["w"] = uw[:, GD_DV:]

    state = [gd_s[h] for h in range(GD_HEADS)]
    outs = [[] for _ in range(GD_HEADS)]
    for j in range(ts // L):
        for half in range(2):
            hs = slice(half * C, (half + 1) * C)
            wq = [_dot(jnp.concatenate([st[h, j]["w"][hs], st[h, j]["q_dec"][hs]], axis=0), state[h])
                  for h in range(GD_HEADS)]
            v_new = [st[h, j]["u"][hs] - wq[h][:C] for h in range(GD_HEADS)]
            for h in range(GD_HEADS):
                v_pad = jnp.concatenate([v_new[h], zeros_c] if half == 0 else [zeros_c, v_new[h]],
                                        axis=0)
                outs[h].append(wq[h][C:] + _dot(st[h, j]["aqk"][hs], v_pad))
            for h in range(GD_HEADS):
                u = st[h, j]
                state[h] = (state[h] * jnp.exp(u["g_last"][half])
                            + _dot_tn(u["k_dec"][hs], v_new[h]))
    for h in range(GD_HEADS):
        gd_s[h] = state[h]
        o = jnp.concatenate(outs[h], axis=0)
        z = p_ref[:, OFF_GD_Z + h * GD_DV:OFF_GD_Z + (h + 1) * GD_DV]
        on = o * lax.rsqrt(jnp.mean(o * o, axis=-1, keepdims=True) + NORM_EPS)
        y_gd = on * gdnw_ref[0:1, :] * _silu(z)
        y_ref[:, RG_WIDTH + ML_WIDTH + h * GD_DV:RG_WIDTH + ML_WIDTH + (h + 1) * GD_DV] = (
            y_gd.astype(y_ref.dtype))


def _mixer_kernel(p_ref, rgp_ref, rgw_ref, gp_ref, mlnw_ref, gdcw_ref, gdnw_ref, cs_ref, y_ref,
                  rg_tail, rg_h, ml_c, ml_m, gd_tail, gd_s, qkv_s):
    @pl.when(pl.program_id(1) == 0)
    def _():
        rg_tail[...] = jnp.zeros_like(rg_tail)
        rg_h[...] = jnp.zeros_like(rg_h)
        ml_c[...] = jnp.zeros_like(ml_c)
        ml_m[...] = jnp.zeros_like(ml_m)
        gd_tail[...] = jnp.zeros_like(gd_tail)
        gd_s[...] = jnp.zeros_like(gd_s)

    _rglru(p_ref, rgp_ref, rgw_ref, y_ref, rg_tail, rg_h)
    gates, beta, csum_gd, csum_ml, rows = _gate_tables(p_ref, gp_ref, cs_ref)
    _mlstm(p_ref, mlnw_ref, y_ref, ml_c, ml_m, gates, csum_ml, rows)
    _gdn(p_ref, gdcw_ref, gdnw_ref, y_ref, gd_tail, gd_s, qkv_s, beta, csum_gd, rows)


def _cumsum_matrix(ts):
    r = jnp.arange(ts)[:, None]
    c = jnp.arange(ts)[None, :]
    m64 = (r >= c) & (r // GD_CHUNK == c // GD_CHUNK)
    m128 = (r >= c) & (r // PAIR == c // PAIR)
    return jnp.concatenate([m64, m128], axis=0).astype(BF16)


def _mixers(p3, rgp, rgw, gp, mlnw, gdcw, gdnw, ts):
    b, s, _ = p3.shape
    cs = _cumsum_matrix(ts)
    full = lambda shape: pl.BlockSpec(shape, lambda i, j: (0,) * len(shape))
    return pl.pallas_call(
        _mixer_kernel,
        grid=(b, s // ts),
        in_specs=[pl.BlockSpec((None, ts, D_PROJ), lambda i, j: (i, j, 0)),
                  full(rgp.shape), full(rgw.shape), full(gp.shape), full(mlnw.shape),
                  full(gdcw.shape), full(gdnw.shape), full(cs.shape)],
        out_specs=pl.BlockSpec((None, ts, D_MIX), lambda i, j: (i, j, 0)),
        out_shape=jax.ShapeDtypeStruct((b, s, D_MIX), BF16),
        scratch_shapes=[pltpu.VMEM((SUBLANES, RG_WIDTH), F32),
                        pltpu.VMEM((SUBLANES, RG_WIDTH), F32),
                        pltpu.VMEM((ML_HEADS, ML_DH, 2 * ML_DH), F32),
                        pltpu.VMEM((ML_HEADS, SUBLANES, LANES), F32),
                        pltpu.VMEM((SUBLANES, 2 * GD_QK + GD_WIDTH), F32),
                        pltpu.VMEM((GD_HEADS, GD_DK, GD_DV), F32),
                        pltpu.VMEM((ts, 2 * GD_QK + GD_WIDTH), F32)],
        compiler_params=pltpu.CompilerParams(dimension_semantics=("parallel", "arbitrary"),
                                             vmem_limit_bytes=VMEM_LIMIT),
        name="mixers",
    )(p3, rgp, rgw, gp, mlnw, gdcw, gdnw, cs)


def _rearrange_w_in(w_in):
    pad = jnp.zeros((w_in.shape[0], LANES - GATE_LANES), w_in.dtype)
    return jnp.concatenate([
        w_in[:, :SRC_ML_GATES], w_in[:, SRC_GD:SRC_GD_GATES],
        w_in[:, SRC_ML_GATES:SRC_GD], w_in[:, SRC_GD_GATES:], pad], axis=1)


def _rg_gate_blockdiag(gate_w):
    nb = RG_BLOCKS // 2
    halves = []
    for hf in range(2):
        blk = gate_w[:, hf * nb:(hf + 1) * nb]
        bd = jnp.einsum('gncd,nm->gncmd', blk, jnp.eye(nb, dtype=gate_w.dtype))
        bd = bd.reshape(2, nb * RG_BLOCK, nb * RG_BLOCK)
        halves.append(jnp.concatenate([bd[0], bd[1]], axis=1))
    return jnp.stack(halves)


def _lane_row(pieces):
    row = jnp.zeros((LANES,), F32)
    for off, vals in pieces:
        row = lax.dynamic_update_slice(row, vals.astype(F32), (off,))
    return row


def kernel(x, norm_w, w_in, rg_conv_w, rg_conv_b, rg_gate_w, rg_gate_b, rg_lambda, ml_gate_b,
           ml_norm_w, gd_conv_w, gd_a_log, gd_dt_bias, gd_norm_w, w_out, final_norm_w):
    bsz, seq, _ = x.shape
    depth = w_in.shape[0]
    t = bsz * seq
    x2 = x.reshape(t, D_MODEL)
    for l in range(depth):
        w_big = _rearrange_w_in(w_in[l]).astype(BF16)
        rgp = jnp.concatenate([rg_conv_w[l], rg_conv_b[l][None], rg_gate_b[l], rg_lambda[l][None]],
                              axis=0)
        rgw = _rg_gate_blockdiag(rg_gate_w[l]).astype(BF16)
        gp = jnp.zeros((SUBLANES, LANES), F32)
        gp = gp.at[0].set(_lane_row([(LANE_ML_I, ml_gate_b[l, 0]), (LANE_ML_F, ml_gate_b[l, 1]),
                                     (LANE_GD_A, gd_dt_bias[l])]))
        gp = gp.at[1].set(_lane_row([(LANE_GD_A, gd_a_log[l])]))
        p = _inproj(x2, norm_w[l][None], w_big, 256)
        y = _mixers(p.reshape(bsz, seq, D_PROJ), rgp, rgw, gp, ml_norm_w[l][None],
                    gd_conv_w[l], gd_norm_w[l][None], TS_MIX)
        x2 = _outproj(y.reshape(t, D_MIX), w_out[l].astype(BF16), x2, final_norm_w[None],
                      512, l == depth - 1)
    return x2.reshape(bsz, seq, D_MODEL)
```

```python
import functools

import jax
import jax.numpy as jnp
from jax import lax
from jax.experimental import pallas as pl
from jax.experimental.pallas import tpu as pltpu

F32 = jnp.float32
BF16 = jnp.bfloat16

D_MODEL = 1024
CONV_K = 4
NORM_EPS = 1e-6
RG_WIDTH = 512
RG_BLOCKS = 8
RG_BLOCK = RG_WIDTH // RG_BLOCKS
RG_C = 8.0
ML_HEADS = 4
ML_DH = 128
ML_WIDTH = ML_HEADS * ML_DH
GD_HEADS = 4
GD_DK = 128
GD_DV = 128
GD_QK = GD_HEADS * GD_DK
GD_WIDTH = GD_HEADS * GD_DV
GD_CONV = 2 * GD_QK + GD_WIDTH
D_MIX = RG_WIDTH + ML_WIDTH + GD_WIDTH
GD_CHUNK = 64
PAIR = 2 * GD_CHUNK
SUBLANES = 8
LANES = 128

W = 512
WCOL_RG_X, WCOL_RG_Z = 0 * W, 1 * W
WCOL_ML_Q, WCOL_ML_K, WCOL_ML_V, WCOL_ML_O, WCOL_ML_Z = 2 * W, 3 * W, 4 * W, 5 * W, 6 * W
WCOL_GD_QKV, WCOL_GD_Z = 7 * W, 10 * W
WCOL_GATES = 11 * W
D_PROJ = WCOL_GATES + LANES
OFF_RG_A, OFF_RG_B, OFF_RG_G = 0 * W, 1 * W, 2 * W
OFF_ML_Q, OFF_ML_K, OFF_ML_V, OFF_ML_G = 3 * W, 4 * W, 5 * W, 6 * W
OFF_GD_Q, OFF_GD_K, OFF_GD_V, OFF_GD_G = 7 * W, 8 * W, 9 * W, 10 * W
OFF_GATES = 11 * W
LANE_ML_I, LANE_ML_F, LANE_GD_A, LANE_GD_B = 0, 4, 8, 12
GATE_LANES = 16
SRC_ML_GATES = 2 * RG_WIDTH + 5 * ML_WIDTH
SRC_GD = SRC_ML_GATES + 2 * ML_HEADS
SRC_GD_GATES = SRC_GD + 2 * GD_QK + 2 * GD_WIDTH

VMEM_LIMIT = 56 * 1024 * 1024
TM_IN = 256
TS_MIX = 256
TM_OUT = 512
MIX_ORDER = "GGGGGGGGGMRGMRGMRGMRGMR"


def _dot(a, b):
    return jnp.dot(a.astype(BF16), b.astype(BF16), preferred_element_type=F32)


def _dot_nt(a, b):
    return lax.dot_general(a.astype(BF16), b.astype(BF16), (((1,), (1,)), ((), ())),
                           preferred_element_type=F32)


def _dot_tn(a, b):
    return lax.dot_general(a.astype(BF16), b.astype(BF16), (((0,), (0,)), ((), ())),
                           preferred_element_type=F32)


def _split3(x):
    hi = x.astype(BF16)
    r1 = x - hi.astype(F32)
    mid = r1.astype(BF16)
    lo = (r1 - mid.astype(F32)).astype(BF16)
    return hi, mid, lo


def _softplus(x):
    return jnp.maximum(x, 0.0) + jnp.log1p(jnp.exp(-jnp.abs(x)))


def _sigmoid(x):
    return 1.0 / (1.0 + jnp.exp(-x))


def _silu(x):
    return x * _sigmoid(x)


def _causal_conv(x_ref, tail_ref, cw_ref):
    ts = x_ref.shape[0]
    head = jnp.concatenate([tail_ref[...], x_ref[0:SUBLANES, :]], axis=0)
    acc = None
    acc0 = None
    for k in range(CONV_K):
        j = CONV_K - 1 - k
        wk = cw_ref[k:k + 1, :]
        term = x_ref[SUBLANES - j:ts - j, :] * wk
        hs = head[SUBLANES:, :] if j == 0 else pltpu.roll(head, j, 0)[SUBLANES:, :]
        term0 = hs * wk
        acc = term if acc is None else acc + term
        acc0 = term0 if acc0 is None else acc0 + term0
    tail_ref[...] = x_ref[ts - SUBLANES:ts, :]
    return jnp.concatenate([acc0, acc], axis=0)


def _inproj_kernel(x_ref, nw_ref, w_ref, rgp_ref, rgw_ref, gp_ref, gdcw_ref, p_ref,
                   raw_rg, raw_gd, rg_tail, gd_tail, *, tiles_per_seq):
    @pl.when(pl.program_id(0) % tiles_per_seq == 0)
    def _():
        rg_tail[...] = jnp.zeros_like(rg_tail)
        gd_tail[...] = jnp.zeros_like(gd_tail)

    x = x_ref[...]
    r = lax.rsqrt(jnp.mean(x * x, axis=-1, keepdims=True) + NORM_EPS)
    hn = ((x * r) * nw_ref[...]).astype(BF16)

    def proj(col, width=W):
        return jnp.dot(hn, w_ref[:, col:col + width], preferred_element_type=F32)

    raw_rg[...] = proj(WCOL_RG_X)
    xc = _causal_conv(raw_rg, rg_tail, rgp_ref) + rgp_ref[4:5, :]
    xcb = xc.astype(BF16)
    half = RG_WIDTH // 2
    g0 = jnp.dot(xcb[:, :half], rgw_ref[0], preferred_element_type=F32)
    g1 = jnp.dot(xcb[:, half:], rgw_ref[1], preferred_element_type=F32)
    gr = jnp.concatenate([g0[:, :half], g1[:, :half]], axis=1) + rgp_ref[5:6, :]
    gi = jnp.concatenate([g0[:, half:], g1[:, half:]], axis=1) + rgp_ref[6:7, :]
    log_a = (-RG_C * _sigmoid(gr)) * _softplus(-rgp_ref[7:8, :])
    a = jnp.exp(log_a)
    p_ref[:, OFF_RG_A:OFF_RG_A + W] = a
    p_ref[:, OFF_RG_B:OFF_RG_B + W] = (jnp.sqrt(-jnp.tanh(log_a) * (a * a + 1.0))
                                       * (_sigmoid(gi) * xc))
    p_ref[:, OFF_RG_G:OFF_RG_G + W] = _silu(proj(WCOL_RG_Z))

    p_ref[:, OFF_ML_Q:OFF_ML_Q + W] = proj(WCOL_ML_Q) * (ML_DH ** -0.5)
    p_ref[:, OFF_ML_K:OFF_ML_K + W] = proj(WCOL_ML_K)
    p_ref[:, OFF_ML_V:OFF_ML_V + W] = proj(WCOL_ML_V)
    p_ref[:, OFF_ML_G:OFF_ML_G + W] = _sigmoid(proj(WCOL_ML_O)) * _silu(proj(WCOL_ML_Z))

    for c in range(0, GD_CONV, W):
        raw_gd[:, c:c + W] = proj(WCOL_GD_QKV + c)
    qkv = _silu(_causal_conv(raw_gd, gd_tail, gdcw_ref))
    for h in range(GD_HEADS):
        q = qkv[:, h * GD_DK:(h + 1) * GD_DK]
        k = qkv[:, GD_QK + h * GD_DK:GD_QK + (h + 1) * GD_DK]
        q = q * lax.rsqrt(jnp.sum(q * q, axis=-1, keepdims=True) + NORM_EPS) * (GD_DK ** -0.5)
        k = k * lax.rsqrt(jnp.sum(k * k, axis=-1, keepdims=True) + NORM_EPS)
        p_ref[:, OFF_GD_Q + h * GD_DK:OFF_GD_Q + (h + 1) * GD_DK] = q
        p_ref[:, OFF_GD_K + h * GD_DK:OFF_GD_K + (h + 1) * GD_DK] = k
    p_ref[:, OFF_GD_V:OFF_GD_V + W] = qkv[:, 2 * GD_QK:]
    p_ref[:, OFF_GD_G:OFF_GD_G + W] = _silu(proj(WCOL_GD_Z))

    gates = proj(WCOL_GATES, LANES) + gp_ref[0:1, :]
    lane = lax.broadcasted_iota(jnp.int32, gates.shape, 1)
    is_i = lane < LANE_ML_F
    is_f = (lane >= LANE_ML_F) & (lane < LANE_ML_F + ML_HEADS)
    is_a = (lane >= LANE_GD_A) & (lane < LANE_GD_A + GD_HEADS)
    log_f = -_softplus(-gates)
    g_gd = -jnp.exp(gp_ref[1:2, :]) * _softplus(gates)
    p_ref[:, OFF_GATES:OFF_GATES + LANES] = jnp.where(
        is_i, gates, jnp.where(is_f, log_f, jnp.where(is_a, g_gd, _sigmoid(gates))))


def _inproj(x2, norm_w, w_big, rgp, rgw, gp, gdcw, seq):
    t = x2.shape[0]
    tm = TM_IN
    full = lambda shape: pl.BlockSpec(shape, lambda i: (0,) * len(shape))
    return pl.pallas_call(
        functools.partial(_inproj_kernel, tiles_per_seq=seq // tm),
        grid=(t // tm,),
        in_specs=[pl.BlockSpec((tm, D_MODEL), lambda i: (i, 0)),
                  full((1, D_MODEL)), full((D_MODEL, D_PROJ)), full(rgp.shape), full(rgw.shape),
                  full(gp.shape), full(gdcw.shape)],
        out_specs=pl.BlockSpec((tm, D_PROJ), lambda i: (i, 0)),
        out_shape=jax.ShapeDtypeStruct((t, D_PROJ), F32),
        scratch_shapes=[pltpu.VMEM((tm, RG_WIDTH), F32),
                        pltpu.VMEM((tm, GD_CONV), F32),
                        pltpu.VMEM((SUBLANES, RG_WIDTH), F32),
                        pltpu.VMEM((SUBLANES, GD_CONV), F32)],
        compiler_params=pltpu.CompilerParams(dimension_semantics=("arbitrary",),
                                             vmem_limit_bytes=VMEM_LIMIT),
        name="inproj",
    )(x2, norm_w, w_big, rgp, rgw, gp, gdcw)


def _outproj_kernel(y_ref, w_ref, x_ref, fw_ref, o_ref, *, final_norm):
    xn = x_ref[...] + jnp.dot(y_ref[...], w_ref[...], preferred_element_type=F32)
    if final_norm:
        r = lax.rsqrt(jnp.mean(xn * xn, axis=-1, keepdims=True) + NORM_EPS)
        xn = (xn * r) * fw_ref[...]
    o_ref[...] = xn


def _outproj(y2, w_out, x2, final_w, final_norm):
    t = x2.shape[0]
    tm = TM_OUT
    return pl.pallas_call(
        functools.partial(_outproj_kernel, final_norm=final_norm),
        grid=(t // tm,),
        in_specs=[pl.BlockSpec((tm, D_MIX), lambda i: (i, 0)),
                  pl.BlockSpec((D_MIX, D_MODEL), lambda i: (0, 0)),
                  pl.BlockSpec((tm, D_MODEL), lambda i: (i, 0)),
                  pl.BlockSpec((1, D_MODEL), lambda i: (0, 0))],
        out_specs=pl.BlockSpec((tm, D_MODEL), lambda i: (i, 0)),
        out_shape=jax.ShapeDtypeStruct((t, D_MODEL), F32),
        compiler_params=pltpu.CompilerParams(dimension_semantics=("parallel",),
                                             vmem_limit_bytes=VMEM_LIMIT),
        name="outproj",
    )(y2, w_out, x2, final_w)


def _rglru(p_ref, y_ref, rg_h):
    ts = p_ref.shape[0]
    ng = ts // SUBLANES
    a3 = p_ref[:, OFF_RG_A:OFF_RG_A + W].reshape(ng, SUBLANES, RG_WIDTH)
    b3 = p_ref[:, OFF_RG_B:OFF_RG_B + W].reshape(ng, SUBLANES, RG_WIDTH)
    sub = lax.broadcasted_iota(jnp.int32, (ng, SUBLANES, RG_WIDTH), 1)
    d = 1
    while d < SUBLANES:
        a_sh = pltpu.roll(a3, d, 1)
        b_sh = pltpu.roll(b3, d, 1)
        valid = sub >= d
        b3 = jnp.where(valid, a3 * b_sh + b3, b3)
        a3 = jnp.where(valid, a3 * a_sh, a3)
        d *= 2
        yield
    carry = rg_h[0:1, :]
    groups = []
    for g in range(ng):
        hg = b3[g] + a3[g] * carry
        groups.append(hg)
        carry = hg[SUBLANES - 1:SUBLANES, :]
    rg_h[...] = jnp.broadcast_to(carry, rg_h.shape)
    yield
    h = jnp.concatenate(groups, axis=0)
    y_ref[:, 0:RG_WIDTH] = (h * p_ref[:, OFF_RG_G:OFF_RG_G + W]).astype(y_ref.dtype)


def _gate_tables(p_ref, cs_ref):
    ts = p_ref.shape[0]
    gates = p_ref[:, OFF_GATES:OFF_GATES + LANES]
    lane = lax.broadcasted_iota(jnp.int32, (ts, LANES), 1)
    is_i = lane < LANE_ML_F
    is_f = (lane >= LANE_ML_F) & (lane < LANE_ML_F + ML_HEADS)
    is_a = (lane >= LANE_GD_A) & (lane < LANE_GD_A + GD_HEADS)
    x = jnp.where(is_f | is_a, gates, 0.0)
    hi, mid, lo = _split3(x)
    packed = (hi.astype(F32) + pltpu.roll(mid.astype(F32), GATE_LANES, 1)
              + pltpu.roll(lo.astype(F32), 2 * GATE_LANES, 1))
    cs = jnp.dot(cs_ref[...], packed.astype(BF16), preferred_element_type=F32)
    cs = cs + pltpu.roll(cs, LANES - GATE_LANES, 1) + pltpu.roll(cs, LANES - 2 * GATE_LANES, 1)
    csum_gd = cs[:ts]
    csum_ml = cs[ts:]
    z = jnp.where(is_i, gates, jnp.where(is_f, csum_ml, jnp.where(is_a, csum_gd, 0.0)))
    zs = jnp.concatenate(_split3(z), axis=0)
    sel = (lax.broadcasted_iota(jnp.int32, (GATE_LANES, LANES), 0)
           == lax.broadcasted_iota(jnp.int32, (GATE_LANES, LANES), 1)).astype(BF16)
    zt = lax.dot_general(sel, zs, (((1,), (1,)), ((), ())), preferred_element_type=F32)
    rows = zt[:, :ts] + zt[:, ts:2 * ts] + zt[:, 2 * ts:]
    return gates, csum_gd, csum_ml, rows


def _mlstm(p_ref, mlnw_ref, y_ref, ml_c, ml_m, gates, csum_ml, rows):
    ts = p_ref.shape[0]
    L = PAIR
    r_i = lax.broadcasted_iota(jnp.int32, (L, L), 0)
    c_i = lax.broadcasted_iota(jnp.int32, (L, L), 1)
    causal = r_i >= c_i
    ones_blk = jnp.ones((L, ML_DH), F32)
    units = [(h, c) for c in range(ts // L) for h in range(ML_HEADS)]

    st = {}
    for h, c in units:
        rs = slice(c * L, (c + 1) * L)
        q = p_ref[rs, OFF_ML_Q + h * ML_DH:OFF_ML_Q + (h + 1) * ML_DH]
        k = p_ref[rs, OFF_ML_K + h * ML_DH:OFF_ML_K + (h + 1) * ML_DH]
        li_c = gates[rs, LANE_ML_I + h:LANE_ML_I + h + 1]
        bq_c = csum_ml[rs, LANE_ML_F + h:LANE_ML_F + h + 1]
        r_row = (rows[LANE_ML_I + h:LANE_ML_I + h + 1, rs]
                 - rows[LANE_ML_F + h:LANE_ML_F + h + 1, rs])
        dm = jnp.where(causal, bq_c + r_row, -1e30)
        mx = jnp.max(dm, axis=-1, keepdims=True)
        g_last = bq_c[L - 1:L, :]
        ws = g_last - bq_c + li_c
        mws = jnp.max(ws, axis=0, keepdims=True)
        st[h, c] = dict(q=q, k=k, bq_c=bq_c, mx=mx, g_last=g_last, mws=mws,
                        pm=jnp.exp(dm - mx), wk=jnp.exp(ws - mws), qk=_dot_nt(q, k))
    yield
    for h, c in units:
        u = st[h, c]
        rs = slice(c * L, (c + 1) * L)
        v = p_ref[rs, OFF_ML_V + h * ML_DH:OFF_ML_V + (h + 1) * ML_DH]
        v_aug = jnp.concatenate([v, ones_blk], axis=1)
        u["intra"] = _dot(u["qk"] * u["pm"], v_aug)
        u["kv"] = _dot_tn(u["k"] * u["wk"], v_aug)

    yield
    c_aug = [ml_c[h] for h in range(ML_HEADS)]
    m_prev = [ml_m[h][0:1, 0:1] for h in range(ML_HEADS)]
    outs = [[] for _ in range(ML_HEADS)]
    for c in range(ts // L):
        qc = [_dot(st[h, c]["q"], c_aug[h]) for h in range(ML_HEADS)]
        for h in range(ML_HEADS):
            u = st[h, c]
            m_inter = u["bq_c"] + m_prev[h]
            m_t = jnp.maximum(m_inter, u["mx"])
            na = jnp.exp(u["mx"] - m_t) * u["intra"] + jnp.exp(m_inter - m_t) * qc[h]
            outs[h].append(na[:, :ML_DH] / jnp.maximum(jnp.abs(na[:, ML_DH:]), jnp.exp(-m_t)))
            m_new = jnp.maximum(u["g_last"] + m_prev[h], u["mws"])
            c_aug[h] = (jnp.exp(u["g_last"] + m_prev[h] - m_new) * c_aug[h]
                        + jnp.exp(u["mws"] - m_new) * u["kv"])
            m_prev[h] = m_new
        yield
    for h in range(ML_HEADS):
        sl = slice(h * ML_DH, (h + 1) * ML_DH)
        ml_c[h] = c_aug[h]
        ml_m[h] = jnp.broadcast_to(m_prev[h], ml_m.shape[1:])
        hh = jnp.concatenate(outs[h], axis=0)
        hn = hh * lax.rsqrt(jnp.mean(hh * hh, axis=-1, keepdims=True) + NORM_EPS)
        y_ml = (hn * mlnw_ref[0:1, sl]) * p_ref[:, OFF_ML_G + h * ML_DH:OFF_ML_G + (h + 1) * ML_DH]
        y_ref[:, RG_WIDTH + h * ML_DH:RG_WIDTH + (h + 1) * ML_DH] = y_ml.astype(y_ref.dtype)


def _gdn(p_ref, gdnw_ref, y_ref, gd_s, gates, csum_gd, rows):
    ts = p_ref.shape[0]
    L = PAIR
    C = GD_CHUNK
    r_i = lax.broadcasted_iota(jnp.int32, (L, L), 0)
    c_i = lax.broadcasted_iota(jnp.int32, (L, L), 1)
    same = (r_i >= C) == (c_i >= C)
    incl = same & (r_i >= c_i)
    strict = same & (r_i > c_i)
    eye = (r_i == c_i).astype(F32)
    second = lax.broadcasted_iota(jnp.int32, (L, 1), 0) >= C
    zeros_c = jnp.zeros((C, GD_DV), F32)
    units = [(h, j) for j in range(ts // L) for h in range(GD_HEADS)]

    st = {}
    for h, j in units:
        rs = slice(j * L, (j + 1) * L)
        q = p_ref[rs, OFF_GD_Q + h * GD_DK:OFF_GD_Q + (h + 1) * GD_DK]
        k = p_ref[rs, OFF_GD_K + h * GD_DK:OFF_GD_K + (h + 1) * GD_DK]
        beta = gates[rs, LANE_GD_B + h:LANE_GD_B + h + 1]
        gc_c = csum_gd[rs, LANE_GD_A + h:LANE_GD_A + h + 1]
        gc_r = rows[LANE_GD_A + h:LANE_GD_A + h + 1, rs]
        gam = jnp.where(incl, jnp.exp(jnp.where(incl, gc_c - gc_r, 0.0)), 0.0)
        kb = k * beta
        egc = jnp.exp(gc_c)
        g_first = gc_c[C - 1:C, :]
        g_second = gc_c[L - 1:L, :]
        st[h, j] = dict(k=k, kb=kb, beta=beta, gam=gam, egc=egc, q_dec=q * egc,
                        g_last=(g_first, g_second),
                        k_dec=k * jnp.exp(jnp.where(second, g_second, g_first) - gc_c),
                        kq=_dot_nt(jnp.concatenate([kb, q], axis=0), k))
    yield
    for key in units:
        u = st[key]
        u["aqk"] = u["kq"][L:] * u["gam"]
        pw = jnp.where(strict, -(u["kq"][:L] * u["gam"]), 0.0)
        u["t_inv"] = eye + pw
        u["pw"] = _dot(pw, pw)
    yield
    span = 4
    while span < C:
        for key in units:
            u = st[key]
            both = _dot(jnp.concatenate([u["pw"], u["t_inv"]], axis=0), u["pw"])
            u["pw"] = both[:L]
            u["t_inv"] = u["t_inv"] + both[L:]
        span *= 2
        yield
    for key in units:
        u = st[key]
        u["t_inv"] = u["t_inv"] + _dot(u["t_inv"], u["pw"])
    yield
    for h, j in units:
        u = st[h, j]
        rs = slice(j * L, (j + 1) * L)
        v = p_ref[rs, OFF_GD_V + h * GD_DV:OFF_GD_V + (h + 1) * GD_DV]
        uw = _dot(u["t_inv"], jnp.concatenate([v * u["beta"], u["kb"] * u["egc"]], axis=1))
        u["u"] = uw[:, :GD_DV]
        u["w"] = uw[:, GD_DV:]

    yield
    state = [gd_s[h] for h in range(GD_HEADS)]
    outs = [[] for _ in range(GD_HEADS)]
    for j in range(ts // L):
        for half in range(2):
            hs = slice(half * C, (half + 1) * C)
            wq = [_dot(jnp.concatenate([st[h, j]["w"][hs], st[h, j]["q_dec"][hs]], axis=0), state[h])
                  for h in range(GD_HEADS)]
            v_new = [st[h, j]["u"][hs] - wq[h][:C] for h in range(GD_HEADS)]
            for h in range(GD_HEADS):
                v_pad = jnp.concatenate([v_new[h], zeros_c] if half == 0 else [zeros_c, v_new[h]],
                                        axis=0)
                outs[h].append(wq[h][C:] + _dot(st[h, j]["aqk"][hs], v_pad))
            for h in range(GD_HEADS):
                u = st[h, j]
                state[h] = (state[h] * jnp.exp(u["g_last"][half])
                            + _dot_tn(u["k_dec"][hs], v_new[h]))
            yield
    for h in range(GD_HEADS):
        gd_s[h] = state[h]
        o = jnp.concatenate(outs[h], axis=0)
        on = o * lax.rsqrt(jnp.mean(o * o, axis=-1, keepdims=True) + NORM_EPS)
        y_gd = on * gdnw_ref[0:1, :] * p_ref[:, OFF_GD_G + h * GD_DV:OFF_GD_G + (h + 1) * GD_DV]
        y_ref[:, RG_WIDTH + ML_WIDTH + h * GD_DV:RG_WIDTH + ML_WIDTH + (h + 1) * GD_DV] = (
            y_gd.astype(y_ref.dtype))


def _interleave(order, stages):
    live = dict(stages)

    def step(key):
        if key in live:
            try:
                next(live[key])
            except StopIteration:
                del live[key]

    for key in order:
        step(key)
    while live:
        for key in list(live):
            step(key)


def _mixer_kernel(p_ref, mlnw_ref, gdnw_ref, cs_ref, y_ref, rg_h, ml_c, ml_m, gd_s):
    @pl.when(pl.program_id(1) == 0)
    def _():
        rg_h[...] = jnp.zeros_like(rg_h)
        ml_c[...] = jnp.zeros_like(ml_c)
        ml_m[...] = jnp.zeros_like(ml_m)
        gd_s[...] = jnp.zeros_like(gd_s)

    gates, csum_gd, csum_ml, rows = _gate_tables(p_ref, cs_ref)
    _interleave(MIX_ORDER, dict(
        G=_gdn(p_ref, gdnw_ref, y_ref, gd_s, gates, csum_gd, rows),
        M=_mlstm(p_ref, mlnw_ref, y_ref, ml_c, ml_m, gates, csum_ml, rows),
        R=_rglru(p_ref, y_ref, rg_h)))


def _cumsum_matrix(ts):
    r = jnp.arange(ts)[:, None]
    c = jnp.arange(ts)[None, :]
    m64 = (r >= c) & (r // GD_CHUNK == c // GD_CHUNK)
    m128 = (r >= c) & (r // PAIR == c // PAIR)
    return jnp.concatenate([m64, m128], axis=0).astype(BF16)


def _mixers(p3, mlnw, gdnw, cs):
    b, s, _ = p3.shape
    ts = TS_MIX
    full = lambda shape: pl.BlockSpec(shape, lambda i, j: (0,) * len(shape))
    return pl.pallas_call(
        _mixer_kernel,
        grid=(b, s // ts),
        in_specs=[pl.BlockSpec((None, ts, D_PROJ), lambda i, j: (i, j, 0)),
                  full(mlnw.shape), full(gdnw.shape), full(cs.shape)],
        out_specs=pl.BlockSpec((None, ts, D_MIX), lambda i, j: (i, j, 0)),
        out_shape=jax.ShapeDtypeStruct((b, s, D_MIX), BF16),
        scratch_shapes=[pltpu.VMEM((SUBLANES, RG_WIDTH), F32),
                        pltpu.VMEM((ML_HEADS, ML_DH, 2 * ML_DH), F32),
                        pltpu.VMEM((ML_HEADS, SUBLANES, LANES), F32),
                        pltpu.VMEM((GD_HEADS, GD_DK, GD_DV), F32)],
        compiler_params=pltpu.CompilerParams(dimension_semantics=("parallel", "arbitrary"),
                                             vmem_limit_bytes=VMEM_LIMIT),
        name="mixers",
    )(p3, mlnw, gdnw, cs)


def _rearrange_w_in(w_in):
    pad = jnp.zeros(w_in.shape[:-1] + (LANES - GATE_LANES,), w_in.dtype)
    return jnp.concatenate([
        w_in[..., :SRC_ML_GATES], w_in[..., SRC_GD:SRC_GD_GATES],
        w_in[..., SRC_ML_GATES:SRC_GD], w_in[..., SRC_GD_GATES:], pad], axis=-1)


def _rg_gate_blockdiag(gate_w):
    depth = gate_w.shape[0]
    nb = RG_BLOCKS // 2
    eye = jnp.eye(nb, dtype=gate_w.dtype)
    halves = []
    for hf in range(2):
        blk = gate_w[:, :, hf * nb:(hf + 1) * nb]
        bd = jnp.einsum('lgncd,nm->lgncmd', blk, eye)
        bd = bd.reshape(depth, 2, nb * RG_BLOCK, nb * RG_BLOCK)
        halves.append(jnp.concatenate([bd[:, 0], bd[:, 1]], axis=-1))
    return jnp.stack(halves, axis=1)


def _gate_params(ml_gate_b, gd_dt_bias, gd_a_log):
    depth = ml_gate_b.shape[0]
    zeros = lambda n: jnp.zeros((depth, n), F32)
    row0 = jnp.concatenate([ml_gate_b[:, 0], ml_gate_b[:, 1], gd_dt_bias,
                            zeros(LANES - LANE_GD_B)], axis=-1)
    row1 = jnp.concatenate([zeros(LANE_GD_A), gd_a_log, zeros(LANES - LANE_GD_B)], axis=-1)
    rest = jnp.zeros((depth, SUBLANES - 2, LANES), F32)
    return jnp.concatenate([row0[:, None], row1[:, None], rest], axis=1)


def kernel(x, norm_w, w_in, rg_conv_w, rg_conv_b, rg_gate_w, rg_gate_b, rg_lambda, ml_gate_b,
           ml_norm_w, gd_conv_w, gd_a_log, gd_dt_bias, gd_norm_w, w_out, final_norm_w):
    bsz, seq, _ = x.shape
    depth = w_in.shape[0]
    t = bsz * seq
    w_big = _rearrange_w_in(w_in).astype(BF16)
    w_o = w_out.astype(BF16)
    rgp = jnp.concatenate([rg_conv_w, rg_conv_b[:, None], rg_gate_b, rg_lambda[:, None]],
                          axis=1)
    rgw = _rg_gate_blockdiag(rg_gate_w).astype(BF16)
    gp = _gate_params(ml_gate_b, gd_dt_bias, gd_a_log)
    cs = _cumsum_matrix(TS_MIX)

    x2 = x.reshape(t, D_MODEL)
    for l in range(depth):
        p = _inproj(x2, norm_w[l][None], w_big[l], rgp[l], rgw[l], gp[l], gd_conv_w[l], seq)
        y = _mixers(p.reshape(bsz, seq, D_PROJ), ml_norm_w[l][None], gd_norm_w[l][None], cs)
        x2 = _outproj(y.reshape(t, D_MIX), w_o[l], x2, final_norm_w[None], l == depth - 1)
    return x2.reshape(bsz, seq, D_MODEL)
```

```python
import functools

import jax
import jax.numpy as jnp
from jax import lax
from jax.experimental import pallas as pl
from jax.experimental.pallas import tpu as pltpu

F32 = jnp.float32
BF16 = jnp.bfloat16

D_MODEL = 1024
CONV_K = 4
NORM_EPS = 1e-6
RG_WIDTH = 512
RG_BLOCKS = 8
RG_BLOCK = RG_WIDTH // RG_BLOCKS
RG_C = 8.0
ML_HEADS = 4
ML_DH = 128
ML_WIDTH = ML_HEADS * ML_DH
GD_HEADS = 4
GD_DK = 128
GD_DV = 128
GD_QK = GD_HEADS * GD_DK
GD_WIDTH = GD_HEADS * GD_DV
GD_CONV = 2 * GD_QK + GD_WIDTH
D_MIX = RG_WIDTH + ML_WIDTH + GD_WIDTH
GD_CHUNK = 64
PAIR = 2 * GD_CHUNK
SUBLANES = 8
LANES = 128

W = 512
WCOL_RG_X, WCOL_RG_Z = 0 * W, 1 * W
WCOL_ML_Q, WCOL_ML_K, WCOL_ML_V, WCOL_ML_O, WCOL_ML_Z = 2 * W, 3 * W, 4 * W, 5 * W, 6 * W
WCOL_GD_QKV, WCOL_GD_Z = 7 * W, 10 * W
WCOL_GATES = 11 * W
D_PROJ = WCOL_GATES + LANES
OFF_RG_A, OFF_RG_B, OFF_RG_G = 0 * W, 1 * W, 2 * W
OFF_ML_Q, OFF_ML_K, OFF_ML_V, OFF_ML_G = 3 * W, 4 * W, 5 * W, 6 * W
OFF_GD_Q, OFF_GD_K, OFF_GD_V, OFF_GD_G = 7 * W, 8 * W, 9 * W, 10 * W
OFF_GATES = 11 * W
LANE_ML_I, LANE_ML_F, LANE_GD_A, LANE_GD_B = 0, 4, 8, 12
GATE_LANES = 16
SRC_ML_GATES = 2 * RG_WIDTH + 5 * ML_WIDTH
SRC_GD = SRC_ML_GATES + 2 * ML_HEADS
SRC_GD_GATES = SRC_GD + 2 * GD_QK + 2 * GD_WIDTH

VMEM_LIMIT = 56 * 1024 * 1024
TM_IN = 256
TS_MIX = 256
MIX_ORDER = "GGGGGGGGGMRGMRGMRGMRGMR"


def _dot(a, b):
    return jnp.dot(a.astype(BF16), b.astype(BF16), preferred_element_type=F32)


def _dot_nt(a, b):
    return lax.dot_general(a.astype(BF16), b.astype(BF16), (((1,), (1,)), ((), ())),
                           preferred_element_type=F32)


def _dot_tn(a, b):
    return lax.dot_general(a.astype(BF16), b.astype(BF16), (((0,), (0,)), ((), ())),
                           preferred_element_type=F32)


def _split3(x):
    hi = x.astype(BF16)
    r1 = x - hi.astype(F32)
    mid = r1.astype(BF16)
    lo = (r1 - mid.astype(F32)).astype(BF16)
    return hi, mid, lo


def _softplus(x):
    return jnp.maximum(x, 0.0) + jnp.log1p(jnp.exp(-jnp.abs(x)))


def _sigmoid(x):
    return 1.0 / (1.0 + jnp.exp(-x))


def _silu(x):
    return x * _sigmoid(x)


def _causal_conv(x_ref, col0, tail_ref, tcol0, cw_ref, width=W):
    ts = x_ref.shape[0]
    cols = slice(col0, col0 + width)
    tcols = slice(tcol0, tcol0 + width)
    head = jnp.concatenate([tail_ref[:, tcols], x_ref[0:SUBLANES, cols]], axis=0)
    acc = None
    acc0 = None
    for k in range(CONV_K):
        j = CONV_K - 1 - k
        wk = cw_ref[k:k + 1, tcols]
        term = x_ref[SUBLANES - j:ts - j, cols] * wk
        hs = head[SUBLANES:, :] if j == 0 else pltpu.roll(head, j, 0)[SUBLANES:, :]
        term0 = hs * wk
        acc = term if acc is None else acc + term
        acc0 = term0 if acc0 is None else acc0 + term0
    tail_ref[:, tcols] = x_ref[ts - SUBLANES:ts, cols]
    return jnp.concatenate([acc0, acc], axis=0)


def _inproj_kernel(x_ref, nw_ref, w_ref, rgp_ref, rgw_ref, gp_ref, gdcw_ref, p_ref,
                   hn_s, raw_z, xc_s, rg_tail, gd_tail, *, tiles_per_seq):
    @pl.when(pl.program_id(0) % tiles_per_seq == 0)
    def _():
        rg_tail[...] = jnp.zeros_like(rg_tail)
        gd_tail[...] = jnp.zeros_like(gd_tail)

    x = x_ref[...]
    r = lax.rsqrt(jnp.mean(x * x, axis=-1, keepdims=True) + NORM_EPS)
    hn_s[...] = ((x * r) * nw_ref[...]).astype(BF16)

    def proj(col, width=W):
        return jnp.dot(hn_s[...], w_ref[:, col:col + width], preferred_element_type=F32)

    def mm(dst, src, scale=None):
        def run():
            y = proj(src)
            p_ref[:, dst:dst + W] = y if scale is None else y * scale
        return run

    def mm_rg_gates():
        xcb = xc_s[...].astype(BF16)
        half = RG_WIDTH // 2
        g0 = jnp.dot(xcb[:, :half], rgw_ref[0], preferred_element_type=F32)
        g1 = jnp.dot(xcb[:, half:], rgw_ref[1], preferred_element_type=F32)
        p_ref[:, OFF_RG_B:OFF_RG_B + W] = jnp.concatenate([g0[:, :half], g1[:, :half]], axis=1)
        raw_z[...] = jnp.concatenate([g0[:, half:], g1[:, half:]], axis=1)

    def ep_rg_conv():
        xc_s[...] = _causal_conv(p_ref, OFF_RG_A, rg_tail, 0, rgp_ref) + rgp_ref[4:5, :]

    def ep_rg_ab():
        gr = p_ref[:, OFF_RG_B:OFF_RG_B + W] + rgp_ref[5:6, :]
        gi = raw_z[...] + rgp_ref[6:7, :]
        log_a = (-RG_C * _sigmoid(gr)) * _softplus(-rgp_ref[7:8, :])
        a = jnp.exp(log_a)
        p_ref[:, OFF_RG_A:OFF_RG_A + W] = a
        p_ref[:, OFF_RG_B:OFF_RG_B + W] = (jnp.sqrt(-jnp.tanh(log_a) * (a * a + 1.0))
                                           * (_sigmoid(gi) * xc_s[...]))

    def ep_silu(off):
        def run():
            p_ref[:, off:off + W] = _silu(p_ref[:, off:off + W])
        return run

    def ep_gd_conv(off, tcol, norm_scale):
        def run():
            y = _silu(_causal_conv(p_ref, off, gd_tail, tcol, gdcw_ref))
            if norm_scale is not None:
                heads = []
                for h in range(GD_HEADS):
                    yh = y[:, h * GD_DK:(h + 1) * GD_DK]
                    yh = yh * lax.rsqrt(jnp.sum(yh * yh, axis=-1, keepdims=True) + NORM_EPS)
                    heads.append(yh if norm_scale == 1.0 else yh * norm_scale)
                y = jnp.concatenate(heads, axis=1)
            p_ref[:, off:off + W] = y
        return run

    def mm_ml_z():
        raw_z[...] = proj(WCOL_ML_Z)

    def ep_ml_gate():
        p_ref[:, OFF_ML_G:OFF_ML_G + W] = (_sigmoid(p_ref[:, OFF_ML_G:OFF_ML_G + W])
                                           * _silu(raw_z[...]))

    def mm_gates():
        p_ref[:, OFF_GATES:OFF_GATES + LANES] = proj(WCOL_GATES, LANES)

    def ep_gates():
        gates = p_ref[:, OFF_GATES:OFF_GATES + LANES] + gp_ref[0:1, :]
        lane = lax.broadcasted_iota(jnp.int32, gates.shape, 1)
        is_i = lane < LANE_ML_F
        is_f = (lane >= LANE_ML_F) & (lane < LANE_ML_F + ML_HEADS)
        is_a = (lane >= LANE_GD_A) & (lane < LANE_GD_A + GD_HEADS)
        log_f = -_softplus(-gates)
        g_gd = -jnp.exp(gp_ref[1:2, :]) * _softplus(gates)
        p_ref[:, OFF_GATES:OFF_GATES + LANES] = jnp.where(
            is_i, gates, jnp.where(is_f, log_f, jnp.where(is_a, g_gd, _sigmoid(gates))))

    program = [
        mm(OFF_RG_A, WCOL_RG_X),
        mm(OFF_GD_Q, WCOL_GD_QKV), ep_rg_conv,
        mm_rg_gates,
        mm(OFF_GD_K, WCOL_GD_QKV + W), ep_rg_ab,
        mm(OFF_GD_V, WCOL_GD_QKV + 2 * W), ep_gd_conv(OFF_GD_Q, 0, GD_DK ** -0.5),
        mm(OFF_RG_G, WCOL_RG_Z), ep_gd_conv(OFF_GD_K, W, 1.0),
        mm(OFF_ML_G, WCOL_ML_O), ep_gd_conv(OFF_GD_V, 2 * W, None),
        mm_ml_z, ep_silu(OFF_RG_G),
        mm(OFF_GD_G, WCOL_GD_Z), ep_ml_gate,
        mm_gates, ep_silu(OFF_GD_G),
        mm(OFF_ML_Q, WCOL_ML_Q, ML_DH ** -0.5), ep_gates,
        mm(OFF_ML_K, WCOL_ML_K),
        mm(OFF_ML_V, WCOL_ML_V),
    ]
    for stage in program:
        stage()


def _inproj(x2, norm_w, w_big, rgp, rgw, gp, gdcw, layer, seq):
    t = x2.shape[0]
    tm = TM_IN
    per_layer = lambda a: pl.BlockSpec((None,) + a.shape[1:],
                                       lambda i: (layer,) + (0,) * (a.ndim - 1))
    return pl.pallas_call(
        functools.partial(_inproj_kernel, tiles_per_seq=seq // tm),
        grid=(t // tm,),
        in_specs=[pl.BlockSpec((tm, D_MODEL), lambda i: (i, 0)),
                  per_layer(norm_w), per_layer(w_big), per_layer(rgp), per_layer(rgw),
                  per_layer(gp), per_layer(gdcw)],
        out_specs=pl.BlockSpec((tm, D_PROJ), lambda i: (i, 0)),
        out_shape=jax.ShapeDtypeStruct((t, D_PROJ), F32),
        scratch_shapes=[pltpu.VMEM((tm, D_MODEL), BF16),
                        pltpu.VMEM((tm, W), F32),
                        pltpu.VMEM((tm, RG_WIDTH), F32),
                        pltpu.VMEM((SUBLANES, RG_WIDTH), F32),
                        pltpu.VMEM((SUBLANES, GD_CONV), F32)],
        compiler_params=pltpu.CompilerParams(dimension_semantics=("arbitrary",),
                                             vmem_limit_bytes=VMEM_LIMIT),
        name="inproj",
    )(x2, norm_w, w_big, rgp, rgw, gp, gdcw)


def _rglru(p_ref, y_ref, rg_h, project):
    ts = p_ref.shape[0]
    ng = ts // SUBLANES
    a3 = p_ref[:, OFF_RG_A:OFF_RG_A + W].reshape(ng, SUBLANES, RG_WIDTH)
    b3 = p_ref[:, OFF_RG_B:OFF_RG_B + W].reshape(ng, SUBLANES, RG_WIDTH)
    sub = lax.broadcasted_iota(jnp.int32, (ng, SUBLANES, RG_WIDTH), 1)
    d = 1
    while d < SUBLANES:
        a_sh = pltpu.roll(a3, d, 1)
        b_sh = pltpu.roll(b3, d, 1)
        valid = sub >= d
        b3 = jnp.where(valid, a3 * b_sh + b3, b3)
        a3 = jnp.where(valid, a3 * a_sh, a3)
        d *= 2
        yield
    carry = rg_h[0:1, :]
    groups = []
    for g in range(ng):
        hg = b3[g] + a3[g] * carry
        groups.append(hg)
        carry = hg[SUBLANES - 1:SUBLANES, :]
    rg_h[...] = jnp.broadcast_to(carry, rg_h.shape)
    yield
    h = jnp.concatenate(groups, axis=0)
    y_ref[:, 0:RG_WIDTH] = (h * p_ref[:, OFF_RG_G:OFF_RG_G + W]).astype(y_ref.dtype)
    project(0)


def _gate_tables(p_ref, cs_ref):
    ts = p_ref.shape[0]
    gates = p_ref[:, OFF_GATES:OFF_GATES + LANES]
    lane = lax.broadcasted_iota(jnp.int32, (ts, LANES), 1)
    is_i = lane < LANE_ML_F
    is_f = (lane >= LANE_ML_F) & (lane < LANE_ML_F + ML_HEADS)
    is_a = (lane >= LANE_GD_A) & (lane < LANE_GD_A + GD_HEADS)
    x = jnp.where(is_f | is_a, gates, 0.0)
    hi, mid, lo = _split3(x)
    packed = (hi.astype(F32) + pltpu.roll(mid.astype(F32), GATE_LANES, 1)
              + pltpu.roll(lo.astype(F32), 2 * GATE_LANES, 1))
    cs = jnp.dot(cs_ref[...], packed.astype(BF16), preferred_element_type=F32)
    cs = cs + pltpu.roll(cs, LANES - GATE_LANES, 1) + pltpu.roll(cs, LANES - 2 * GATE_LANES, 1)
    csum_gd = cs[:ts]
    csum_ml = cs[ts:]
    z = jnp.where(is_i, gates, jnp.where(is_f, csum_ml, jnp.where(is_a, csum_gd, 0.0)))
    zs = jnp.concatenate(_split3(z), axis=0)
    sel = (lax.broadcasted_iota(jnp.int32, (GATE_LANES, LANES), 0)
           == lax.broadcasted_iota(jnp.int32, (GATE_LANES, LANES), 1)).astype(BF16)
    zt = lax.dot_general(sel, zs, (((1,), (1,)), ((), ())), preferred_element_type=F32)
    rows = zt[:, :ts] + zt[:, ts:2 * ts] + zt[:, 2 * ts:]
    return gates, csum_gd, csum_ml, rows


def _mlstm(p_ref, mlnw_ref, y_ref, ml_c, ml_m, gates, csum_ml, rows, project):
    ts = p_ref.shape[0]
    L = PAIR
    r_i = lax.broadcasted_iota(jnp.int32, (L, L), 0)
    c_i = lax.broadcasted_iota(jnp.int32, (L, L), 1)
    causal = r_i >= c_i
    ones_blk = jnp.ones((L, ML_DH), F32)
    units = [(h, c) for c in range(ts // L) for h in range(ML_HEADS)]

    st = {}
    for h, c in units:
        rs = slice(c * L, (c + 1) * L)
        q = p_ref[rs, OFF_ML_Q + h * ML_DH:OFF_ML_Q + (h + 1) * ML_DH]
        k = p_ref[rs, OFF_ML_K + h * ML_DH:OFF_ML_K + (h + 1) * ML_DH]
        li_c = gates[rs, LANE_ML_I + h:LANE_ML_I + h + 1]
        bq_c = csum_ml[rs, LANE_ML_F + h:LANE_ML_F + h + 1]
        r_row = (rows[LANE_ML_I + h:LANE_ML_I + h + 1, rs]
                 - rows[LANE_ML_F + h:LANE_ML_F + h + 1, rs])
        dm = jnp.where(causal, bq_c + r_row, -1e30)
        mx = jnp.max(dm, axis=-1, keepdims=True)
        g_last = bq_c[L - 1:L, :]
        ws = g_last - bq_c + li_c
        mws = jnp.max(ws, axis=0, keepdims=True)
        st[h, c] = dict(q=q, k=k, bq_c=bq_c, mx=mx, g_last=g_last, mws=mws,
                        pm=jnp.exp(dm - mx), wk=jnp.exp(ws - mws), qk=_dot_nt(q, k))
    yield
    for h, c in units:
        u = st[h, c]
        rs = slice(c * L, (c + 1) * L)
        v = p_ref[rs, OFF_ML_V + h * ML_DH:OFF_ML_V + (h + 1) * ML_DH]
        v_aug = jnp.concatenate([v, ones_blk], axis=1)
        u["intra"] = _dot(u["qk"] * u["pm"], v_aug)
        u["kv"] = _dot_tn(u["k"] * u["wk"], v_aug)

    yield
    c_aug = [ml_c[h] for h in range(ML_HEADS)]
    m_prev = [ml_m[h][0:1, 0:1] for h in range(ML_HEADS)]
    outs = [[] for _ in range(ML_HEADS)]
    for c in range(ts // L):
        qc = [_dot(st[h, c]["q"], c_aug[h]) for h in range(ML_HEADS)]
        for h in range(ML_HEADS):
            u = st[h, c]
            m_inter = u["bq_c"] + m_prev[h]
            m_t = jnp.maximum(m_inter, u["mx"])
            na = jnp.exp(u["mx"] - m_t) * u["intra"] + jnp.exp(m_inter - m_t) * qc[h]
            outs[h].append(na[:, :ML_DH] / jnp.maximum(jnp.abs(na[:, ML_DH:]), jnp.exp(-m_t)))
            m_new = jnp.maximum(u["g_last"] + m_prev[h], u["mws"])
            c_aug[h] = (jnp.exp(u["g_last"] + m_prev[h] - m_new) * c_aug[h]
                        + jnp.exp(u["mws"] - m_new) * u["kv"])
            m_prev[h] = m_new
        yield
    for h in range(ML_HEADS):
        sl = slice(h * ML_DH, (h + 1) * ML_DH)
        ml_c[h] = c_aug[h]
        ml_m[h] = jnp.broadcast_to(m_prev[h], ml_m.shape[1:])
        hh = jnp.concatenate(outs[h], axis=0)
        hn = hh * lax.rsqrt(jnp.mean(hh * hh, axis=-1, keepdims=True) + NORM_EPS)
        y_ml = (hn * mlnw_ref[0:1, sl]) * p_ref[:, OFF_ML_G + h * ML_DH:OFF_ML_G + (h + 1) * ML_DH]
        y_ref[:, RG_WIDTH + h * ML_DH:RG_WIDTH + (h + 1) * ML_DH] = y_ml.astype(y_ref.dtype)
    project(RG_WIDTH)


def _gdn(p_ref, gdnw_ref, y_ref, gd_s, gates, csum_gd, rows, project):
    ts = p_ref.shape[0]
    L = PAIR
    C = GD_CHUNK
    r_i = lax.broadcasted_iota(jnp.int32, (L, L), 0)
    c_i = lax.broadcasted_iota(jnp.int32, (L, L), 1)
    same = (r_i >= C) == (c_i >= C)
    incl = same & (r_i >= c_i)
    strict = same & (r_i > c_i)
    eye = (r_i == c_i).astype(F32)
    second = lax.broadcasted_iota(jnp.int32, (L, 1), 0) >= C
    zeros_c = jnp.zeros((C, GD_DV), F32)
    units = [(h, j) for j in range(ts // L) for h in range(GD_HEADS)]

    st = {}
    for h, j in units:
        rs = slice(j * L, (j + 1) * L)
        q = p_ref[rs, OFF_GD_Q + h * GD_DK:OFF_GD_Q + (h + 1) * GD_DK]
        k = p_ref[rs, OFF_GD_K + h * GD_DK:OFF_GD_K + (h + 1) * GD_DK]
        beta = gates[rs, LANE_GD_B + h:LANE_GD_B + h + 1]
        gc_c = csum_gd[rs, LANE_GD_A + h:LANE_GD_A + h + 1]
        gc_r = rows[LANE_GD_A + h:LANE_GD_A + h + 1, rs]
        gam = jnp.where(incl, jnp.exp(jnp.where(incl, gc_c - gc_r, 0.0)), 0.0)
        kb = k * beta
        egc = jnp.exp(gc_c)
        g_first = gc_c[C - 1:C, :]
        g_second = gc_c[L - 1:L, :]
        st[h, j] = dict(k=k, kb=kb, beta=beta, gam=gam, egc=egc, q_dec=q * egc,
                        g_last=(g_first, g_second),
                        k_dec=k * jnp.exp(jnp.where(second, g_second, g_first) - gc_c),
                        kq=_dot_nt(jnp.concatenate([kb, q], axis=0), k))
    yield
    for key in units:
        u = st[key]
        u["aqk"] = u["kq"][L:] * u["gam"]
        pw = jnp.where(strict, -(u["kq"][:L] * u["gam"]), 0.0)
        u["t_inv"] = eye + pw
        u["pw"] = _dot(pw, pw)
    yield
    span = 4
    while span < C:
        for key in units:
            u = st[key]
            both = _dot(jnp.concatenate([u["pw"], u["t_inv"]], axis=0), u["pw"])
            u["pw"] = both[:L]
            u["t_inv"] = u["t_inv"] + both[L:]
        span *= 2
        yield
    for key in units:
        u = st[key]
        u["t_inv"] = u["t_inv"] + _dot(u["t_inv"], u["pw"])
    yield
    for h, j in units:
        u = st[h, j]
        rs = slice(j * L, (j + 1) * L)
        v = p_ref[rs, OFF_GD_V + h * GD_DV:OFF_GD_V + (h + 1) * GD_DV]
        uw = _dot(u["t_inv"], jnp.concatenate([v * u["beta"], u["kb"] * u["egc"]], axis=1))
        u["u"] = uw[:, :GD_DV]
        u["w"] = uw[:, GD_DV:]

    yield
    state = [gd_s[h] for h in range(GD_HEADS)]
    outs = [[] for _ in range(GD_HEADS)]
    for j in range(ts // L):
        for half in range(2):
            hs = slice(half * C, (half + 1) * C)
            wq = [_dot(jnp.concatenate([st[h, j]["w"][hs], st[h, j]["q_dec"][hs]], axis=0), state[h])
                  for h in range(GD_HEADS)]
            v_new = [st[h, j]["u"][hs] - wq[h][:C] for h in range(GD_HEADS)]
            for h in range(GD_HEADS):
                v_pad = jnp.concatenate([v_new[h], zeros_c] if half == 0 else [zeros_c, v_new[h]],
                                        axis=0)
                outs[h].append(wq[h][C:] + _dot(st[h, j]["aqk"][hs], v_pad))
            for h in range(GD_HEADS):
                u = st[h, j]
                state[h] = (state[h] * jnp.exp(u["g_last"][half])
                            + _dot_tn(u["k_dec"][hs], v_new[h]))
            yield
    for h in range(GD_HEADS):
        gd_s[h] = state[h]
        o = jnp.concatenate(outs[h], axis=0)
        on = o * lax.rsqrt(jnp.mean(o * o, axis=-1, keepdims=True) + NORM_EPS)
        y_gd = on * gdnw_ref[0:1, :] * p_ref[:, OFF_GD_G + h * GD_DV:OFF_GD_G + (h + 1) * GD_DV]
        y_ref[:, RG_WIDTH + ML_WIDTH + h * GD_DV:RG_WIDTH + ML_WIDTH + (h + 1) * GD_DV] = (
            y_gd.astype(y_ref.dtype))
    project(RG_WIDTH + ML_WIDTH)


def _interleave(order, stages):
    live = dict(stages)

    def step(key):
        if key in live:
            try:
                next(live[key])
            except StopIteration:
                del live[key]

    for key in order:
        step(key)
    while live:
        for key in list(live):
            step(key)


def _mixer_kernel(p_ref, x_ref, wo_ref, fw_ref, mlnw_ref, gdnw_ref, cs_ref, o_ref,
                  y_ref, rg_h, ml_c, ml_m, gd_s, *, final_norm):
    @pl.when(pl.program_id(1) == 0)
    def _():
        rg_h[...] = jnp.zeros_like(rg_h)
        ml_c[...] = jnp.zeros_like(ml_c)
        ml_m[...] = jnp.zeros_like(ml_m)
        gd_s[...] = jnp.zeros_like(gd_s)

    o_ref[...] = x_ref[...]

    def project(k0):
        o_ref[...] += jnp.dot(y_ref[:, k0:k0 + W], wo_ref[k0:k0 + W, :],
                              preferred_element_type=F32)

    gates, csum_gd, csum_ml, rows = _gate_tables(p_ref, cs_ref)
    _interleave(MIX_ORDER, dict(
        G=_gdn(p_ref, gdnw_ref, y_ref, gd_s, gates, csum_gd, rows, project),
        M=_mlstm(p_ref, mlnw_ref, y_ref, ml_c, ml_m, gates, csum_ml, rows, project),
        R=_rglru(p_ref, y_ref, rg_h, project)))
    if final_norm:
        xn = o_ref[...]
        r = lax.rsqrt(jnp.mean(xn * xn, axis=-1, keepdims=True) + NORM_EPS)
        o_ref[...] = (xn * r) * fw_ref[...]


def _cumsum_matrix(ts):
    r = jnp.arange(ts)[:, None]
    c = jnp.arange(ts)[None, :]
    m64 = (r >= c) & (r // GD_CHUNK == c // GD_CHUNK)
    m128 = (r >= c) & (r // PAIR == c // PAIR)
    return jnp.concatenate([m64, m128], axis=0).astype(BF16)


def _mixers(p3, x3, w_out, final_w, mlnw, gdnw, cs, layer, final_norm):
    b, s, _ = p3.shape
    ts = TS_MIX
    per_layer = lambda a: pl.BlockSpec((None,) + a.shape[1:],
                                       lambda i, j: (layer,) + (0,) * (a.ndim - 1))
    return pl.pallas_call(
        functools.partial(_mixer_kernel, final_norm=final_norm),
        grid=(b, s // ts),
        in_specs=[pl.BlockSpec((None, ts, D_PROJ), lambda i, j: (i, j, 0)),
                  pl.BlockSpec((None, ts, D_MODEL), lambda i, j: (i, j, 0)),
                  per_layer(w_out), pl.BlockSpec(final_w.shape, lambda i, j: (0, 0)),
                  per_layer(mlnw), per_layer(gdnw), pl.BlockSpec(cs.shape, lambda i, j: (0, 0))],
        out_specs=pl.BlockSpec((None, ts, D_MODEL), lambda i, j: (i, j, 0)),
        out_shape=jax.ShapeDtypeStruct((b, s, D_MODEL), F32),
        scratch_shapes=[pltpu.VMEM((ts, D_MIX), BF16),
                        pltpu.VMEM((SUBLANES, RG_WIDTH), F32),
                        pltpu.VMEM((ML_HEADS, ML_DH, 2 * ML_DH), F32),
                        pltpu.VMEM((ML_HEADS, SUBLANES, LANES), F32),
                        pltpu.VMEM((GD_HEADS, GD_DK, GD_DV), F32)],
        compiler_params=pltpu.CompilerParams(dimension_semantics=("parallel", "arbitrary"),
                                             vmem_limit_bytes=VMEM_LIMIT),
        name="mixers",
    )(p3, x3, w_out, final_w, mlnw, gdnw, cs)


def _rearrange_w_in(w_in):
    pad = jnp.zeros(w_in.shape[:-1] + (LANES - GATE_LANES,), w_in.dtype)
    return jnp.concatenate([
        w_in[..., :SRC_ML_GATES], w_in[..., SRC_GD:SRC_GD_GATES],
        w_in[..., SRC_ML_GATES:SRC_GD], w_in[..., SRC_GD_GATES:], pad], axis=-1)


def _rg_gate_blockdiag(gate_w):
    depth = gate_w.shape[0]
    nb = RG_BLOCKS // 2
    eye = jnp.eye(nb, dtype=gate_w.dtype)
    halves = []
    for hf in range(2):
        blk = gate_w[:, :, hf * nb:(hf + 1) * nb]
        bd = jnp.einsum('lgncd,nm->lgncmd', blk, eye)
        bd = bd.reshape(depth, 2, nb * RG_BLOCK, nb * RG_BLOCK)
        halves.append(jnp.concatenate([bd[:, 0], bd[:, 1]], axis=-1))
    return jnp.stack(halves, axis=1)


def _gate_params(ml_gate_b, gd_dt_bias, gd_a_log):
    depth = ml_gate_b.shape[0]
    zeros = lambda n: jnp.zeros((depth, n), F32)
    row0 = jnp.concatenate([ml_gate_b[:, 0], ml_gate_b[:, 1], gd_dt_bias,
                            zeros(LANES - LANE_GD_B)], axis=-1)
    row1 = jnp.concatenate([zeros(LANE_GD_A), gd_a_log, zeros(LANES - LANE_GD_B)], axis=-1)
    rest = jnp.zeros((depth, SUBLANES - 2, LANES), F32)
    return jnp.concatenate([row0[:, None], row1[:, None], rest], axis=1)


def kernel(x, norm_w, w_in, rg_conv_w, rg_conv_b, rg_gate_w, rg_gate_b, rg_lambda, ml_gate_b,
           ml_norm_w, gd_conv_w, gd_a_log, gd_dt_bias, gd_norm_w, w_out, final_norm_w):
    bsz, seq, _ = x.shape
    depth = w_in.shape[0]
    t = bsz * seq
    w_big = _rearrange_w_in(w_in).astype(BF16)
    w_o = w_out.astype(BF16)
    rgp = jnp.concatenate([rg_conv_w, rg_conv_b[:, None], rg_gate_b, rg_lambda[:, None]],
                          axis=1)
    rgw = _rg_gate_blockdiag(rg_gate_w).astype(BF16)
    gp = _gate_params(ml_gate_b, gd_dt_bias, gd_a_log)
    cs = _cumsum_matrix(TS_MIX)

    for l in range(depth):
        p = _inproj(x.reshape(t, D_MODEL), norm_w[:, None], w_big, rgp, rgw, gp, gd_conv_w, l, seq)
        x = _mixers(p.reshape(bsz, seq, D_PROJ), x, w_o, final_norm_w[None], ml_norm_w[:, None],
                    gd_norm_w[:, None], cs, l, l == depth - 1)
    return x
```

```python
import functools

import jax
import jax.numpy as jnp
from jax import lax
from jax.experimental import pallas as pl
from jax.experimental.pallas import tpu as pltpu

F32 = jnp.float32
BF16 = jnp.bfloat16

D_MODEL = 1024
CONV_K = 4
NORM_EPS = 1e-6
RG_WIDTH = 512
RG_BLOCKS = 8
RG_BLOCK = RG_WIDTH // RG_BLOCKS
RG_C = 8.0
ML_HEADS = 4
ML_DH = 128
ML_WIDTH = ML_HEADS * ML_DH
GD_HEADS = 4
GD_DK = 128
GD_DV = 128
GD_QK = GD_HEADS * GD_DK
GD_WIDTH = GD_HEADS * GD_DV
GD_CONV = 2 * GD_QK + GD_WIDTH
D_MIX = RG_WIDTH + ML_WIDTH + GD_WIDTH
GD_CHUNK = 64
PAIR = 2 * GD_CHUNK
SUBLANES = 8
LANES = 128

W = 512
WCOL_RG_X, WCOL_RG_Z = 0 * W, 1 * W
WCOL_ML_Q, WCOL_ML_K, WCOL_ML_V, WCOL_ML_O, WCOL_ML_Z = 2 * W, 3 * W, 4 * W, 5 * W, 6 * W
WCOL_GD_QKV, WCOL_GD_Z = 7 * W, 10 * W
WCOL_GATES = 11 * W
WCOL_SPLIT = WCOL_GD_QKV
D_PROJ = WCOL_GATES + LANES
OFF_RG_A, OFF_RG_B, OFF_RG_G = 0 * W, 1 * W, 2 * W
OFF_ML_Q, OFF_ML_K, OFF_ML_V, OFF_ML_G = 3 * W, 4 * W, 5 * W, 6 * W
OFF_GD_Q, OFF_GD_K, OFF_GD_V, OFF_GD_G = 7 * W, 8 * W, 9 * W, 10 * W
OFF_GATES = 11 * W
LANE_ML_I, LANE_ML_F, LANE_GD_A, LANE_GD_B = 0, 4, 8, 12
GATE_LANES = 16
SRC_ML_GATES = 2 * RG_WIDTH + 5 * ML_WIDTH
SRC_GD = SRC_ML_GATES + 2 * ML_HEADS
SRC_GD_GATES = SRC_GD + 2 * GD_QK + 2 * GD_WIDTH

VMEM_LIMIT = 56 * 1024 * 1024
TM_IN = 256
TS_MIX = 256
MIX_ORDER = "GGGGGGGGGMRGMRGMRGMRGMR"


def _dot(a, b):
    return jnp.dot(a.astype(BF16), b.astype(BF16), preferred_element_type=F32)


def _dot_nt(a, b):
    return lax.dot_general(a.astype(BF16), b.astype(BF16), (((1,), (1,)), ((), ())),
                           preferred_element_type=F32)


def _dot_tn(a, b):
    return lax.dot_general(a.astype(BF16), b.astype(BF16), (((0,), (0,)), ((), ())),
                           preferred_element_type=F32)


def _split3(x):
    hi = x.astype(BF16)
    r1 = x - hi.astype(F32)
    mid = r1.astype(BF16)
    lo = (r1 - mid.astype(F32)).astype(BF16)
    return hi, mid, lo


def _softplus(x):
    return jnp.maximum(x, 0.0) + jnp.log1p(jnp.exp(-jnp.abs(x)))


def _sigmoid(x):
    return 1.0 / (1.0 + jnp.exp(-x))


def _silu(x):
    return x * _sigmoid(x)


def _causal_conv(x_ref, col0, tail_ref, tcol0, cw_ref, width=W):
    ts = x_ref.shape[0]
    cols = slice(col0, col0 + width)
    tcols = slice(tcol0, tcol0 + width)
    head = jnp.concatenate([tail_ref[:, tcols], x_ref[0:SUBLANES, cols]], axis=0)
    acc = None
    acc0 = None
    for k in range(CONV_K):
        j = CONV_K - 1 - k
        wk = cw_ref[k:k + 1, tcols]
        term = x_ref[SUBLANES - j:ts - j, cols] * wk
        hs = head[SUBLANES:, :] if j == 0 else pltpu.roll(head, j, 0)[SUBLANES:, :]
        term0 = hs * wk
        acc = term if acc is None else acc + term
        acc0 = term0 if acc0 is None else acc0 + term0
    tail_ref[:, tcols] = x_ref[ts - SUBLANES:ts, cols]
    return jnp.concatenate([acc0, acc], axis=0)


def _inproj_kernel(x_ref, nw_ref, wa_ref, wb_ref, rgp_ref, rgw_ref, gp_ref, gdcw_ref, p_ref,
                   hn_s, raw_z, xc_s, rg_tail, gd_tail, *, tiles_per_seq):
    @pl.when(pl.program_id(0) % tiles_per_seq == 0)
    def _():
        rg_tail[...] = jnp.zeros_like(rg_tail)
        gd_tail[...] = jnp.zeros_like(gd_tail)

    x = x_ref[...]
    r = lax.rsqrt(jnp.mean(x * x, axis=-1, keepdims=True) + NORM_EPS)
    hn_s[...] = ((x * r) * nw_ref[...]).astype(BF16)

    H = W // 2

    def proj(col, width):
        if col < WCOL_SPLIT:
            w = wa_ref[:, col:col + width]
        else:
            w = wb_ref[:, col - WCOL_SPLIT:col - WCOL_SPLIT + width]
        return jnp.dot(hn_s[...], w, preferred_element_type=F32)

    def mm(dst_ref, dst, src, scale=None):
        def run():
            y = proj(src, H)
            dst_ref[:, dst:dst + H] = y if scale is None else y * scale
        return run

    def mm_rg_gates(c):
        def run():
            g = jnp.dot(xc_s[:, c:c + H].astype(BF16), rgw_ref[c // H],
                        preferred_element_type=F32)
            p_ref[:, OFF_RG_B + c:OFF_RG_B + c + H] = g[:, :H]
            raw_z[:, c:c + H] = g[:, H:]
        return run

    def ep_rg_conv(c):
        def run():
            xc_s[:, c:c + H] = (_causal_conv(p_ref, OFF_RG_A + c, rg_tail, c, rgp_ref, H)
                                + rgp_ref[4:5, c:c + H])
        return run

    def ep_rg_ab(c):
        def run():
            gr = p_ref[:, OFF_RG_B + c:OFF_RG_B + c + H] + rgp_ref[5:6, c:c + H]
            gi = raw_z[:, c:c + H] + rgp_ref[6:7, c:c + H]
            log_a = (-RG_C * _sigmoid(gr)) * _softplus(-rgp_ref[7:8, c:c + H])
            a = jnp.exp(log_a)
            p_ref[:, OFF_RG_A + c:OFF_RG_A + c + H] = a
            p_ref[:, OFF_RG_B + c:OFF_RG_B + c + H] = (
                jnp.sqrt(-jnp.tanh(log_a) * (a * a + 1.0)) * (_sigmoid(gi) * xc_s[:, c:c + H]))
        return run

    def ep_silu(off):
        def run():
            p_ref[:, off:off + H] = _silu(p_ref[:, off:off + H])
        return run

    def ep_gd_conv(off, tcol, norm_scale):
        def run():
            y = _silu(_causal_conv(p_ref, off, gd_tail, tcol, gdcw_ref, H))
            if norm_scale is not None:
                heads = []
                for h in range(H // GD_DK):
                    yh = y[:, h * GD_DK:(h + 1) * GD_DK]
                    yh = yh * lax.rsqrt(jnp.sum(yh * yh, axis=-1, keepdims=True) + NORM_EPS)
                    heads.append(yh if norm_scale == 1.0 else yh * norm_scale)
                y = jnp.concatenate(heads, axis=1)
            p_ref[:, off:off + H] = y
        return run

    def ep_ml_gate(c):
        def run():
            p_ref[:, OFF_ML_G + c:OFF_ML_G + c + H] = (
                _sigmoid(p_ref[:, OFF_ML_G + c:OFF_ML_G + c + H]) * _silu(raw_z[:, c:c + H]))
        return run

    def mm_gates():
        p_ref[:, OFF_GATES:OFF_GATES + LANES] = proj(WCOL_GATES, LANES)

    def ep_gates():
        gates = p_ref[:, OFF_GATES:OFF_GATES + LANES] + gp_ref[0:1, :]
        lane = lax.broadcasted_iota(jnp.int32, gates.shape, 1)
        is_i = lane < LANE_ML_F
        is_f = (lane >= LANE_ML_F) & (lane < LANE_ML_F + ML_HEADS)
        is_a = (lane >= LANE_GD_A) & (lane < LANE_GD_A + GD_HEADS)
        log_f = -_softplus(-gates)
        g_gd = -jnp.exp(gp_ref[1:2, :]) * _softplus(gates)
        p_ref[:, OFF_GATES:OFF_GATES + LANES] = jnp.where(
            is_i, gates, jnp.where(is_f, log_f, jnp.where(is_a, g_gd, _sigmoid(gates))))

    units = []
    for c in range(0, W, H):
        units += [
            ([mm(p_ref, OFF_RG_A + c, WCOL_RG_X + c)], ep_rg_conv(c)),
            ([mm(p_ref, OFF_GD_Q + c, WCOL_GD_QKV + c)],
             ep_gd_conv(OFF_GD_Q + c, c, GD_DK ** -0.5)),
            ([mm(p_ref, OFF_ML_Q + c, WCOL_ML_Q + c, ML_DH ** -0.5)], None),
            ([mm_rg_gates(c)], ep_rg_ab(c)),
            ([mm(p_ref, OFF_GD_K + c, WCOL_GD_QKV + W + c)], ep_gd_conv(OFF_GD_K + c, W + c, 1.0)),
            ([mm(p_ref, OFF_ML_K + c, WCOL_ML_K + c)], None),
            ([mm(p_ref, OFF_GD_V + c, WCOL_GD_QKV + 2 * W + c)],
             ep_gd_conv(OFF_GD_V + c, 2 * W + c, None)),
            ([mm(p_ref, OFF_ML_V + c, WCOL_ML_V + c)], None),
            ([mm(p_ref, OFF_RG_G + c, WCOL_RG_Z + c)], ep_silu(OFF_RG_G + c)),
            ([mm(p_ref, OFF_ML_G + c, WCOL_ML_O + c), mm(raw_z, c, WCOL_ML_Z + c)], ep_ml_gate(c)),
            ([mm(p_ref, OFF_GD_G + c, WCOL_GD_Z + c)], ep_silu(OFF_GD_G + c)),
        ]
    units.append(([mm_gates], ep_gates))
    pending = None
    for mms, ep in units:
        for run in mms:
            run()
        if pending is not None:
            pending()
        pending = ep
    if pending is not None:
        pending()


def _inproj(x2, norm_w, w_a, w_b, rgp, rgw, gp, gdcw, layer, seq):
    t = x2.shape[0]
    tm = TM_IN
    per_layer = lambda a: pl.BlockSpec((None,) + a.shape[1:],
                                       lambda i: (layer,) + (0,) * (a.ndim - 1))
    return pl.pallas_call(
        functools.partial(_inproj_kernel, tiles_per_seq=seq // tm),
        grid=(t // tm,),
        in_specs=[pl.BlockSpec((tm, D_MODEL), lambda i: (i, 0)),
                  per_layer(norm_w),
                  pl.BlockSpec((None, D_MODEL, WCOL_SPLIT), lambda i: (layer, 0, 0)),
                  per_layer(w_b), per_layer(rgp), per_layer(rgw),
                  per_layer(gp), per_layer(gdcw)],
        out_specs=pl.BlockSpec((tm, D_PROJ), lambda i: (i, 0)),
        out_shape=jax.ShapeDtypeStruct((t, D_PROJ), F32),
        scratch_shapes=[pltpu.VMEM((tm, D_MODEL), BF16),
                        pltpu.VMEM((tm, W), F32),
                        pltpu.VMEM((tm, RG_WIDTH), F32),
                        pltpu.VMEM((SUBLANES, RG_WIDTH), F32),
                        pltpu.VMEM((SUBLANES, GD_CONV), F32)],
        compiler_params=pltpu.CompilerParams(dimension_semantics=("arbitrary",),
                                             vmem_limit_bytes=VMEM_LIMIT),
        name="inproj",
    )(x2, norm_w, w_a, w_b, rgp, rgw, gp, gdcw)


def _rglru(p_ref, y_ref, rg_h, project):
    ts = p_ref.shape[0]
    ng = ts // SUBLANES
    a3 = p_ref[:, OFF_RG_A:OFF_RG_A + W].reshape(ng, SUBLANES, RG_WIDTH)
    b3 = p_ref[:, OFF_RG_B:OFF_RG_B + W].reshape(ng, SUBLANES, RG_WIDTH)
    sub = lax.broadcasted_iota(jnp.int32, (ng, SUBLANES, RG_WIDTH), 1)
    d = 1
    while d < SUBLANES:
        a_sh = pltpu.roll(a3, d, 1)
        b_sh = pltpu.roll(b3, d, 1)
        valid = sub >= d
        b3 = jnp.where(valid, a3 * b_sh + b3, b3)
        a3 = jnp.where(valid, a3 * a_sh, a3)
        d *= 2
        yield
    carry = rg_h[0:1, :]
    groups = []
    for g in range(ng):
        hg = b3[g] + a3[g] * carry
        groups.append(hg)
        carry = hg[SUBLANES - 1:SUBLANES, :]
    rg_h[...] = jnp.broadcast_to(carry, rg_h.shape)
    yield
    h = jnp.concatenate(groups, axis=0)
    y_ref[:, 0:RG_WIDTH] = (h * p_ref[:, OFF_RG_G:OFF_RG_G + W]).astype(y_ref.dtype)
    project(0)


def _gate_tables(p_ref, cs_ref):
    ts = p_ref.shape[0]
    gates = p_ref[:, OFF_GATES:OFF_GATES + LANES]
    lane = lax.broadcasted_iota(jnp.int32, (ts, LANES), 1)
    is_i = lane < LANE_ML_F
    is_f = (lane >= LANE_ML_F) & (lane < LANE_ML_F + ML_HEADS)
    is_a = (lane >= LANE_GD_A) & (lane < LANE_GD_A + GD_HEADS)
    x = jnp.where(is_f | is_a, gates, 0.0)
    hi, mid, lo = _split3(x)
    packed = (hi.astype(F32) + pltpu.roll(mid.astype(F32), GATE_LANES, 1)
              + pltpu.roll(lo.astype(F32), 2 * GATE_LANES, 1))
    cs = jnp.dot(cs_ref[...], packed.astype(BF16), preferred_element_type=F32)
    cs = cs + pltpu.roll(cs, LANES - GATE_LANES, 1) + pltpu.roll(cs, LANES - 2 * GATE_LANES, 1)
    csum_gd = cs[:ts]
    csum_ml = cs[ts:]
    z = jnp.where(is_i, gates, jnp.where(is_f, csum_ml, jnp.where(is_a, csum_gd, 0.0)))
    zs = jnp.concatenate(_split3(z), axis=0)
    sel = (lax.broadcasted_iota(jnp.int32, (GATE_LANES, LANES), 0)
           == lax.broadcasted_iota(jnp.int32, (GATE_LANES, LANES), 1)).astype(BF16)
    zt = lax.dot_general(sel, zs, (((1,), (1,)), ((), ())), preferred_element_type=F32)
    rows = zt[:, :ts] + zt[:, ts:2 * ts] + zt[:, 2 * ts:]
    return gates, csum_gd, csum_ml, rows


def _mlstm(p_ref, mlnw_ref, y_ref, ml_c, ml_m, gates, csum_ml, rows, project):
    ts = p_ref.shape[0]
    L = PAIR
    r_i = lax.broadcasted_iota(jnp.int32, (L, L), 0)
    c_i = lax.broadcasted_iota(jnp.int32, (L, L), 1)
    causal = r_i >= c_i
    ones_blk = jnp.ones((L, ML_DH), F32)
    units = [(h, c) for c in range(ts // L) for h in range(ML_HEADS)]

    st = {}
    for h, c in units:
        rs = slice(c * L, (c + 1) * L)
        q = p_ref[rs, OFF_ML_Q + h * ML_DH:OFF_ML_Q + (h + 1) * ML_DH]
        k = p_ref[rs, OFF_ML_K + h * ML_DH:OFF_ML_K + (h + 1) * ML_DH]
        li_c = gates[rs, LANE_ML_I + h:LANE_ML_I + h + 1]
        bq_c = csum_ml[rs, LANE_ML_F + h:LANE_ML_F + h + 1]
        r_row = (rows[LANE_ML_I + h:LANE_ML_I + h + 1, rs]
                 - rows[LANE_ML_F + h:LANE_ML_F + h + 1, rs])
        dm = jnp.where(causal, bq_c + r_row, -1e30)
        mx = jnp.max(dm, axis=-1, keepdims=True)
        g_last = bq_c[L - 1:L, :]
        ws = g_last - bq_c + li_c
        mws = jnp.max(ws, axis=0, keepdims=True)
        st[h, c] = dict(q=q, k=k, bq_c=bq_c, mx=mx, g_last=g_last, mws=mws,
                        pm=jnp.exp(dm - mx), wk=jnp.exp(ws - mws), qk=_dot_nt(q, k))
    yield
    for h, c in units:
        u = st[h, c]
        rs = slice(c * L, (c + 1) * L)
        v = p_ref[rs, OFF_ML_V + h * ML_DH:OFF_ML_V + (h + 1) * ML_DH]
        v_aug = jnp.concatenate([v, ones_blk], axis=1)
        u["intra"] = _dot(u["qk"] * u["pm"], v_aug)
        u["kv"] = _dot_tn(u["k"] * u["wk"], v_aug)

    yield
    c_aug = [ml_c[h] for h in range(ML_HEADS)]
    m_prev = [ml_m[h][0:1, 0:1] for h in range(ML_HEADS)]
    outs = [[] for _ in range(ML_HEADS)]
    for c in range(ts // L):
        qc = [_dot(st[h, c]["q"], c_aug[h]) for h in range(ML_HEADS)]
        for h in range(ML_HEADS):
            u = st[h, c]
            m_inter = u["bq_c"] + m_prev[h]
            m_t = jnp.maximum(m_inter, u["mx"])
            na = jnp.exp(u["mx"] - m_t) * u["intra"] + jnp.exp(m_inter - m_t) * qc[h]
            outs[h].append(na[:, :ML_DH] / jnp.maximum(jnp.abs(na[:, ML_DH:]), jnp.exp(-m_t)))
            m_new = jnp.maximum(u["g_last"] + m_prev[h], u["mws"])
            c_aug[h] = (jnp.exp(u["g_last"] + m_prev[h] - m_new) * c_aug[h]
                        + jnp.exp(u["mws"] - m_new) * u["kv"])
            m_prev[h] = m_new
        yield
    for h in range(ML_HEADS):
        sl = slice(h * ML_DH, (h + 1) * ML_DH)
        ml_c[h] = c_aug[h]
        ml_m[h] = jnp.broadcast_to(m_prev[h], ml_m.shape[1:])
        hh = jnp.concatenate(outs[h], axis=0)
        hn = hh * lax.rsqrt(jnp.mean(hh * hh, axis=-1, keepdims=True) + NORM_EPS)
        y_ml = (hn * mlnw_ref[0:1, sl]) * p_ref[:, OFF_ML_G + h * ML_DH:OFF_ML_G + (h + 1) * ML_DH]
        y_ref[:, RG_WIDTH + h * ML_DH:RG_WIDTH + (h + 1) * ML_DH] = y_ml.astype(y_ref.dtype)
    project(RG_WIDTH)


def _gdn(p_ref, gdnw_ref, y_ref, gd_s, gates, csum_gd, rows, project):
    ts = p_ref.shape[0]
    L = PAIR
    C = GD_CHUNK
    r_i = lax.broadcasted_iota(jnp.int32, (L, L), 0)
    c_i = lax.broadcasted_iota(jnp.int32, (L, L), 1)
    same = (r_i >= C) == (c_i >= C)
    incl = same & (r_i >= c_i)
    strict = same & (r_i > c_i)
    eye = (r_i == c_i).astype(F32)
    second = lax.broadcasted_iota(jnp.int32, (L, 1), 0) >= C
    zeros_c = jnp.zeros((C, GD_DV), F32)
    units = [(h, j) for j in range(ts // L) for h in range(GD_HEADS)]

    st = {}
    for h, j in units:
        rs = slice(j * L, (j + 1) * L)
        q = p_ref[rs, OFF_GD_Q + h * GD_DK:OFF_GD_Q + (h + 1) * GD_DK]
        k = p_ref[rs, OFF_GD_K + h * GD_DK:OFF_GD_K + (h + 1) * GD_DK]
        beta = gates[rs, LANE_GD_B + h:LANE_GD_B + h + 1]
        gc_c = csum_gd[rs, LANE_GD_A + h:LANE_GD_A + h + 1]
        gc_r = rows[LANE_GD_A + h:LANE_GD_A + h + 1, rs]
        gam = jnp.where(incl, jnp.exp(jnp.where(incl, gc_c - gc_r, 0.0)), 0.0)
        kb = k * beta
        egc = jnp.exp(gc_c)
        g_first = gc_c[C - 1:C, :]
        g_second = gc_c[L - 1:L, :]
        st[h, j] = dict(k=k, kb=kb, beta=beta, gam=gam, egc=egc, q_dec=q * egc,
                        g_last=(g_first, g_second),
                        k_dec=k * jnp.exp(jnp.where(second, g_second, g_first) - gc_c),
                        kq=_dot_nt(jnp.concatenate([kb, q], axis=0), k))
    yield
    for key in units:
        u = st[key]
        u["aqk"] = u["kq"][L:] * u["gam"]
        pw = jnp.where(strict, -(u["kq"][:L] * u["gam"]), 0.0)
        u["t_inv"] = eye + pw
        u["pw"] = _dot(pw, pw)
    yield
    span = 4
    while span < C:
        for key in units:
            u = st[key]
            both = _dot(jnp.concatenate([u["pw"], u["t_inv"]], axis=0), u["pw"])
            u["pw"] = both[:L]
            u["t_inv"] = u["t_inv"] + both[L:]
        span *= 2
        yield
    for key in units:
        u = st[key]
        u["t_inv"] = u["t_inv"] + _dot(u["t_inv"], u["pw"])
    yield
    for h, j in units:
        u = st[h, j]
        rs = slice(j * L, (j + 1) * L)
        v = p_ref[rs, OFF_GD_V + h * GD_DV:OFF_GD_V + (h + 1) * GD_DV]
        uw = _dot(u["t_inv"], jnp.concatenate([v * u["beta"], u["kb"] * u["egc"]], axis=1))
        u["u"] = uw[:, :GD_DV]
        u["w"] = uw[:, GD_DV:]

    yield
    state = [gd_s[h] for h in range(GD_HEADS)]
    outs = [[] for _ in range(GD_HEADS)]
    for j in range(ts // L):
        for half in range(2):
            hs = slice(half * C, (half + 1) * C)
            wq = [_dot(jnp.concatenate([st[h, j]["w"][hs], st[h, j]["q_dec"][hs]], axis=0), state[h])
                  for h in range(GD_HEADS)]
            v_new = [st[h, j]["u"][hs] - wq[h][:C] for h in range(GD_HEADS)]
            for h in range(GD_HEADS):
                v_pad = jnp.concatenate([v_new[h], zeros_c] if half == 0 else [zeros_c, v_new[h]],
                                        axis=0)
                outs[h].append(wq[h][C:] + _dot(st[h, j]["aqk"][hs], v_pad))
            for h in range(GD_HEADS):
                u = st[h, j]
                state[h] = (state[h] * jnp.exp(u["g_last"][half])
                            + _dot_tn(u["k_dec"][hs], v_new[h]))
            yield
    for h in range(GD_HEADS):
        gd_s[h] = state[h]
        o = jnp.concatenate(outs[h], axis=0)
        on = o * lax.rsqrt(jnp.mean(o * o, axis=-1, keepdims=True) + NORM_EPS)
        y_gd = on * gdnw_ref[0:1, :] * p_ref[:, OFF_GD_G + h * GD_DV:OFF_GD_G + (h + 1) * GD_DV]
        y_ref[:, RG_WIDTH + ML_WIDTH + h * GD_DV:RG_WIDTH + ML_WIDTH + (h + 1) * GD_DV] = (
            y_gd.astype(y_ref.dtype))
    project(RG_WIDTH + ML_WIDTH)


def _interleave(order, stages):
    live = dict(stages)

    def step(key):
        if key in live:
            try:
                next(live[key])
            except StopIteration:
                del live[key]

    for key in order:
        step(key)
    while live:
        for key in list(live):
            step(key)


def _mixer_kernel(p_ref, x_ref, wo_ref, fw_ref, mlnw_ref, gdnw_ref, cs_ref, o_ref,
                  y_ref, rg_h, ml_c, ml_m, gd_s, *, final_norm):
    @pl.when(pl.program_id(1) == 0)
    def _():
        rg_h[...] = jnp.zeros_like(rg_h)
        ml_c[...] = jnp.zeros_like(ml_c)
        ml_m[...] = jnp.zeros_like(ml_m)
        gd_s[...] = jnp.zeros_like(gd_s)

    o_ref[...] = x_ref[...]

    def project(k0):
        o_ref[...] += jnp.dot(y_ref[:, k0:k0 + W], wo_ref[k0:k0 + W, :],
                              preferred_element_type=F32)

    gates, csum_gd, csum_ml, rows = _gate_tables(p_ref, cs_ref)
    _interleave(MIX_ORDER, dict(
        G=_gdn(p_ref, gdnw_ref, y_ref, gd_s, gates, csum_gd, rows, project),
        M=_mlstm(p_ref, mlnw_ref, y_ref, ml_c, ml_m, gates, csum_ml, rows, project),
        R=_rglru(p_ref, y_ref, rg_h, project)))
    if final_norm:
        xn = o_ref[...]
        r = lax.rsqrt(jnp.mean(xn * xn, axis=-1, keepdims=True) + NORM_EPS)
        o_ref[...] = (xn * r) * fw_ref[...]


def _cumsum_matrix(ts):
    r = jnp.arange(ts)[:, None]
    c = jnp.arange(ts)[None, :]
    m64 = (r >= c) & (r // GD_CHUNK == c // GD_CHUNK)
    m128 = (r >= c) & (r // PAIR == c // PAIR)
    return jnp.concatenate([m64, m128], axis=0).astype(BF16)


def _mixers(p3, x3, w_out, final_w, mlnw, gdnw, cs, layer, final_norm):
    b, s, _ = p3.shape
    ts = TS_MIX
    per_layer = lambda a: pl.BlockSpec((None,) + a.shape[1:],
                                       lambda i, j: (layer,) + (0,) * (a.ndim - 1))
    return pl.pallas_call(
        functools.partial(_mixer_kernel, final_norm=final_norm),
        grid=(b, s // ts),
        in_specs=[pl.BlockSpec((None, ts, D_PROJ), lambda i, j: (i, j, 0)),
                  pl.BlockSpec((None, ts, D_MODEL), lambda i, j: (i, j, 0)),
                  per_layer(w_out), pl.BlockSpec(final_w.shape, lambda i, j: (0, 0)),
                  per_layer(mlnw), per_layer(gdnw), pl.BlockSpec(cs.shape, lambda i, j: (0, 0))],
        out_specs=pl.BlockSpec((None, ts, D_MODEL), lambda i, j: (i, j, 0)),
        out_shape=jax.ShapeDtypeStruct((b, s, D_MODEL), F32),
        scratch_shapes=[pltpu.VMEM((ts, D_MIX), BF16),
                        pltpu.VMEM((SUBLANES, RG_WIDTH), F32),
                        pltpu.VMEM((ML_HEADS, ML_DH, 2 * ML_DH), F32),
                        pltpu.VMEM((ML_HEADS, SUBLANES, LANES), F32),
                        pltpu.VMEM((GD_HEADS, GD_DK, GD_DV), F32)],
        compiler_params=pltpu.CompilerParams(dimension_semantics=("parallel", "arbitrary"),
                                             vmem_limit_bytes=VMEM_LIMIT),
        name="mixers",
    )(p3, x3, w_out, final_w, mlnw, gdnw, cs)


def _w_in_tail(w_in):
    pad = jnp.zeros(w_in.shape[:-1] + (LANES - GATE_LANES,), w_in.dtype)
    return jnp.concatenate([
        w_in[..., SRC_GD:SRC_GD_GATES], w_in[..., SRC_ML_GATES:SRC_GD],
        w_in[..., SRC_GD_GATES:], pad], axis=-1)


def _rg_gate_blockdiag(gate_w):
    depth = gate_w.shape[0]
    nb = RG_BLOCKS // 2
    eye = jnp.eye(nb, dtype=gate_w.dtype)
    halves = []
    for hf in range(2):
        blk = gate_w[:, :, hf * nb:(hf + 1) * nb]
        bd = jnp.einsum('lgncd,nm->lgncmd', blk, eye)
        bd = bd.reshape(depth, 2, nb * RG_BLOCK, nb * RG_BLOCK)
        halves.append(jnp.concatenate([bd[:, 0], bd[:, 1]], axis=-1))
    return jnp.stack(halves, axis=1)


def _gate_params(ml_gate_b, gd_dt_bias, gd_a_log):
    depth = ml_gate_b.shape[0]
    zeros = lambda n: jnp.zeros((depth, n), F32)
    row0 = jnp.concatenate([ml_gate_b[:, 0], ml_gate_b[:, 1], gd_dt_bias,
                            zeros(LANES - LANE_GD_B)], axis=-1)
    row1 = jnp.concatenate([zeros(LANE_GD_A), gd_a_log, zeros(LANES - LANE_GD_B)], axis=-1)
    rest = jnp.zeros((depth, SUBLANES - 2, LANES), F32)
    return jnp.concatenate([row0[:, None], row1[:, None], rest], axis=1)


def kernel(x, norm_w, w_in, rg_conv_w, rg_conv_b, rg_gate_w, rg_gate_b, rg_lambda, ml_gate_b,
           ml_norm_w, gd_conv_w, gd_a_log, gd_dt_bias, gd_norm_w, w_out, final_norm_w):
    bsz, seq, _ = x.shape
    depth = w_in.shape[0]
    t = bsz * seq
    w_a = w_in.astype(BF16)
    w_b = _w_in_tail(w_in).astype(BF16)
    w_o = w_out.astype(BF16)
    rgp = jnp.concatenate([rg_conv_w, rg_conv_b[:, None], rg_gate_b, rg_lambda[:, None]],
                          axis=1)
    rgw = _rg_gate_blockdiag(rg_gate_w).astype(BF16)
    gp = _gate_params(ml_gate_b, gd_dt_bias, gd_a_log)
    cs = _cumsum_matrix(TS_MIX)

    for l in range(depth):
        p = _inproj(x.reshape(t, D_MODEL), norm_w[:, None], w_a, w_b, rgp, rgw, gp, gd_conv_w, l,
                    seq)
        x = _mixers(p.reshape(bsz, seq, D_PROJ), x, w_o, final_norm_w[None], ml_norm_w[:, None],
                    gd_norm_w[:, None], cs, l, l == depth - 1)
    return x
```

```python
import functools

import jax
import jax.numpy as jnp
from jax import lax
from jax.experimental import pallas as pl
from jax.experimental.pallas import tpu as pltpu

F32 = jnp.float32
BF16 = jnp.bfloat16

D_MODEL = 1024
CONV_K = 4
NORM_EPS = 1e-6
RG_WIDTH = 512
RG_BLOCKS = 8
RG_BLOCK = RG_WIDTH // RG_BLOCKS
RG_C = 8.0
ML_HEADS = 4
ML_DH = 128
ML_WIDTH = ML_HEADS * ML_DH
GD_HEADS = 4
GD_DK = 128
GD_DV = 128
GD_QK = GD_HEADS * GD_DK
GD_WIDTH = GD_HEADS * GD_DV
GD_CONV = 2 * GD_QK + GD_WIDTH
D_MIX = RG_WIDTH + ML_WIDTH + GD_WIDTH
GD_CHUNK = 64
PAIR = 2 * GD_CHUNK
SUBLANES = 8
LANES = 128

W = 512
WCOL_RG_X, WCOL_RG_Z = 0 * W, 1 * W
WCOL_ML_Q, WCOL_ML_K, WCOL_ML_V, WCOL_ML_O, WCOL_ML_Z = 2 * W, 3 * W, 4 * W, 5 * W, 6 * W
WCOL_GD_QKV, WCOL_GD_Z = 7 * W, 10 * W
WCOL_GATES = 11 * W
WCOL_SPLIT = WCOL_GD_QKV
D_PROJ = WCOL_GATES + LANES
OFF_RG_A, OFF_RG_B, OFF_RG_G = 0 * W, 1 * W, 2 * W
OFF_ML_Q, OFF_ML_K, OFF_ML_V, OFF_ML_G = 3 * W, 4 * W, 5 * W, 6 * W
OFF_GD_Q, OFF_GD_K, OFF_GD_V, OFF_GD_G = 7 * W, 8 * W, 9 * W, 10 * W
OFF_GATES = 11 * W
LANE_ML_I, LANE_ML_F, LANE_GD_A, LANE_GD_B = 0, 4, 8, 12
GATE_LANES = 16
SRC_ML_GATES = 2 * RG_WIDTH + 5 * ML_WIDTH
SRC_GD = SRC_ML_GATES + 2 * ML_HEADS
SRC_GD_GATES = SRC_GD + 2 * GD_QK + 2 * GD_WIDTH

VMEM_LIMIT = 56 * 1024 * 1024
TS_BLOCK = 256
LAYER_ORDER = "GPP" * 9 + "MRPG" * 5


def _dot(a, b):
    return jnp.dot(a.astype(BF16), b.astype(BF16), preferred_element_type=F32)


def _dot_nt(a, b):
    return lax.dot_general(a.astype(BF16), b.astype(BF16), (((1,), (1,)), ((), ())),
                           preferred_element_type=F32)


def _dot_tn(a, b):
    return lax.dot_general(a.astype(BF16), b.astype(BF16), (((0,), (0,)), ((), ())),
                           preferred_element_type=F32)


def _split3(x):
    hi = x.astype(BF16)
    r1 = x - hi.astype(F32)
    mid = r1.astype(BF16)
    lo = (r1 - mid.astype(F32)).astype(BF16)
    return hi, mid, lo


def _softplus(x):
    return jnp.maximum(x, 0.0) + jnp.log1p(jnp.exp(-jnp.abs(x)))


def _sigmoid(x):
    return 1.0 / (1.0 + jnp.exp(-x))


def _silu(x):
    return x * _sigmoid(x)


def _causal_conv(x_ref, col0, tail_ref, tcol0, cw_ref, width=W):
    ts = x_ref.shape[0]
    cols = slice(col0, col0 + width)
    tcols = slice(tcol0, tcol0 + width)
    head = jnp.concatenate([tail_ref[:, tcols], x_ref[0:SUBLANES, cols]], axis=0)
    acc = None
    acc0 = None
    for k in range(CONV_K):
        j = CONV_K - 1 - k
        wk = cw_ref[k:k + 1, tcols]
        term = x_ref[SUBLANES - j:ts - j, cols] * wk
        hs = head[SUBLANES:, :] if j == 0 else pltpu.roll(head, j, 0)[SUBLANES:, :]
        term0 = hs * wk
        acc = term if acc is None else acc + term
        acc0 = term0 if acc0 is None else acc0 + term0
    tail_ref[:, tcols] = x_ref[ts - SUBLANES:ts, cols]
    return jnp.concatenate([acc0, acc], axis=0)


def _inproj_stream(x_ref, nw_ref, wa_ref, wb_ref, rgp_ref, rgw_ref, gp_ref, gdcw_ref, p_ref,
                   hn_s, raw_z, xc_s, rg_tail, gd_tail):
    x = x_ref[...]
    r = lax.rsqrt(jnp.mean(x * x, axis=-1, keepdims=True) + NORM_EPS)
    hn_s[...] = ((x * r) * nw_ref[...]).astype(BF16)

    H = W // 2

    def proj(col, width):
        if col < WCOL_SPLIT:
            w = wa_ref[:, col:col + width]
        else:
            w = wb_ref[:, col - WCOL_SPLIT:col - WCOL_SPLIT + width]
        return jnp.dot(hn_s[...], w, preferred_element_type=F32)

    def mm(dst_ref, dst, src, scale=None):
        def run():
            y = proj(src, H)
            dst_ref[:, dst:dst + H] = y if scale is None else y * scale
        return run

    def mm_rg_gates(c):
        def run():
            g = jnp.dot(xc_s[:, c:c + H].astype(BF16), rgw_ref[c // H],
                        preferred_element_type=F32)
            p_ref[:, OFF_RG_B + c:OFF_RG_B + c + H] = g[:, :H]
            raw_z[:, c:c + H] = g[:, H:]
        return run

    def ep_rg_conv(c):
        def run():
            xc_s[:, c:c + H] = (_causal_conv(p_ref, OFF_RG_A + c, rg_tail, c, rgp_ref, H)
                                + rgp_ref[4:5, c:c + H])
        return run

    def ep_rg_ab(c):
        def run():
            gr = p_ref[:, OFF_RG_B + c:OFF_RG_B + c + H] + rgp_ref[5:6, c:c + H]
            gi = raw_z[:, c:c + H] + rgp_ref[6:7, c:c + H]
            log_a = (-RG_C * _sigmoid(gr)) * _softplus(-rgp_ref[7:8, c:c + H])
            a = jnp.exp(log_a)
            p_ref[:, OFF_RG_A + c:OFF_RG_A + c + H] = a
            p_ref[:, OFF_RG_B + c:OFF_RG_B + c + H] = (
                jnp.sqrt(-jnp.tanh(log_a) * (a * a + 1.0)) * (_sigmoid(gi) * xc_s[:, c:c + H]))
        return run

    def ep_silu(off):
        def run():
            p_ref[:, off:off + H] = _silu(p_ref[:, off:off + H])
        return run

    def ep_gd_conv(off, tcol, norm_scale):
        def run():
            y = _silu(_causal_conv(p_ref, off, gd_tail, tcol, gdcw_ref, H))
            if norm_scale is not None:
                heads = []
                for h in range(H // GD_DK):
                    yh = y[:, h * GD_DK:(h + 1) * GD_DK]
                    yh = yh * lax.rsqrt(jnp.sum(yh * yh, axis=-1, keepdims=True) + NORM_EPS)
                    heads.append(yh if norm_scale == 1.0 else yh * norm_scale)
                y = jnp.concatenate(heads, axis=1)
            p_ref[:, off:off + H] = y
        return run

    def ep_ml_gate(c):
        def run():
            p_ref[:, OFF_ML_G + c:OFF_ML_G + c + H] = (
                _sigmoid(p_ref[:, OFF_ML_G + c:OFF_ML_G + c + H]) * _silu(raw_z[:, c:c + H]))
        return run

    def mm_gates():
        p_ref[:, OFF_GATES:OFF_GATES + LANES] = proj(WCOL_GATES, LANES)

    def ep_gates():
        gates = p_ref[:, OFF_GATES:OFF_GATES + LANES] + gp_ref[0:1, :]
        lane = lax.broadcasted_iota(jnp.int32, gates.shape, 1)
        is_i = lane < LANE_ML_F
        is_f = (lane >= LANE_ML_F) & (lane < LANE_ML_F + ML_HEADS)
        is_a = (lane >= LANE_GD_A) & (lane < LANE_GD_A + GD_HEADS)
        log_f = -_softplus(-gates)
        g_gd = -jnp.exp(gp_ref[1:2, :]) * _softplus(gates)
        p_ref[:, OFF_GATES:OFF_GATES + LANES] = jnp.where(
            is_i, gates, jnp.where(is_f, log_f, jnp.where(is_a, g_gd, _sigmoid(gates))))

    units = []
    for c in range(0, W, H):
        units += [
            ([mm(p_ref, OFF_RG_A + c, WCOL_RG_X + c)], ep_rg_conv(c)),
            ([mm(p_ref, OFF_GD_Q + c, WCOL_GD_QKV + c)],
             ep_gd_conv(OFF_GD_Q + c, c, GD_DK ** -0.5)),
            ([mm(p_ref, OFF_ML_Q + c, WCOL_ML_Q + c, ML_DH ** -0.5)], None),
            ([mm_rg_gates(c)], ep_rg_ab(c)),
            ([mm(p_ref, OFF_GD_K + c, WCOL_GD_QKV + W + c)], ep_gd_conv(OFF_GD_K + c, W + c, 1.0)),
            ([mm(p_ref, OFF_ML_K + c, WCOL_ML_K + c)], None),
            ([mm(p_ref, OFF_GD_V + c, WCOL_GD_QKV + 2 * W + c)],
             ep_gd_conv(OFF_GD_V + c, 2 * W + c, None)),
            ([mm(p_ref, OFF_ML_V + c, WCOL_ML_V + c)], None),
            ([mm(p_ref, OFF_RG_G + c, WCOL_RG_Z + c)], ep_silu(OFF_RG_G + c)),
            ([mm(p_ref, OFF_ML_G + c, WCOL_ML_O + c), mm(raw_z, c, WCOL_ML_Z + c)], ep_ml_gate(c)),
            ([mm(p_ref, OFF_GD_G + c, WCOL_GD_Z + c)], ep_silu(OFF_GD_G + c)),
        ]
    units.append(([mm_gates], ep_gates))
    pending = None
    for mms, ep in units:
        for run in mms:
            run()
        if pending is not None:
            pending()
        pending = ep
        yield
    if pending is not None:
        pending()


def _rglru(p_ref, y_ref, rg_h, project):
    ts = p_ref.shape[0]
    ng = ts // SUBLANES
    a3 = p_ref[:, OFF_RG_A:OFF_RG_A + W].reshape(ng, SUBLANES, RG_WIDTH)
    b3 = p_ref[:, OFF_RG_B:OFF_RG_B + W].reshape(ng, SUBLANES, RG_WIDTH)
    sub = lax.broadcasted_iota(jnp.int32, (ng, SUBLANES, RG_WIDTH), 1)
    d = 1
    while d < SUBLANES:
        a_sh = pltpu.roll(a3, d, 1)
        b_sh = pltpu.roll(b3, d, 1)
        valid = sub >= d
        b3 = jnp.where(valid, a3 * b_sh + b3, b3)
        a3 = jnp.where(valid, a3 * a_sh, a3)
        d *= 2
        yield
    carry = rg_h[0:1, :]
    groups = []
    for g in range(ng):
        hg = b3[g] + a3[g] * carry
        groups.append(hg)
        carry = hg[SUBLANES - 1:SUBLANES, :]
    rg_h[...] = jnp.broadcast_to(carry, rg_h.shape)
    yield
    h = jnp.concatenate(groups, axis=0)
    y_ref[:, 0:RG_WIDTH] = (h * p_ref[:, OFF_RG_G:OFF_RG_G + W]).astype(y_ref.dtype)
    project(0)


def _gate_tables(p_ref, cs_ref):
    ts = p_ref.shape[0]
    gates = p_ref[:, OFF_GATES:OFF_GATES + LANES]
    lane = lax.broadcasted_iota(jnp.int32, (ts, LANES), 1)
    is_i = lane < LANE_ML_F
    is_f = (lane >= LANE_ML_F) & (lane < LANE_ML_F + ML_HEADS)
    is_a = (lane >= LANE_GD_A) & (lane < LANE_GD_A + GD_HEADS)
    x = jnp.where(is_f | is_a, gates, 0.0)
    hi, mid, lo = _split3(x)
    packed = (hi.astype(F32) + pltpu.roll(mid.astype(F32), GATE_LANES, 1)
              + pltpu.roll(lo.astype(F32), 2 * GATE_LANES, 1))
    cs = jnp.dot(cs_ref[...], packed.astype(BF16), preferred_element_type=F32)
    cs = cs + pltpu.roll(cs, LANES - GATE_LANES, 1) + pltpu.roll(cs, LANES - 2 * GATE_LANES, 1)
    csum_gd = cs[:ts]
    csum_ml = cs[ts:]
    z = jnp.where(is_i, gates, jnp.where(is_f, csum_ml, jnp.where(is_a, csum_gd, 0.0)))
    zs = jnp.concatenate(_split3(z), axis=0)
    sel = (lax.broadcasted_iota(jnp.int32, (GATE_LANES, LANES), 0)
           == lax.broadcasted_iota(jnp.int32, (GATE_LANES, LANES), 1)).astype(BF16)
    zt = lax.dot_general(sel, zs, (((1,), (1,)), ((), ())), preferred_element_type=F32)
    rows = zt[:, :ts] + zt[:, ts:2 * ts] + zt[:, 2 * ts:]
    return gates, csum_gd, csum_ml, rows


def _mlstm(p_ref, mlnw_ref, y_ref, ml_c, ml_m, gates, csum_ml, rows, project):
    ts = p_ref.shape[0]
    L = PAIR
    r_i = lax.broadcasted_iota(jnp.int32, (L, L), 0)
    c_i = lax.broadcasted_iota(jnp.int32, (L, L), 1)
    causal = r_i >= c_i
    ones_blk = jnp.ones((L, ML_DH), F32)
    units = [(h, c) for c in range(ts // L) for h in range(ML_HEADS)]

    st = {}
    for h, c in units:
        rs = slice(c * L, (c + 1) * L)
        q = p_ref[rs, OFF_ML_Q + h * ML_DH:OFF_ML_Q + (h + 1) * ML_DH]
        k = p_ref[rs, OFF_ML_K + h * ML_DH:OFF_ML_K + (h + 1) * ML_DH]
        li_c = gates[rs, LANE_ML_I + h:LANE_ML_I + h + 1]
        bq_c = csum_ml[rs, LANE_ML_F + h:LANE_ML_F + h + 1]
        r_row = (rows[LANE_ML_I + h:LANE_ML_I + h + 1, rs]
                 - rows[LANE_ML_F + h:LANE_ML_F + h + 1, rs])
        dm = jnp.where(causal, bq_c + r_row, -1e30)
        mx = jnp.max(dm, axis=-1, keepdims=True)
        g_last = bq_c[L - 1:L, :]
        ws = g_last - bq_c + li_c
        mws = jnp.max(ws, axis=0, keepdims=True)
        st[h, c] = dict(q=q, k=k, bq_c=bq_c, mx=mx, g_last=g_last, mws=mws,
                        pm=jnp.exp(dm - mx), wk=jnp.exp(ws - mws), qk=_dot_nt(q, k))
    yield
    for h, c in units:
        u = st[h, c]
        rs = slice(c * L, (c + 1) * L)
        v = p_ref[rs, OFF_ML_V + h * ML_DH:OFF_ML_V + (h + 1) * ML_DH]
        v_aug = jnp.concatenate([v, ones_blk], axis=1)
        u["intra"] = _dot(u["qk"] * u["pm"], v_aug)
        u["kv"] = _dot_tn(u["k"] * u["wk"], v_aug)

    yield
    c_aug = [ml_c[h] for h in range(ML_HEADS)]
    m_prev = [ml_m[h][0:1, 0:1] for h in range(ML_HEADS)]
    outs = [[] for _ in range(ML_HEADS)]
    for c in range(ts // L):
        qc = [_dot(st[h, c]["q"], c_aug[h]) for h in range(ML_HEADS)]
        for h in range(ML_HEADS):
            u = st[h, c]
            m_inter = u["bq_c"] + m_prev[h]
            m_t = jnp.maximum(m_inter, u["mx"])
            na = jnp.exp(u["mx"] - m_t) * u["intra"] + jnp.exp(m_inter - m_t) * qc[h]
            outs[h].append(na[:, :ML_DH] / jnp.maximum(jnp.abs(na[:, ML_DH:]), jnp.exp(-m_t)))
            m_new = jnp.maximum(u["g_last"] + m_prev[h], u["mws"])
            c_aug[h] = (jnp.exp(u["g_last"] + m_prev[h] - m_new) * c_aug[h]
                        + jnp.exp(u["mws"] - m_new) * u["kv"])
            m_prev[h] = m_new
        yield
    for h in range(ML_HEADS):
        sl = slice(h * ML_DH, (h + 1) * ML_DH)
        ml_c[h] = c_aug[h]
        ml_m[h] = jnp.broadcast_to(m_prev[h], ml_m.shape[1:])
        hh = jnp.concatenate(outs[h], axis=0)
        hn = hh * lax.rsqrt(jnp.mean(hh * hh, axis=-1, keepdims=True) + NORM_EPS)
        y_ml = (hn * mlnw_ref[0:1, sl]) * p_ref[:, OFF_ML_G + h * ML_DH:OFF_ML_G + (h + 1) * ML_DH]
        y_ref[:, RG_WIDTH + h * ML_DH:RG_WIDTH + (h + 1) * ML_DH] = y_ml.astype(y_ref.dtype)
    project(RG_WIDTH)


def _gdn(p_ref, gdnw_ref, y_ref, gd_s, gates, csum_gd, rows, project):
    ts = p_ref.shape[0]
    L = PAIR
    C = GD_CHUNK
    r_i = lax.broadcasted_iota(jnp.int32, (L, L), 0)
    c_i = lax.broadcasted_iota(jnp.int32, (L, L), 1)
    same = (r_i >= C) == (c_i >= C)
    incl = same & (r_i >= c_i)
    strict = same & (r_i > c_i)
    eye = (r_i == c_i).astype(F32)
    second = lax.broadcasted_iota(jnp.int32, (L, 1), 0) >= C
    zeros_c = jnp.zeros((C, GD_DV), F32)
    units = [(h, j) for j in range(ts // L) for h in range(GD_HEADS)]

    st = {}
    for h, j in units:
        rs = slice(j * L, (j + 1) * L)
        q = p_ref[rs, OFF_GD_Q + h * GD_DK:OFF_GD_Q + (h + 1) * GD_DK]
        k = p_ref[rs, OFF_GD_K + h * GD_DK:OFF_GD_K + (h + 1) * GD_DK]
        beta = gates[rs, LANE_GD_B + h:LANE_GD_B + h + 1]
        gc_c = csum_gd[rs, LANE_GD_A + h:LANE_GD_A + h + 1]
        gc_r = rows[LANE_GD_A + h:LANE_GD_A + h + 1, rs]
        gam = jnp.where(incl, jnp.exp(jnp.where(incl, gc_c - gc_r, 0.0)), 0.0)
        kb = k * beta
        egc = jnp.exp(gc_c)
        g_first = gc_c[C - 1:C, :]
        g_second = gc_c[L - 1:L, :]
        st[h, j] = dict(k=k, kb=kb, beta=beta, gam=gam, egc=egc, q_dec=q * egc,
                        g_last=(g_first, g_second),
                        k_dec=k * jnp.exp(jnp.where(second, g_second, g_first) - gc_c),
                        kq=_dot_nt(jnp.concatenate([kb, q], axis=0), k))
    yield
    for key in units:
        u = st[key]
        u["aqk"] = u["kq"][L:] * u["gam"]
        pw = jnp.where(strict, -(u["kq"][:L] * u["gam"]), 0.0)
        u["t_inv"] = eye + pw
        u["pw"] = _dot(pw, pw)
    yield
    span = 4
    while span < C:
        for key in units:
            u = st[key]
            both = _dot(jnp.concatenate([u["pw"], u["t_inv"]], axis=0), u["pw"])
            u["pw"] = both[:L]
            u["t_inv"] = u["t_inv"] + both[L:]
        span *= 2
        yield
    for key in units:
        u = st[key]
        u["t_inv"] = u["t_inv"] + _dot(u["t_inv"], u["pw"])
    yield
    for h, j in units:
        u = st[h, j]
        rs = slice(j * L, (j + 1) * L)
        v = p_ref[rs, OFF_GD_V + h * GD_DV:OFF_GD_V + (h + 1) * GD_DV]
        uw = _dot(u["t_inv"], jnp.concatenate([v * u["beta"], u["kb"] * u["egc"]], axis=1))
        u["u"] = uw[:, :GD_DV]
        u["w"] = uw[:, GD_DV:]

    yield
    state = [gd_s[h] for h in range(GD_HEADS)]
    outs = [[] for _ in range(GD_HEADS)]
    for j in range(ts // L):
        for half in range(2):
            hs = slice(half * C, (half + 1) * C)
            wq = [_dot(jnp.concatenate([st[h, j]["w"][hs], st[h, j]["q_dec"][hs]], axis=0), state[h])
                  for h in range(GD_HEADS)]
            v_new = [st[h, j]["u"][hs] - wq[h][:C] for h in range(GD_HEADS)]
            for h in range(GD_HEADS):
                v_pad = jnp.concatenate([v_new[h], zeros_c] if half == 0 else [zeros_c, v_new[h]],
                                        axis=0)
                outs[h].append(wq[h][C:] + _dot(st[h, j]["aqk"][hs], v_pad))
            for h in range(GD_HEADS):
                u = st[h, j]
                state[h] = (state[h] * jnp.exp(u["g_last"][half])
                            + _dot_tn(u["k_dec"][hs], v_new[h]))
            yield
    for h in range(GD_HEADS):
        gd_s[h] = state[h]
        o = jnp.concatenate(outs[h], axis=0)
        on = o * lax.rsqrt(jnp.mean(o * o, axis=-1, keepdims=True) + NORM_EPS)
        y_gd = on * gdnw_ref[0:1, :] * p_ref[:, OFF_GD_G + h * GD_DV:OFF_GD_G + (h + 1) * GD_DV]
        y_ref[:, RG_WIDTH + ML_WIDTH + h * GD_DV:RG_WIDTH + ML_WIDTH + (h + 1) * GD_DV] = (
            y_gd.astype(y_ref.dtype))
    project(RG_WIDTH + ML_WIDTH)


def _interleave(order, stages):
    live = dict(stages)

    def step(key):
        if key in live:
            try:
                next(live[key])
            except StopIteration:
                del live[key]

    for key in order:
        step(key)
    while live:
        for key in list(live):
            step(key)


def _layer_kernel(xn_ref, xc_ref, nw_ref, wa_ref, wb_ref, rgp_ref, rgw_ref, gp_ref, gdcw_ref,
                  wo_ref, fw_ref, mlnw_ref, gdnw_ref, cs_ref, o_ref,
                  p_s, hn_s, raw_z, xc_s, rg_tail, gd_tail, y_ref, rg_h, ml_c, ml_m, gd_s,
                  *, blocks_per_seq, final_norm):
    i = pl.program_id(0)
    slot = lax.rem(i, 2)
    p_new = p_s.at[slot]
    p_cur = p_s.at[1 - slot]

    @pl.when(i == 0)
    def _():
        p_s[1] = jnp.zeros(p_s.shape[1:], p_s.dtype)

    @pl.when(lax.rem(i, blocks_per_seq) == 0)
    def _():
        rg_tail[...] = jnp.zeros_like(rg_tail)
        gd_tail[...] = jnp.zeros_like(gd_tail)

    @pl.when(lax.rem(jnp.maximum(i - 1, 0), blocks_per_seq) == 0)
    def _():
        rg_h[...] = jnp.zeros_like(rg_h)
        ml_c[...] = jnp.zeros_like(ml_c)
        ml_m[...] = jnp.zeros_like(ml_m)
        gd_s[...] = jnp.zeros_like(gd_s)

    o_ref[...] = xc_ref[...]

    def project(k0):
        o_ref[...] += jnp.dot(y_ref[:, k0:k0 + W], wo_ref[k0:k0 + W, :],
                              preferred_element_type=F32)

    gates, csum_gd, csum_ml, rows = _gate_tables(p_cur, cs_ref)
    _interleave(LAYER_ORDER, dict(
        P=_inproj_stream(xn_ref, nw_ref, wa_ref, wb_ref, rgp_ref, rgw_ref, gp_ref, gdcw_ref,
                         p_new, hn_s, raw_z, xc_s, rg_tail, gd_tail),
        G=_gdn(p_cur, gdnw_ref, y_ref, gd_s, gates, csum_gd, rows, project),
        M=_mlstm(p_cur, mlnw_ref, y_ref, ml_c, ml_m, gates, csum_ml, rows, project),
        R=_rglru(p_cur, y_ref, rg_h, project)))
    if final_norm:
        xn = o_ref[...]
        r = lax.rsqrt(jnp.mean(xn * xn, axis=-1, keepdims=True) + NORM_EPS)
        o_ref[...] = (xn * r) * fw_ref[...]


def _cumsum_matrix(ts):
    r = jnp.arange(ts)[:, None]
    c = jnp.arange(ts)[None, :]
    m64 = (r >= c) & (r // GD_CHUNK == c // GD_CHUNK)
    m128 = (r >= c) & (r // PAIR == c // PAIR)
    return jnp.concatenate([m64, m128], axis=0).astype(BF16)


def _layer(x2, norm_w, w_a, w_b, rgp, rgw, gp, gdcw, w_out, final_w, mlnw, gdnw, cs, layer, seq,
           final_norm):
    t = x2.shape[0]
    ts = TS_BLOCK
    nb = t // ts
    per_layer = lambda a, **kw: pl.BlockSpec((None,) + a.shape[1:],
                                             lambda i: (layer,) + (0,) * (a.ndim - 1), **kw)
    once = dict(pipeline_mode=pl.Buffered(1))
    return pl.pallas_call(
        functools.partial(_layer_kernel, blocks_per_seq=seq // ts, final_norm=final_norm),
        grid=(nb + 1,),
        in_specs=[pl.BlockSpec((ts, D_MODEL), lambda i: (jnp.minimum(i, nb - 1), 0)),
                  pl.BlockSpec((ts, D_MODEL), lambda i: (jnp.maximum(i - 1, 0), 0)),
                  per_layer(norm_w),
                  pl.BlockSpec((None, D_MODEL, WCOL_SPLIT), lambda i: (layer, 0, 0), **once),
                  per_layer(w_b, **once), per_layer(rgp), per_layer(rgw), per_layer(gp),
                  per_layer(gdcw), per_layer(w_out, **once),
                  pl.BlockSpec(final_w.shape, lambda i: (0, 0)),
                  per_layer(mlnw), per_layer(gdnw), pl.BlockSpec(cs.shape, lambda i: (0, 0))],
        out_specs=pl.BlockSpec((ts, D_MODEL), lambda i: (jnp.maximum(i - 1, 0), 0)),
        out_shape=jax.ShapeDtypeStruct((t, D_MODEL), F32),
        scratch_shapes=[pltpu.VMEM((2, ts, D_PROJ), F32),
                        pltpu.VMEM((ts, D_MODEL), BF16),
                        pltpu.VMEM((ts, W), F32),
                        pltpu.VMEM((ts, RG_WIDTH), F32),
                        pltpu.VMEM((SUBLANES, RG_WIDTH), F32),
                        pltpu.VMEM((SUBLANES, GD_CONV), F32),
                        pltpu.VMEM((ts, D_MIX), BF16),
                        pltpu.VMEM((SUBLANES, RG_WIDTH), F32),
                        pltpu.VMEM((ML_HEADS, ML_DH, 2 * ML_DH), F32),
                        pltpu.VMEM((ML_HEADS, SUBLANES, LANES), F32),
                        pltpu.VMEM((GD_HEADS, GD_DK, GD_DV), F32)],
        compiler_params=pltpu.CompilerParams(dimension_semantics=("arbitrary",),
                                             vmem_limit_bytes=VMEM_LIMIT),
        name="layer",
    )(x2, x2, norm_w, w_a, w_b, rgp, rgw, gp, gdcw, w_out, final_w, mlnw, gdnw, cs)


def _w_in_tail(w_in):
    pad = jnp.zeros(w_in.shape[:-1] + (LANES - GATE_LANES,), w_in.dtype)
    return jnp.concatenate([
        w_in[..., SRC_GD:SRC_GD_GATES], w_in[..., SRC_ML_GATES:SRC_GD],
        w_in[..., SRC_GD_GATES:], pad], axis=-1)


def _rg_gate_blockdiag(gate_w):
    depth = gate_w.shape[0]
    nb = RG_BLOCKS // 2
    eye = jnp.eye(nb, dtype=gate_w.dtype)
    halves = []
    for hf in range(2):
        blk = gate_w[:, :, hf * nb:(hf + 1) * nb]
        bd = jnp.einsum('lgncd,nm->lgncmd', blk, eye)
        bd = bd.reshape(depth, 2, nb * RG_BLOCK, nb * RG_BLOCK)
        halves.append(jnp.concatenate([bd[:, 0], bd[:, 1]], axis=-1))
    return jnp.stack(halves, axis=1)


def _gate_params(ml_gate_b, gd_dt_bias, gd_a_log):
    depth = ml_gate_b.shape[0]
    zeros = lambda n: jnp.zeros((depth, n), F32)
    row0 = jnp.concatenate([ml_gate_b[:, 0], ml_gate_b[:, 1], gd_dt_bias,
                            zeros(LANES - LANE_GD_B)], axis=-1)
    row1 = jnp.concatenate([zeros(LANE_GD_A), gd_a_log, zeros(LANES - LANE_GD_B)], axis=-1)
    rest = jnp.zeros((depth, SUBLANES - 2, LANES), F32)
    return jnp.concatenate([row0[:, None], row1[:, None], rest], axis=1)


def kernel(x, norm_w, w_in, rg_conv_w, rg_conv_b, rg_gate_w, rg_gate_b, rg_lambda, ml_gate_b,
           ml_norm_w, gd_conv_w, gd_a_log, gd_dt_bias, gd_norm_w, w_out, final_norm_w):
    bsz, seq, _ = x.shape
    depth = w_in.shape[0]
    t = bsz * seq
    w_a = w_in.astype(BF16)
    w_b = _w_in_tail(w_in).astype(BF16)
    w_o = w_out.astype(BF16)
    rgp = jnp.concatenate([rg_conv_w, rg_conv_b[:, None], rg_gate_b, rg_lambda[:, None]],
                          axis=1)
    rgw = _rg_gate_blockdiag(rg_gate_w).astype(BF16)
    gp = _gate_params(ml_gate_b, gd_dt_bias, gd_a_log)
    cs = _cumsum_matrix(TS_BLOCK)

    x2 = x.reshape(t, D_MODEL)
    for l in range(depth):
        x2 = _layer(x2, norm_w[:, None], w_a, w_b, rgp, rgw, gp, gd_conv_w, w_o, final_norm_w[None],
                    ml_norm_w[:, None], gd_norm_w[:, None], cs, l, seq, l == depth - 1)
    return x2.reshape(bsz, seq, D_MODEL)
```

```python
import functools

import jax
import jax.numpy as jnp
from jax import lax
from jax.experimental import pallas as pl
from jax.experimental.pallas import tpu as pltpu

F32 = jnp.float32
BF16 = jnp.bfloat16

D_MODEL = 1024
CONV_K = 4
NORM_EPS = 1e-6
RG_WIDTH = 512
RG_BLOCKS = 8
RG_BLOCK = RG_WIDTH // RG_BLOCKS
RG_C = 8.0
ML_HEADS = 4
ML_DH = 128
ML_WIDTH = ML_HEADS * ML_DH
GD_HEADS = 4
GD_DK = 128
GD_DV = 128
GD_QK = GD_HEADS * GD_DK
GD_WIDTH = GD_HEADS * GD_DV
GD_CONV = 2 * GD_QK + GD_WIDTH
D_MIX = RG_WIDTH + ML_WIDTH + GD_WIDTH
GD_CHUNK = 64
PAIR = 2 * GD_CHUNK
SUBLANES = 8
LANES = 128

W = 512
WCOL_RG_X, WCOL_RG_Z = 0 * W, 1 * W
WCOL_ML_Q, WCOL_ML_K, WCOL_ML_V, WCOL_ML_O, WCOL_ML_Z = 2 * W, 3 * W, 4 * W, 5 * W, 6 * W
WCOL_GD_QKV, WCOL_GD_Z = 7 * W, 10 * W
WCOL_GATES = 11 * W
WCOL_SPLIT = WCOL_GD_QKV
D_PROJ = WCOL_GATES + LANES
OFF_RG_A, OFF_RG_B, OFF_RG_G = 0 * W, 1 * W, 2 * W
OFF_ML_Q, OFF_ML_K, OFF_ML_V, OFF_ML_G = 3 * W, 4 * W, 5 * W, 6 * W
OFF_GD_Q, OFF_GD_K, OFF_GD_V, OFF_GD_G = 7 * W, 8 * W, 9 * W, 10 * W
OFF_GATES = 11 * W
LANE_ML_I, LANE_ML_F, LANE_GD_A, LANE_GD_B = 0, 4, 8, 12
GATE_LANES = 16
SRC_ML_GATES = 2 * RG_WIDTH + 5 * ML_WIDTH
SRC_GD = SRC_ML_GATES + 2 * ML_HEADS
SRC_GD_GATES = SRC_GD + 2 * GD_QK + 2 * GD_WIDTH

VMEM_LIMIT = 60 * 1024 * 1024
TS_BLOCK = 256
LAYER_ORDER = "".join(k + "P" for k in "GRGMGRGMGRGMGRGMGRMGGGG")


def _dot(a, b):
    return jnp.dot(a.astype(BF16), b.astype(BF16), preferred_element_type=F32)


def _dot_nt(a, b):
    return lax.dot_general(a.astype(BF16), b.astype(BF16), (((1,), (1,)), ((), ())),
                           preferred_element_type=F32)


def _dot_tn(a, b):
    return lax.dot_general(a.astype(BF16), b.astype(BF16), (((0,), (0,)), ((), ())),
                           preferred_element_type=F32)


def _split3(x):
    hi = x.astype(BF16)
    r1 = x - hi.astype(F32)
    mid = r1.astype(BF16)
    lo = (r1 - mid.astype(F32)).astype(BF16)
    return hi, mid, lo


def _softplus(x):
    return jnp.maximum(x, 0.0) + jnp.log1p(jnp.exp(-jnp.abs(x)))


def _sigmoid(x):
    return 1.0 / (1.0 + jnp.exp(-x))


def _silu(x):
    return x * _sigmoid(x)


def _causal_conv(x_ref, col0, tail_ref, tcol0, cw_ref, width=W):
    ts = x_ref.shape[0]
    cols = slice(col0, col0 + width)
    tcols = slice(tcol0, tcol0 + width)
    head = jnp.concatenate([tail_ref[:, tcols], x_ref[0:SUBLANES, cols]], axis=0)
    acc = None
    acc0 = None
    for k in range(CONV_K):
        j = CONV_K - 1 - k
        wk = cw_ref[k:k + 1, tcols]
        term = x_ref[SUBLANES - j:ts - j, cols] * wk
        hs = head[SUBLANES:, :] if j == 0 else pltpu.roll(head, j, 0)[SUBLANES:, :]
        term0 = hs * wk
        acc = term if acc is None else acc + term
        acc0 = term0 if acc0 is None else acc0 + term0
    tail_ref[:, tcols] = x_ref[ts - SUBLANES:ts, cols]
    return jnp.concatenate([acc0, acc], axis=0)


def _inproj_stream(x_ref, nw_ref, wa_ref, wb_ref, rgp_ref, rgw_ref, gp_ref, gdcw_ref, p_ref,
                   hn_s, raw_z, xc_s, rg_tail, gd_tail):
    x = x_ref[...]
    r = lax.rsqrt(jnp.mean(x * x, axis=-1, keepdims=True) + NORM_EPS)
    hn_s[...] = ((x * r) * nw_ref[...]).astype(BF16)

    H = W // 2

    def proj(col, width):
        if col < WCOL_SPLIT:
            w = wa_ref[:, col:col + width]
        else:
            w = wb_ref[:, col - WCOL_SPLIT:col - WCOL_SPLIT + width]
        return jnp.dot(hn_s[...], w, preferred_element_type=F32)

    def mm(dst_ref, dst, src, scale=None):
        def run():
            y = proj(src, H)
            dst_ref[:, dst:dst + H] = y if scale is None else y * scale
        return run

    def mm_rg_gates(c):
        def run():
            g = jnp.dot(xc_s[:, c:c + H].astype(BF16), rgw_ref[c // H],
                        preferred_element_type=F32)
            p_ref[:, OFF_RG_B + c:OFF_RG_B + c + H] = g[:, :H]
            raw_z[:, c:c + H] = g[:, H:]
        return run

    def ep_rg_conv(c):
        def run():
            xc_s[:, c:c + H] = (_causal_conv(p_ref, OFF_RG_A + c, rg_tail, c, rgp_ref, H)
                                + rgp_ref[4:5, c:c + H])
        return run

    def ep_rg_ab(c):
        def run():
            gr = p_ref[:, OFF_RG_B + c:OFF_RG_B + c + H] + rgp_ref[5:6, c:c + H]
            gi = raw_z[:, c:c + H] + rgp_ref[6:7, c:c + H]
            log_a = (-RG_C * _sigmoid(gr)) * _softplus(-rgp_ref[7:8, c:c + H])
            a = jnp.exp(log_a)
            p_ref[:, OFF_RG_A + c:OFF_RG_A + c + H] = a
            p_ref[:, OFF_RG_B + c:OFF_RG_B + c + H] = (
                jnp.sqrt(-jnp.tanh(log_a) * (a * a + 1.0)) * (_sigmoid(gi) * xc_s[:, c:c + H]))
        return run

    def ep_silu(off):
        def run():
            p_ref[:, off:off + H] = _silu(p_ref[:, off:off + H])
        return run

    def ep_gd_conv(off, tcol, norm_scale):
        def run():
            y = _silu(_causal_conv(p_ref, off, gd_tail, tcol, gdcw_ref, H))
            if norm_scale is not None:
                heads = []
                for h in range(H // GD_DK):
                    yh = y[:, h * GD_DK:(h + 1) * GD_DK]
                    yh = yh * lax.rsqrt(jnp.sum(yh * yh, axis=-1, keepdims=True) + NORM_EPS)
                    heads.append(yh if norm_scale == 1.0 else yh * norm_scale)
                y = jnp.concatenate(heads, axis=1)
            p_ref[:, off:off + H] = y
        return run

    def ep_ml_gate(c):
        def run():
            p_ref[:, OFF_ML_G + c:OFF_ML_G + c + H] = (
                _sigmoid(p_ref[:, OFF_ML_G + c:OFF_ML_G + c + H]) * _silu(raw_z[:, c:c + H]))
        return run

    def mm_gates():
        p_ref[:, OFF_GATES:OFF_GATES + LANES] = proj(WCOL_GATES, LANES)

    def ep_gates():
        gates = p_ref[:, OFF_GATES:OFF_GATES + LANES] + gp_ref[0:1, :]
        lane = lax.broadcasted_iota(jnp.int32, gates.shape, 1)
        is_i = lane < LANE_ML_F
        is_f = (lane >= LANE_ML_F) & (lane < LANE_ML_F + ML_HEADS)
        is_a = (lane >= LANE_GD_A) & (lane < LANE_GD_A + GD_HEADS)
        log_f = -_softplus(-gates)
        g_gd = -jnp.exp(gp_ref[1:2, :]) * _softplus(gates)
        p_ref[:, OFF_GATES:OFF_GATES + LANES] = jnp.where(
            is_i, gates, jnp.where(is_f, log_f, jnp.where(is_a, g_gd, _sigmoid(gates))))

    units = [([mm_gates], ep_gates)]
    for c in range(0, W, H):
        units += [
            ([mm(p_ref, OFF_RG_A + c, WCOL_RG_X + c)], ep_rg_conv(c)),
            ([mm(p_ref, OFF_GD_Q + c, WCOL_GD_QKV + c)],
             ep_gd_conv(OFF_GD_Q + c, c, GD_DK ** -0.5)),
            ([mm(p_ref, OFF_ML_Q + c, WCOL_ML_Q + c, ML_DH ** -0.5)], None),
            ([mm_rg_gates(c)], ep_rg_ab(c)),
            ([mm(p_ref, OFF_GD_K + c, WCOL_GD_QKV + W + c)], ep_gd_conv(OFF_GD_K + c, W + c, 1.0)),
            ([mm(p_ref, OFF_ML_K + c, WCOL_ML_K + c)], None),
            ([mm(p_ref, OFF_GD_V + c, WCOL_GD_QKV + 2 * W + c)],
             ep_gd_conv(OFF_GD_V + c, 2 * W + c, None)),
            ([mm(p_ref, OFF_ML_V + c, WCOL_ML_V + c)], None),
            ([mm(p_ref, OFF_RG_G + c, WCOL_RG_Z + c)], ep_silu(OFF_RG_G + c)),
            ([mm(p_ref, OFF_ML_G + c, WCOL_ML_O + c), mm(raw_z, c, WCOL_ML_Z + c)], ep_ml_gate(c)),
            ([mm(p_ref, OFF_GD_G + c, WCOL_GD_Z + c)], ep_silu(OFF_GD_G + c)),
        ]
    pending = None
    for mms, ep in units:
        for run in mms:
            run()
        if pending is not None:
            pending()
        pending = ep
        yield
    if pending is not None:
        pending()


def _rglru(p_ref, y_ref, rg_h, project):
    ts = p_ref.shape[0]
    ng = ts // SUBLANES
    a3 = p_ref[:, OFF_RG_A:OFF_RG_A + W].reshape(ng, SUBLANES, RG_WIDTH)
    b3 = p_ref[:, OFF_RG_B:OFF_RG_B + W].reshape(ng, SUBLANES, RG_WIDTH)
    sub = lax.broadcasted_iota(jnp.int32, (ng, SUBLANES, RG_WIDTH), 1)
    d = 1
    while d < SUBLANES:
        a_sh = pltpu.roll(a3, d, 1)
        b_sh = pltpu.roll(b3, d, 1)
        valid = sub >= d
        b3 = jnp.where(valid, a3 * b_sh + b3, b3)
        a3 = jnp.where(valid, a3 * a_sh, a3)
        d *= 2
        yield
    carry = rg_h[0:1, :]
    groups = []
    for g in range(ng):
        hg = b3[g] + a3[g] * carry
        groups.append(hg)
        carry = hg[SUBLANES - 1:SUBLANES, :]
    rg_h[...] = jnp.broadcast_to(carry, rg_h.shape)
    yield
    h = jnp.concatenate(groups, axis=0)
    y_ref[:, 0:RG_WIDTH] = (h * p_ref[:, OFF_RG_G:OFF_RG_G + W]).astype(y_ref.dtype)
    project(0)


def _gate_tables(p_ref, cs_ref):
    ts = p_ref.shape[0]
    gates = p_ref[:, OFF_GATES:OFF_GATES + LANES]
    lane = lax.broadcasted_iota(jnp.int32, (ts, LANES), 1)
    is_i = lane < LANE_ML_F
    is_f = (lane >= LANE_ML_F) & (lane < LANE_ML_F + ML_HEADS)
    is_a = (lane >= LANE_GD_A) & (lane < LANE_GD_A + GD_HEADS)
    x = jnp.where(is_f | is_a, gates, 0.0)
    hi, mid, lo = _split3(x)
    packed = (hi.astype(F32) + pltpu.roll(mid.astype(F32), GATE_LANES, 1)
              + pltpu.roll(lo.astype(F32), 2 * GATE_LANES, 1))
    cs = jnp.dot(cs_ref[...], packed.astype(BF16), preferred_element_type=F32)
    cs = cs + pltpu.roll(cs, LANES - GATE_LANES, 1) + pltpu.roll(cs, LANES - 2 * GATE_LANES, 1)
    csum_gd = cs[:ts]
    csum_ml = cs[ts:]
    z = jnp.where(is_i, gates, jnp.where(is_f, csum_ml, jnp.where(is_a, csum_gd, 0.0)))
    zs = jnp.concatenate(_split3(z), axis=0)
    sel = (lax.broadcasted_iota(jnp.int32, (GATE_LANES, LANES), 0)
           == lax.broadcasted_iota(jnp.int32, (GATE_LANES, LANES), 1)).astype(BF16)
    zt = lax.dot_general(sel, zs, (((1,), (1,)), ((), ())), preferred_element_type=F32)
    rows = zt[:, :ts] + zt[:, ts:2 * ts] + zt[:, 2 * ts:]
    return gates, csum_gd, csum_ml, rows


def _mlstm(p_ref, mlnw_ref, y_ref, ml_c, ml_m, gates, csum_ml, rows, project):
    ts = p_ref.shape[0]
    L = PAIR
    r_i = lax.broadcasted_iota(jnp.int32, (L, L), 0)
    c_i = lax.broadcasted_iota(jnp.int32, (L, L), 1)
    causal = r_i >= c_i
    ones_blk = jnp.ones((L, ML_DH), F32)
    units = [(h, c) for c in range(ts // L) for h in range(ML_HEADS)]

    st = {}
    for h, c in units:
        rs = slice(c * L, (c + 1) * L)
        q = p_ref[rs, OFF_ML_Q + h * ML_DH:OFF_ML_Q + (h + 1) * ML_DH]
        k = p_ref[rs, OFF_ML_K + h * ML_DH:OFF_ML_K + (h + 1) * ML_DH]
        li_c = gates[rs, LANE_ML_I + h:LANE_ML_I + h + 1]
        bq_c = csum_ml[rs, LANE_ML_F + h:LANE_ML_F + h + 1]
        r_row = (rows[LANE_ML_I + h:LANE_ML_I + h + 1, rs]
                 - rows[LANE_ML_F + h:LANE_ML_F + h + 1, rs])
        dm = jnp.where(causal, bq_c + r_row, -1e30)
        mx = jnp.max(dm, axis=-1, keepdims=True)
        g_last = bq_c[L - 1:L, :]
        ws = g_last - bq_c + li_c
        mws = jnp.max(ws, axis=0, keepdims=True)
        st[h, c] = dict(q=q, k=k, bq_c=bq_c, mx=mx, g_last=g_last, mws=mws,
                        pm=jnp.exp(dm - mx), wk=jnp.exp(ws - mws), qk=_dot_nt(q, k))
    yield
    for h, c in units:
        u = st[h, c]
        rs = slice(c * L, (c + 1) * L)
        v = p_ref[rs, OFF_ML_V + h * ML_DH:OFF_ML_V + (h + 1) * ML_DH]
        v_aug = jnp.concatenate([v, ones_blk], axis=1)
        u["intra"] = _dot(u["qk"] * u["pm"], v_aug)
        u["kv"] = _dot_tn(u["k"] * u["wk"], v_aug)

    yield
    c_aug = [ml_c[h] for h in range(ML_HEADS)]
    m_prev = [ml_m[h][0:1, 0:1] for h in range(ML_HEADS)]
    outs = [[] for _ in range(ML_HEADS)]
    for c in range(ts // L):
        qc = [_dot(st[h, c]["q"], c_aug[h]) for h in range(ML_HEADS)]
        for h in range(ML_HEADS):
            u = st[h, c]
            m_inter = u["bq_c"] + m_prev[h]
            m_t = jnp.maximum(m_inter, u["mx"])
            na = jnp.exp(u["mx"] - m_t) * u["intra"] + jnp.exp(m_inter - m_t) * qc[h]
            outs[h].append(na[:, :ML_DH] / jnp.maximum(jnp.abs(na[:, ML_DH:]), jnp.exp(-m_t)))
            m_new = jnp.maximum(u["g_last"] + m_prev[h], u["mws"])
            c_aug[h] = (jnp.exp(u["g_last"] + m_prev[h] - m_new) * c_aug[h]
                        + jnp.exp(u["mws"] - m_new) * u["kv"])
            m_prev[h] = m_new
        yield
    for h in range(ML_HEADS):
        sl = slice(h * ML_DH, (h + 1) * ML_DH)
        ml_c[h] = c_aug[h]
        ml_m[h] = jnp.broadcast_to(m_prev[h], ml_m.shape[1:])
        hh = jnp.concatenate(outs[h], axis=0)
        hn = hh * lax.rsqrt(jnp.mean(hh * hh, axis=-1, keepdims=True) + NORM_EPS)
        y_ml = (hn * mlnw_ref[0:1, sl]) * p_ref[:, OFF_ML_G + h * ML_DH:OFF_ML_G + (h + 1) * ML_DH]
        y_ref[:, RG_WIDTH + h * ML_DH:RG_WIDTH + (h + 1) * ML_DH] = y_ml.astype(y_ref.dtype)
    project(RG_WIDTH)


def _gdn(p_ref, gdnw_ref, y_ref, gd_s, gates, csum_gd, rows, project):
    ts = p_ref.shape[0]
    L = PAIR
    C = GD_CHUNK
    r_i = lax.broadcasted_iota(jnp.int32, (L, L), 0)
    c_i = lax.broadcasted_iota(jnp.int32, (L, L), 1)
    same = (r_i >= C) == (c_i >= C)
    incl = same & (r_i >= c_i)
    strict = same & (r_i > c_i)
    eye = (r_i == c_i).astype(F32)
    second = lax.broadcasted_iota(jnp.int32, (L, 1), 0) >= C
    zeros_c = jnp.zeros((C, GD_DV), F32)
    units = [(h, j) for j in range(ts // L) for h in range(GD_HEADS)]

    st = {}
    for h, j in units:
        rs = slice(j * L, (j + 1) * L)
        q = p_ref[rs, OFF_GD_Q + h * GD_DK:OFF_GD_Q + (h + 1) * GD_DK]
        k = p_ref[rs, OFF_GD_K + h * GD_DK:OFF_GD_K + (h + 1) * GD_DK]
        beta = gates[rs, LANE_GD_B + h:LANE_GD_B + h + 1]
        gc_c = csum_gd[rs, LANE_GD_A + h:LANE_GD_A + h + 1]
        gc_r = rows[LANE_GD_A + h:LANE_GD_A + h + 1, rs]
        gam = jnp.where(incl, jnp.exp(jnp.where(incl, gc_c - gc_r, 0.0)), 0.0)
        kb = k * beta
        egc = jnp.exp(gc_c)
        g_first = gc_c[C - 1:C, :]
        g_second = gc_c[L - 1:L, :]
        st[h, j] = dict(q=q, k=k, kb=kb, beta=beta, gam=gam, egc=egc, q_dec=q * egc,
                        g_last=(g_first, g_second),
                        k_dec=k * jnp.exp(jnp.where(second, g_second, g_first) - gc_c))

    zeros_l = jnp.zeros((L, L), F32)
    side = lambda a, b: jnp.concatenate([a, b], axis=1)
    bdiag = lambda a, b: jnp.concatenate([side(a, zeros_l), side(zeros_l, b)], axis=0)
    pairs = [(st[a, j], st[a + 1, j]) for j in range(ts // L) for a in range(0, GD_HEADS, 2)]

    for ua, ub in pairs:
        lhs = jnp.concatenate([side(ua["kb"], ub["kb"]), side(ua["q"], ub["q"])], axis=0)
        kq = _dot_nt(lhs, bdiag(ua["k"], ub["k"]))
        for u, cs_ in ((ua, slice(0, L)), (ub, slice(L, 2 * L))):
            u["aqk"] = kq[L:, cs_] * u["gam"]
            u["pw"] = jnp.where(strict, -(kq[:L, cs_] * u["gam"]), 0.0)
            u["t_inv"] = eye + u["pw"]
    yield
    for ua, ub in pairs:
        sq = _dot(side(ua["pw"], ub["pw"]), bdiag(ua["pw"], ub["pw"]))
        ua["pw"], ub["pw"] = sq[:, :L], sq[:, L:]
    yield
    span = 4
    while span < C:
        for ua, ub in pairs:
            lhs = jnp.concatenate([side(ua["pw"], ub["pw"]), side(ua["t_inv"], ub["t_inv"])],
                                  axis=0)
            both = _dot(lhs, bdiag(ua["pw"], ub["pw"]))
            ua["pw"], ub["pw"] = both[:L, :L], both[:L, L:]
            ua["t_inv"] = ua["t_inv"] + both[L:, :L]
            ub["t_inv"] = ub["t_inv"] + both[L:, L:]
        span *= 2
        yield
    for ua, ub in pairs:
        last = _dot(side(ua["t_inv"], ub["t_inv"]), bdiag(ua["pw"], ub["pw"]))
        ua["t_inv"] = ua["t_inv"] + last[:, :L]
        ub["t_inv"] = ub["t_inv"] + last[:, L:]
    yield
    for h, j in units:
        u = st[h, j]
        rs = slice(j * L, (j + 1) * L)
        v = p_ref[rs, OFF_GD_V + h * GD_DV:OFF_GD_V + (h + 1) * GD_DV]
        uw = _dot(u["t_inv"], jnp.concatenate([v * u["beta"], u["kb"] * u["egc"]], axis=1))
        u["u"] = uw[:, :GD_DV]
        u["w"] = uw[:, GD_DV:]

    yield
    state = [gd_s[h] for h in range(GD_HEADS)]
    outs = [[] for _ in range(GD_HEADS)]
    for j in range(ts // L):
        for half in range(2):
            hs = slice(half * C, (half + 1) * C)
            wq = [_dot(jnp.concatenate([st[h, j]["w"][hs], st[h, j]["q_dec"][hs]], axis=0), state[h])
                  for h in range(GD_HEADS)]
            v_new = [st[h, j]["u"][hs] - wq[h][:C] for h in range(GD_HEADS)]
            for h in range(GD_HEADS):
                v_pad = jnp.concatenate([v_new[h], zeros_c] if half == 0 else [zeros_c, v_new[h]],
                                        axis=0)
                outs[h].append(wq[h][C:] + _dot(st[h, j]["aqk"][hs], v_pad))
            for h in range(GD_HEADS):
                u = st[h, j]
                state[h] = (state[h] * jnp.exp(u["g_last"][half])
                            + _dot_tn(u["k_dec"][hs], v_new[h]))
            yield
    for h in range(GD_HEADS):
        gd_s[h] = state[h]
        o = jnp.concatenate(outs[h], axis=0)
        on = o * lax.rsqrt(jnp.mean(o * o, axis=-1, keepdims=True) + NORM_EPS)
        y_gd = on * gdnw_ref[0:1, :] * p_ref[:, OFF_GD_G + h * GD_DV:OFF_GD_G + (h + 1) * GD_DV]
        y_ref[:, RG_WIDTH + ML_WIDTH + h * GD_DV:RG_WIDTH + ML_WIDTH + (h + 1) * GD_DV] = (
            y_gd.astype(y_ref.dtype))
    project(RG_WIDTH + ML_WIDTH)


def _interleave(order, stages):
    live = dict(stages)

    def step(key):
        if key in live:
            try:
                next(live[key])
            except StopIteration:
                del live[key]

    for key in order:
        step(key)
    while live:
        for key in list(live):
            step(key)


def _layer_kernel(xn_ref, xc_ref, nw_ref, wa_ref, wb_ref, rgp_ref, rgw_ref, gp_ref, gdcw_ref,
                  wo_ref, fw_ref, mlnw_ref, gdnw_ref, cs_ref, o_ref,
                  wa_s, p_s, hn_s, raw_z, xc_s, rg_tail, gd_tail, y_ref, rg_h, ml_c, ml_m, gd_s,
                  *, blocks_per_seq, final_norm):
    i = pl.program_id(0)
    slot = lax.rem(i, 2)
    p_new = p_s.at[slot]
    p_cur = p_s.at[1 - slot]

    @pl.when(i == 0)
    def _():
        p_s[1] = jnp.zeros(p_s.shape[1:], p_s.dtype)
        for c in range(0, WCOL_SPLIT, W):
            wa_s[:, c:c + W] = wa_ref[:, c:c + W].astype(BF16)

    @pl.when(lax.rem(i, blocks_per_seq) == 0)
    def _():
        rg_tail[...] = jnp.zeros_like(rg_tail)
        gd_tail[...] = jnp.zeros_like(gd_tail)

    @pl.when(lax.rem(jnp.maximum(i - 1, 0), blocks_per_seq) == 0)
    def _():
        rg_h[...] = jnp.zeros_like(rg_h)
        ml_c[...] = jnp.zeros_like(ml_c)
        ml_m[...] = jnp.zeros_like(ml_m)
        gd_s[...] = jnp.zeros_like(gd_s)

    o_ref[...] = xc_ref[...]

    def project(k0):
        o_ref[...] += jnp.dot(y_ref[:, k0:k0 + W], wo_ref[k0:k0 + W, :],
                              preferred_element_type=F32)

    gates, csum_gd, csum_ml, rows = _gate_tables(p_cur, cs_ref)
    _interleave(LAYER_ORDER, dict(
        P=_inproj_stream(xn_ref, nw_ref, wa_s, wb_ref, rgp_ref, rgw_ref, gp_ref, gdcw_ref,
                         p_new, hn_s, raw_z, xc_s, rg_tail, gd_tail),
        G=_gdn(p_cur, gdnw_ref, y_ref, gd_s, gates, csum_gd, rows, project),
        M=_mlstm(p_cur, mlnw_ref, y_ref, ml_c, ml_m, gates, csum_ml, rows, project),
        R=_rglru(p_cur, y_ref, rg_h, project)))
    if final_norm:
        xn = o_ref[...]
        r = lax.rsqrt(jnp.mean(xn * xn, axis=-1, keepdims=True) + NORM_EPS)
        o_ref[...] = (xn * r) * fw_ref[...]


def _cumsum_matrix(ts):
    r = jnp.arange(ts)[:, None]
    c = jnp.arange(ts)[None, :]
    m64 = (r >= c) & (r // GD_CHUNK == c // GD_CHUNK)
    m128 = (r >= c) & (r // PAIR == c // PAIR)
    return jnp.concatenate([m64, m128], axis=0).astype(BF16)


def _layer(x2, norm_w, w_a, w_b, rgp, rgw, gp, gdcw, w_out, final_w, mlnw, gdnw, cs, layer, seq,
           final_norm):
    t = x2.shape[0]
    ts = TS_BLOCK
    nb = t // ts
    per_layer = lambda a, **kw: pl.BlockSpec((None,) + a.shape[1:],
                                             lambda i: (layer,) + (0,) * (a.ndim - 1), **kw)
    once = dict(pipeline_mode=pl.Buffered(1))
    return pl.pallas_call(
        functools.partial(_layer_kernel, blocks_per_seq=seq // ts, final_norm=final_norm),
        grid=(nb + 1,),
        in_specs=[pl.BlockSpec((ts, D_MODEL), lambda i: (jnp.minimum(i, nb - 1), 0)),
                  pl.BlockSpec((ts, D_MODEL), lambda i: (jnp.maximum(i - 1, 0), 0)),
                  per_layer(norm_w),
                  pl.BlockSpec((None, D_MODEL, WCOL_SPLIT), lambda i: (layer, 0, 0), **once),
                  per_layer(w_b, **once), per_layer(rgp), per_layer(rgw), per_layer(gp),
                  per_layer(gdcw), per_layer(w_out, **once),
                  pl.BlockSpec(final_w.shape, lambda i: (0, 0)),
                  per_layer(mlnw), per_layer(gdnw), pl.BlockSpec(cs.shape, lambda i: (0, 0))],
        out_specs=pl.BlockSpec((ts, D_MODEL), lambda i: (jnp.maximum(i - 1, 0), 0)),
        out_shape=jax.ShapeDtypeStruct((t, D_MODEL), F32),
        scratch_shapes=[pltpu.VMEM((D_MODEL, WCOL_SPLIT), BF16),
                        pltpu.VMEM((2, ts, D_PROJ), F32),
                        pltpu.VMEM((ts, D_MODEL), BF16),
                        pltpu.VMEM((ts, W), F32),
                        pltpu.VMEM((ts, RG_WIDTH), F32),
                        pltpu.VMEM((SUBLANES, RG_WIDTH), F32),
                        pltpu.VMEM((SUBLANES, GD_CONV), F32),
                        pltpu.VMEM((ts, D_MIX), BF16),
                        pltpu.VMEM((SUBLANES, RG_WIDTH), F32),
                        pltpu.VMEM((ML_HEADS, ML_DH, 2 * ML_DH), F32),
                        pltpu.VMEM((ML_HEADS, SUBLANES, LANES), F32),
                        pltpu.VMEM((GD_HEADS, GD_DK, GD_DV), F32)],
        compiler_params=pltpu.CompilerParams(dimension_semantics=("arbitrary",),
                                             vmem_limit_bytes=VMEM_LIMIT),
        name="layer",
    )(x2, x2, norm_w, w_a, w_b, rgp, rgw, gp, gdcw, w_out, final_w, mlnw, gdnw, cs)


def _w_in_tail(w_in):
    pad = jnp.zeros(w_in.shape[:-1] + (LANES - GATE_LANES,), w_in.dtype)
    return jnp.concatenate([
        w_in[..., SRC_GD:SRC_GD_GATES], w_in[..., SRC_ML_GATES:SRC_GD],
        w_in[..., SRC_GD_GATES:], pad], axis=-1)


def _rg_gate_blockdiag(gate_w):
    depth = gate_w.shape[0]
    nb = RG_BLOCKS // 2
    eye = jnp.eye(nb, dtype=gate_w.dtype)
    halves = []
    for hf in range(2):
        blk = gate_w[:, :, hf * nb:(hf + 1) * nb]
        bd = jnp.einsum('lgncd,nm->lgncmd', blk, eye)
        bd = bd.reshape(depth, 2, nb * RG_BLOCK, nb * RG_BLOCK)
        halves.append(jnp.concatenate([bd[:, 0], bd[:, 1]], axis=-1))
    return jnp.stack(halves, axis=1)


def _gate_params(ml_gate_b, gd_dt_bias, gd_a_log):
    depth = ml_gate_b.shape[0]
    zeros = lambda n: jnp.zeros((depth, n), F32)
    row0 = jnp.concatenate([ml_gate_b[:, 0], ml_gate_b[:, 1], gd_dt_bias,
                            zeros(LANES - LANE_GD_B)], axis=-1)
    row1 = jnp.concatenate([zeros(LANE_GD_A), gd_a_log, zeros(LANES - LANE_GD_B)], axis=-1)
    rest = jnp.zeros((depth, SUBLANES - 2, LANES), F32)
    return jnp.concatenate([row0[:, None], row1[:, None], rest], axis=1)


def kernel(x, norm_w, w_in, rg_conv_w, rg_conv_b, rg_gate_w, rg_gate_b, rg_lambda, ml_gate_b,
           ml_norm_w, gd_conv_w, gd_a_log, gd_dt_bias, gd_norm_w, w_out, final_norm_w):
    bsz, seq, _ = x.shape
    depth = w_in.shape[0]
    t = bsz * seq
    w_b = _w_in_tail(w_in).astype(BF16)
    w_o = w_out.astype(BF16)
    rgp = jnp.concatenate([rg_conv_w, rg_conv_b[:, None], rg_gate_b, rg_lambda[:, None]],
                          axis=1)
    rgw = _rg_gate_blockdiag(rg_gate_w).astype(BF16)
    gp = _gate_params(ml_gate_b, gd_dt_bias, gd_a_log)
    cs = _cumsum_matrix(TS_BLOCK)

    x2 = x.reshape(t, D_MODEL)
    for l in range(depth):
        x2 = _layer(x2, norm_w[:, None], w_in, w_b, rgp, rgw, gp, gd_conv_w, w_o, final_norm_w[None],
                    ml_norm_w[:, None], gd_norm_w[:, None], cs, l, seq, l == depth - 1)
    return x2.reshape(bsz, seq, D_MODEL)
```

```python
import functools

import jax
import jax.numpy as jnp
from jax import lax
from jax.experimental import pallas as pl
from jax.experimental.pallas import tpu as pltpu

F32 = jnp.float32
BF16 = jnp.bfloat16

D_MODEL = 1024
CONV_K = 4
NORM_EPS = 1e-6
RG_WIDTH = 512
RG_BLOCKS = 8
RG_BLOCK = RG_WIDTH // RG_BLOCKS
RG_C = 8.0
ML_HEADS = 4
ML_DH = 128
ML_WIDTH = ML_HEADS * ML_DH
GD_HEADS = 4
GD_DK = 128
GD_DV = 128
GD_QK = GD_HEADS * GD_DK
GD_WIDTH = GD_HEADS * GD_DV
GD_CONV = 2 * GD_QK + GD_WIDTH
D_MIX = RG_WIDTH + ML_WIDTH + GD_WIDTH
GD_CHUNK = 64
PAIR = 2 * GD_CHUNK
SUBLANES = 8
LANES = 128

W = 512
WCOL_RG_X, WCOL_RG_Z = 0 * W, 1 * W
WCOL_ML_Q, WCOL_ML_K, WCOL_ML_V, WCOL_ML_O, WCOL_ML_Z = 2 * W, 3 * W, 4 * W, 5 * W, 6 * W
WCOL_GD_QKV, WCOL_GD_Z = 7 * W, 10 * W
WCOL_GATES = 11 * W
WCOL_SPLIT = WCOL_GD_QKV
D_PROJ = WCOL_GATES + LANES
OFF_RG_A, OFF_RG_B, OFF_RG_G = 0 * W, 1 * W, 2 * W
OFF_ML_Q, OFF_ML_K, OFF_ML_V, OFF_ML_G = 3 * W, 4 * W, 5 * W, 6 * W
OFF_GD_Q, OFF_GD_K, OFF_GD_V, OFF_GD_G = 7 * W, 8 * W, 9 * W, 10 * W
OFF_GATES = 11 * W
LANE_ML_I, LANE_ML_F, LANE_GD_A, LANE_GD_B = 0, 4, 8, 12
GATE_LANES = 16
SRC_ML_GATES = 2 * RG_WIDTH + 5 * ML_WIDTH
SRC_GD = SRC_ML_GATES + 2 * ML_HEADS
SRC_GD_GATES = SRC_GD + 2 * GD_QK + 2 * GD_WIDTH

VMEM_LIMIT = 56 * 1024 * 1024
TS_BLOCK = 256
LAYER_ORDER = "GPP" * 9 + "MRPG" * 5


def _dot(a, b):
    return jnp.dot(a.astype(BF16), b.astype(BF16), preferred_element_type=F32)


def _dot_nt(a, b):
    return lax.dot_general(a.astype(BF16), b.astype(BF16), (((1,), (1,)), ((), ())),
                           preferred_element_type=F32)


def _dot_tn(a, b):
    return lax.dot_general(a.astype(BF16), b.astype(BF16), (((0,), (0,)), ((), ())),
                           preferred_element_type=F32)


def _split3(x):
    hi = x.astype(BF16)
    r1 = x - hi.astype(F32)
    mid = r1.astype(BF16)
    lo = (r1 - mid.astype(F32)).astype(BF16)
    return hi, mid, lo


def _softplus(x):
    return jnp.maximum(x, 0.0) + jnp.log1p(jnp.exp(-jnp.abs(x)))


def _sigmoid(x):
    return 1.0 / (1.0 + jnp.exp(-x))


def _silu(x):
    return x * _sigmoid(x)


def _causal_conv(x_ref, col0, tail_ref, tcol0, cw_ref, width=W):
    ts = x_ref.shape[0]
    cols = slice(col0, col0 + width)
    tcols = slice(tcol0, tcol0 + width)
    head = jnp.concatenate([tail_ref[:, tcols], x_ref[0:SUBLANES, cols]], axis=0)
    acc = None
    acc0 = None
    for k in range(CONV_K):
        j = CONV_K - 1 - k
        wk = cw_ref[k:k + 1, tcols]
        term = x_ref[SUBLANES - j:ts - j, cols] * wk
        hs = head[SUBLANES:, :] if j == 0 else pltpu.roll(head, j, 0)[SUBLANES:, :]
        term0 = hs * wk
        acc = term if acc is None else acc + term
        acc0 = term0 if acc0 is None else acc0 + term0
    tail_ref[:, tcols] = x_ref[ts - SUBLANES:ts, cols]
    return jnp.concatenate([acc0, acc], axis=0)


def _inproj_stream(x_ref, nw_ref, wa_ref, wb_ref, rgp_ref, rgw_ref, gp_ref, gdcw_ref, p_ref,
                   hn_s, raw_z, xc_s, rg_tail, gd_tail):
    x = x_ref[...]
    r = lax.rsqrt(jnp.mean(x * x, axis=-1, keepdims=True) + NORM_EPS)
    hn_s[...] = ((x * r) * nw_ref[...]).astype(BF16)

    H = W // 2

    def proj(col, width):
        if col < WCOL_SPLIT:
            w = wa_ref[:, col:col + width]
        else:
            w = wb_ref[:, col - WCOL_SPLIT:col - WCOL_SPLIT + width]
        return jnp.dot(hn_s[...], w, preferred_element_type=F32)

    def mm(dst_ref, dst, src, scale=None):
        def run():
            y = proj(src, H)
            dst_ref[:, dst:dst + H] = y if scale is None else y * scale
        return run

    def mm_rg_gates(c):
        def run():
            g = jnp.dot(xc_s[:, c:c + H].astype(BF16), rgw_ref[c // H],
                        preferred_element_type=F32)
            p_ref[:, OFF_RG_B + c:OFF_RG_B + c + H] = g[:, :H]
            raw_z[:, c:c + H] = g[:, H:]
        return run

    def ep_rg_conv(c):
        def run():
            xc_s[:, c:c + H] = (_causal_conv(p_ref, OFF_RG_A + c, rg_tail, c, rgp_ref, H)
                                + rgp_ref[4:5, c:c + H])
        return run

    def ep_rg_ab(c):
        def run():
            gr = p_ref[:, OFF_RG_B + c:OFF_RG_B + c + H] + rgp_ref[5:6, c:c + H]
            gi = raw_z[:, c:c + H] + rgp_ref[6:7, c:c + H]
            log_a = (-RG_C * _sigmoid(gr)) * _softplus(-rgp_ref[7:8, c:c + H])
            a = jnp.exp(log_a)
            p_ref[:, OFF_RG_A + c:OFF_RG_A + c + H] = a
            p_ref[:, OFF_RG_B + c:OFF_RG_B + c + H] = (
                jnp.sqrt(-jnp.tanh(log_a) * (a * a + 1.0)) * (_sigmoid(gi) * xc_s[:, c:c + H]))
        return run

    def ep_silu(off):
        def run():
            p_ref[:, off:off + H] = _silu(p_ref[:, off:off + H])
        return run

    def ep_gd_conv(off, tcol, norm_scale):
        def run():
            y = _silu(_causal_conv(p_ref, off, gd_tail, tcol, gdcw_ref, H))
            if norm_scale is not None:
                heads = []
                for h in range(H // GD_DK):
                    yh = y[:, h * GD_DK:(h + 1) * GD_DK]
                    yh = yh * lax.rsqrt(jnp.sum(yh * yh, axis=-1, keepdims=True) + NORM_EPS)
                    heads.append(yh if norm_scale == 1.0 else yh * norm_scale)
                y = jnp.concatenate(heads, axis=1)
            p_ref[:, off:off + H] = y
        return run

    def ep_ml_gate(c):
        def run():
            p_ref[:, OFF_ML_G + c:OFF_ML_G + c + H] = (
                _sigmoid(p_ref[:, OFF_ML_G + c:OFF_ML_G + c + H]) * _silu(raw_z[:, c:c + H]))
        return run

    def mm_gates():
        p_ref[:, OFF_GATES:OFF_GATES + LANES] = proj(WCOL_GATES, LANES)

    def ep_gates():
        gates = p_ref[:, OFF_GATES:OFF_GATES + LANES] + gp_ref[0:1, :]
        lane = lax.broadcasted_iota(jnp.int32, gates.shape, 1)
        is_i = lane < LANE_ML_F
        is_f = (lane >= LANE_ML_F) & (lane < LANE_ML_F + ML_HEADS)
        is_a = (lane >= LANE_GD_A) & (lane < LANE_GD_A + GD_HEADS)
        log_f = -_softplus(-gates)
        g_gd = -jnp.exp(gp_ref[1:2, :]) * _softplus(gates)
        p_ref[:, OFF_GATES:OFF_GATES + LANES] = jnp.where(
            is_i, gates, jnp.where(is_f, log_f, jnp.where(is_a, g_gd, _sigmoid(gates))))

    units = []
    for c in range(0, W, H):
        units += [
            ([mm(p_ref, OFF_RG_A + c, WCOL_RG_X + c)], ep_rg_conv(c)),
            ([mm(p_ref, OFF_GD_Q + c, WCOL_GD_QKV + c)],
             ep_gd_conv(OFF_GD_Q + c, c, GD_DK ** -0.5)),
            ([mm(p_ref, OFF_ML_Q + c, WCOL_ML_Q + c, ML_DH ** -0.5)], None),
            ([mm_rg_gates(c)], ep_rg_ab(c)),
            ([mm(p_ref, OFF_GD_K + c, WCOL_GD_QKV + W + c)], ep_gd_conv(OFF_GD_K + c, W + c, 1.0)),
            ([mm(p_ref, OFF_ML_K + c, WCOL_ML_K + c)], None),
            ([mm(p_ref, OFF_GD_V + c, WCOL_GD_QKV + 2 * W + c)],
             ep_gd_conv(OFF_GD_V + c, 2 * W + c, None)),
            ([mm(p_ref, OFF_ML_V + c, WCOL_ML_V + c)], None),
            ([mm(p_ref, OFF_RG_G + c, WCOL_RG_Z + c)], ep_silu(OFF_RG_G + c)),
            ([mm(p_ref, OFF_ML_G + c, WCOL_ML_O + c), mm(raw_z, c, WCOL_ML_Z + c)], ep_ml_gate(c)),
            ([mm(p_ref, OFF_GD_G + c, WCOL_GD_Z + c)], ep_silu(OFF_GD_G + c)),
        ]
    units.append(([mm_gates], ep_gates))
    pending = None
    for mms, ep in units:
        for run in mms:
            run()
        if pending is not None:
            pending()
        pending = ep
        yield
    if pending is not None:
        pending()


def _rglru(p_ref, y_ref, rg_h, project):
    ts = p_ref.shape[0]
    ng = ts // SUBLANES
    a3 = p_ref[:, OFF_RG_A:OFF_RG_A + W].reshape(ng, SUBLANES, RG_WIDTH)
    b3 = p_ref[:, OFF_RG_B:OFF_RG_B + W].reshape(ng, SUBLANES, RG_WIDTH)
    sub = lax.broadcasted_iota(jnp.int32, (ng, SUBLANES, RG_WIDTH), 1)
    d = 1
    while d < SUBLANES:
        a_sh = pltpu.roll(a3, d, 1)
        b_sh = pltpu.roll(b3, d, 1)
        valid = sub >= d
        b3 = jnp.where(valid, a3 * b_sh + b3, b3)
        a3 = jnp.where(valid, a3 * a_sh, a3)
        d *= 2
        yield
    carry = rg_h[0:1, :]
    groups = []
    for g in range(ng):
        hg = b3[g] + a3[g] * carry
        groups.append(hg)
        carry = hg[SUBLANES - 1:SUBLANES, :]
    rg_h[...] = jnp.broadcast_to(carry, rg_h.shape)
    yield
    h = jnp.concatenate(groups, axis=0)
    y_ref[:, 0:RG_WIDTH] = (h * p_ref[:, OFF_RG_G:OFF_RG_G + W]).astype(y_ref.dtype)
    project(0)


def _gate_tables(p_ref, cs_ref):
    ts = p_ref.shape[0]
    gates = p_ref[:, OFF_GATES:OFF_GATES + LANES]
    lane = lax.broadcasted_iota(jnp.int32, (ts, LANES), 1)
    is_i = lane < LANE_ML_F
    is_f = (lane >= LANE_ML_F) & (lane < LANE_ML_F + ML_HEADS)
    is_a = (lane >= LANE_GD_A) & (lane < LANE_GD_A + GD_HEADS)
    x = jnp.where(is_f | is_a, gates, 0.0)
    hi, mid, lo = _split3(x)
    packed = (hi.astype(F32) + pltpu.roll(mid.astype(F32), GATE_LANES, 1)
              + pltpu.roll(lo.astype(F32), 2 * GATE_LANES, 1))
    cs = jnp.dot(cs_ref[...], packed.astype(BF16), preferred_element_type=F32)
    cs = cs + pltpu.roll(cs, LANES - GATE_LANES, 1) + pltpu.roll(cs, LANES - 2 * GATE_LANES, 1)
    csum_gd = cs[:ts]
    csum_ml = cs[ts:]
    z = jnp.where(is_i, gates, jnp.where(is_f, csum_ml, jnp.where(is_a, csum_gd, 0.0)))
    zs = jnp.concatenate(_split3(z), axis=0)
    sel = (lax.broadcasted_iota(jnp.int32, (GATE_LANES, LANES), 0)
           == lax.broadcasted_iota(jnp.int32, (GATE_LANES, LANES), 1)).astype(BF16)
    zt = lax.dot_general(sel, zs, (((1,), (1,)), ((), ())), preferred_element_type=F32)
    rows = zt[:, :ts] + zt[:, ts:2 * ts] + zt[:, 2 * ts:]
    return gates, csum_gd, csum_ml, rows


def _mlstm(p_ref, mlnw_ref, y_ref, ml_c, ml_m, gates, csum_ml, rows, project):
    ts = p_ref.shape[0]
    L = PAIR
    r_i = lax.broadcasted_iota(jnp.int32, (L, L), 0)
    c_i = lax.broadcasted_iota(jnp.int32, (L, L), 1)
    causal = r_i >= c_i
    ones_blk = jnp.ones((L, ML_DH), F32)
    units = [(h, c) for c in range(ts // L) for h in range(ML_HEADS)]

    st = {}
    for h, c in units:
        rs = slice(c * L, (c + 1) * L)
        q = p_ref[rs, OFF_ML_Q + h * ML_DH:OFF_ML_Q + (h + 1) * ML_DH]
        k = p_ref[rs, OFF_ML_K + h * ML_DH:OFF_ML_K + (h + 1) * ML_DH]
        li_c = gates[rs, LANE_ML_I + h:LANE_ML_I + h + 1]
        bq_c = csum_ml[rs, LANE_ML_F + h:LANE_ML_F + h + 1]
        r_row = (rows[LANE_ML_I + h:LANE_ML_I + h + 1, rs]
                 - rows[LANE_ML_F + h:LANE_ML_F + h + 1, rs])
        dm = jnp.where(causal, bq_c + r_row, -1e30)
        mx = jnp.max(dm, axis=-1, keepdims=True)
        g_last = bq_c[L - 1:L, :]
        ws = g_last - bq_c + li_c
        mws = jnp.max(ws, axis=0, keepdims=True)
        st[h, c] = dict(q=q, k=k, bq_c=bq_c, mx=mx, g_last=g_last, mws=mws,
                        pm=jnp.exp(dm - mx), wk=jnp.exp(ws - mws), qk=_dot_nt(q, k))
    yield
    for h, c in units:
        u = st[h, c]
        rs = slice(c * L, (c + 1) * L)
        v = p_ref[rs, OFF_ML_V + h * ML_DH:OFF_ML_V + (h + 1) * ML_DH]
        v_aug = jnp.concatenate([v, ones_blk], axis=1)
        u["intra"] = _dot(u["qk"] * u["pm"], v_aug)
        u["kv"] = _dot_tn(u["k"] * u["wk"], v_aug)

    yield
    c_aug = [ml_c[h] for h in range(ML_HEADS)]
    m_prev = [ml_m[h][0:1, 0:1] for h in range(ML_HEADS)]
    outs = [[] for _ in range(ML_HEADS)]
    for c in range(ts // L):
        qc = [_dot(st[h, c]["q"], c_aug[h]) for h in range(ML_HEADS)]
        for h in range(ML_HEADS):
            u = st[h, c]
            m_inter = u["bq_c"] + m_prev[h]
            m_t = jnp.maximum(m_inter, u["mx"])
            na = jnp.exp(u["mx"] - m_t) * u["intra"] + jnp.exp(m_inter - m_t) * qc[h]
            outs[h].append(na[:, :ML_DH] / jnp.maximum(jnp.abs(na[:, ML_DH:]), jnp.exp(-m_t)))
            m_new = jnp.maximum(u["g_last"] + m_prev[h], u["mws"])
            c_aug[h] = (jnp.exp(u["g_last"] + m_prev[h] - m_new) * c_aug[h]
                        + jnp.exp(u["mws"] - m_new) * u["kv"])
            m_prev[h] = m_new
        yield
    for h in range(ML_HEADS):
        sl = slice(h * ML_DH, (h + 1) * ML_DH)
        ml_c[h] = c_aug[h]
        ml_m[h] = jnp.broadcast_to(m_prev[h], ml_m.shape[1:])
        hh = jnp.concatenate(outs[h], axis=0)
        hn = hh * lax.rsqrt(jnp.mean(hh * hh, axis=-1, keepdims=True) + NORM_EPS)
        y_ml = (hn * mlnw_ref[0:1, sl]) * p_ref[:, OFF_ML_G + h * ML_DH:OFF_ML_G + (h + 1) * ML_DH]
        y_ref[:, RG_WIDTH + h * ML_DH:RG_WIDTH + (h + 1) * ML_DH] = y_ml.astype(y_ref.dtype)
    project(RG_WIDTH)


def _gdn(p_ref, gdnw_ref, y_ref, gd_s, gates, csum_gd, rows, project):
    ts = p_ref.shape[0]
    L = PAIR
    C = GD_CHUNK
    r_i = lax.broadcasted_iota(jnp.int32, (L, L), 0)
    c_i = lax.broadcasted_iota(jnp.int32, (L, L), 1)
    same = (r_i >= C) == (c_i >= C)
    incl = same & (r_i >= c_i)
    strict = same & (r_i > c_i)
    eye = (r_i == c_i).astype(F32)
    second = lax.broadcasted_iota(jnp.int32, (L, 1), 0) >= C
    zeros_c = jnp.zeros((C, GD_DV), F32)
    units = [(h, j) for j in range(ts // L) for h in range(GD_HEADS)]

    st = {}
    for h, j in units:
        rs = slice(j * L, (j + 1) * L)
        q = p_ref[rs, OFF_GD_Q + h * GD_DK:OFF_GD_Q + (h + 1) * GD_DK]
        k = p_ref[rs, OFF_GD_K + h * GD_DK:OFF_GD_K + (h + 1) * GD_DK]
        beta = gates[rs, LANE_GD_B + h:LANE_GD_B + h + 1]
        gc_c = csum_gd[rs, LANE_GD_A + h:LANE_GD_A + h + 1]
        gc_r = rows[LANE_GD_A + h:LANE_GD_A + h + 1, rs]
        gam = jnp.where(incl, jnp.exp(jnp.where(incl, gc_c - gc_r, 0.0)), 0.0)
        kb = k * beta
        egc = jnp.exp(gc_c)
        g_first = gc_c[C - 1:C, :]
        g_second = gc_c[L - 1:L, :]
        st[h, j] = dict(k=k, kb=kb, beta=beta, gam=gam, egc=egc, q_dec=q * egc,
                        g_last=(g_first, g_second),
                        k_dec=k * jnp.exp(jnp.where(second, g_second, g_first) - gc_c),
                        kq=_dot_nt(jnp.concatenate([kb, q], axis=0), k))
    yield
    for key in units:
        u = st[key]
        u["aqk"] = u["kq"][L:] * u["gam"]
        pw = jnp.where(strict, -(u["kq"][:L] * u["gam"]), 0.0)
        u["t_inv"] = eye + pw
        u["pw"] = _dot(pw, pw)
    yield
    span = 4
    while span < C:
        for key in units:
            u = st[key]
            both = _dot(jnp.concatenate([u["pw"], u["t_inv"]], axis=0), u["pw"])
            u["pw"] = both[:L]
            u["t_inv"] = u["t_inv"] + both[L:]
        span *= 2
        yield
    for key in units:
        u = st[key]
        u["t_inv"] = u["t_inv"] + _dot(u["t_inv"], u["pw"])
    yield
    for h, j in units:
        u = st[h, j]
        rs = slice(j * L, (j + 1) * L)
        v = p_ref[rs, OFF_GD_V + h * GD_DV:OFF_GD_V + (h + 1) * GD_DV]
        uw = _dot(u["t_inv"], jnp.concatenate([v * u["beta"], u["kb"] * u["egc"]], axis=1))
        u["u"] = uw[:, :GD_DV]
        u["w"] = uw[:, GD_DV:]

    yield
    state = [gd_s[h] for h in range(GD_HEADS)]
    outs = [[] for _ in range(GD_HEADS)]
    for j in range(ts // L):
        for half in range(2):
            hs = slice(half * C, (half + 1) * C)
            wq = [_dot(jnp.concatenate([st[h, j]["w"][hs], st[h, j]["q_dec"][hs]], axis=0), state[h])
                  for h in range(GD_HEADS)]
            v_new = [st[h, j]["u"][hs] - wq[h][:C] for h in range(GD_HEADS)]
            for h in range(GD_HEADS):
                v_pad = jnp.concatenate([v_new[h], zeros_c] if half == 0 else [zeros_c, v_new[h]],
                                        axis=0)
                outs[h].append(wq[h][C:] + _dot(st[h, j]["aqk"][hs], v_pad))
            for h in range(GD_HEADS):
                u = st[h, j]
                state[h] = (state[h] * jnp.exp(u["g_last"][half])
                            + _dot_tn(u["k_dec"][hs], v_new[h]))
            yield
    for h in range(GD_HEADS):
        gd_s[h] = state[h]
        o = jnp.concatenate(outs[h], axis=0)
        on = o * lax.rsqrt(jnp.mean(o * o, axis=-1, keepdims=True) + NORM_EPS)
        y_gd = on * gdnw_ref[0:1, :] * p_ref[:, OFF_GD_G + h * GD_DV:OFF_GD_G + (h + 1) * GD_DV]
        y_ref[:, RG_WIDTH + ML_WIDTH + h * GD_DV:RG_WIDTH + ML_WIDTH + (h + 1) * GD_DV] = (
            y_gd.astype(y_ref.dtype))
    project(RG_WIDTH + ML_WIDTH)


def _interleave(order, stages):
    live = dict(stages)

    def step(key):
        if key in live:
            try:
                next(live[key])
            except StopIteration:
                del live[key]

    for key in order:
        step(key)
    while live:
        for key in list(live):
            step(key)


def _layer_kernel(xn_ref, xc_ref, nw_ref, wa_ref, wb_ref, rgp_ref, rgw_ref, gp_ref, gdcw_ref,
                  wo_ref, fw_ref, mlnw_ref, gdnw_ref, cs_ref, o_ref,
                  p_s, hn_s, raw_z, xc_s, rg_tail, gd_tail, y_ref, rg_h, ml_c, ml_m, gd_s,
                  *, blocks_per_seq, final_norm):
    i = pl.program_id(0)
    slot = lax.rem(i, 2)
    p_new = p_s.at[slot]
    p_cur = p_s.at[1 - slot]

    @pl.when(i == 0)
    def _():
        p_s[1] = jnp.zeros(p_s.shape[1:], p_s.dtype)

    @pl.when(lax.rem(i, blocks_per_seq) == 0)
    def _():
        rg_tail[...] = jnp.zeros_like(rg_tail)
        gd_tail[...] = jnp.zeros_like(gd_tail)

    @pl.when(lax.rem(jnp.maximum(i - 1, 0), blocks_per_seq) == 0)
    def _():
        rg_h[...] = jnp.zeros_like(rg_h)
        ml_c[...] = jnp.zeros_like(ml_c)
        ml_m[...] = jnp.zeros_like(ml_m)
        gd_s[...] = jnp.zeros_like(gd_s)

    o_ref[...] = xc_ref[...]

    def project(k0):
        o_ref[...] += jnp.dot(y_ref[:, k0:k0 + W], wo_ref[k0:k0 + W, :],
                              preferred_element_type=F32)

    gates, csum_gd, csum_ml, rows = _gate_tables(p_cur, cs_ref)
    _interleave(LAYER_ORDER, dict(
        P=_inproj_stream(xn_ref, nw_ref, wa_ref, wb_ref, rgp_ref, rgw_ref, gp_ref, gdcw_ref,
                         p_new, hn_s, raw_z, xc_s, rg_tail, gd_tail),
        G=_gdn(p_cur, gdnw_ref, y_ref, gd_s, gates, csum_gd, rows, project),
        M=_mlstm(p_cur, mlnw_ref, y_ref, ml_c, ml_m, gates, csum_ml, rows, project),
        R=_rglru(p_cur, y_ref, rg_h, project)))
    if final_norm:
        xn = o_ref[...]
        r = lax.rsqrt(jnp.mean(xn * xn, axis=-1, keepdims=True) + NORM_EPS)
        o_ref[...] = (xn * r) * fw_ref[...]


def _cumsum_matrix(ts):
    r = jnp.arange(ts)[:, None]
    c = jnp.arange(ts)[None, :]
    m64 = (r >= c) & (r // GD_CHUNK == c // GD_CHUNK)
    m128 = (r >= c) & (r // PAIR == c // PAIR)
    return jnp.concatenate([m64, m128], axis=0).astype(BF16)


def _layer(x2, norm_w, w_a, w_b, rgp, rgw, gp, gdcw, w_out, final_w, mlnw, gdnw, cs, layer, seq,
           final_norm):
    t = x2.shape[0]
    ts = TS_BLOCK
    nb = t // ts
    per_layer = lambda a, **kw: pl.BlockSpec((None,) + a.shape[1:],
                                             lambda i: (layer,) + (0,) * (a.ndim - 1), **kw)
    once = dict(pipeline_mode=pl.Buffered(1))
    return pl.pallas_call(
        functools.partial(_layer_kernel, blocks_per_seq=seq // ts, final_norm=final_norm),
        grid=(nb + 1,),
        in_specs=[pl.BlockSpec((ts, D_MODEL), lambda i: (jnp.minimum(i, nb - 1), 0)),
                  pl.BlockSpec((ts, D_MODEL), lambda i: (jnp.maximum(i - 1, 0), 0)),
                  per_layer(norm_w),
                  pl.BlockSpec((None, D_MODEL, WCOL_SPLIT), lambda i: (layer, 0, 0), **once),
                  per_layer(w_b, **once), per_layer(rgp), per_layer(rgw), per_layer(gp),
                  per_layer(gdcw), per_layer(w_out, **once),
                  pl.BlockSpec(final_w.shape, lambda i: (0, 0)),
                  per_layer(mlnw), per_layer(gdnw), pl.BlockSpec(cs.shape, lambda i: (0, 0))],
        out_specs=pl.BlockSpec((ts, D_MODEL), lambda i: (jnp.maximum(i - 1, 0), 0)),
        out_shape=jax.ShapeDtypeStruct((t, D_MODEL), F32),
        scratch_shapes=[pltpu.VMEM((2, ts, D_PROJ), F32),
                        pltpu.VMEM((ts, D_MODEL), BF16),
                        pltpu.VMEM((ts, W), F32),
                        pltpu.VMEM((ts, RG_WIDTH), F32),
                        pltpu.VMEM((SUBLANES, RG_WIDTH), F32),
                        pltpu.VMEM((SUBLANES, GD_CONV), F32),
                        pltpu.VMEM((ts, D_MIX), BF16),
                        pltpu.VMEM((SUBLANES, RG_WIDTH), F32),
                        pltpu.VMEM((ML_HEADS, ML_DH, 2 * ML_DH), F32),
                        pltpu.VMEM((ML_HEADS, SUBLANES, LANES), F32),
                        pltpu.VMEM((GD_HEADS, GD_DK, GD_DV), F32)],
        compiler_params=pltpu.CompilerParams(dimension_semantics=("arbitrary",),
                                             vmem_limit_bytes=VMEM_LIMIT),
        name="layer",
    )(x2, x2, norm_w, w_a, w_b, rgp, rgw, gp, gdcw, w_out, final_w, mlnw, gdnw, cs)


def _w_in_tail(w_in):
    pad = jnp.zeros(w_in.shape[:-1] + (LANES - GATE_LANES,), w_in.dtype)
    return jnp.concatenate([
        w_in[..., SRC_GD:SRC_GD_GATES], w_in[..., SRC_ML_GATES:SRC_GD],
        w_in[..., SRC_GD_GATES:], pad], axis=-1)


def _rg_gate_blockdiag(gate_w):
    depth = gate_w.shape[0]
    nb = RG_BLOCKS // 2
    eye = jnp.eye(nb, dtype=gate_w.dtype)
    halves = []
    for hf in range(2):
        blk = gate_w[:, :, hf * nb:(hf + 1) * nb]
        bd = jnp.einsum('lgncd,nm->lgncmd', blk, eye)
        bd = bd.reshape(depth, 2, nb * RG_BLOCK, nb * RG_BLOCK)
        halves.append(jnp.concatenate([bd[:, 0], bd[:, 1]], axis=-1))
    return jnp.stack(halves, axis=1)


def _gate_params(ml_gate_b, gd_dt_bias, gd_a_log):
    depth = ml_gate_b.shape[0]
    zeros = lambda n: jnp.zeros((depth, n), F32)
    row0 = jnp.concatenate([ml_gate_b[:, 0], ml_gate_b[:, 1], gd_dt_bias,
                            zeros(LANES - LANE_GD_B)], axis=-1)
    row1 = jnp.concatenate([zeros(LANE_GD_A), gd_a_log, zeros(LANES - LANE_GD_B)], axis=-1)
    rest = jnp.zeros((depth, SUBLANES - 2, LANES), F32)
    return jnp.concatenate([row0[:, None], row1[:, None], rest], axis=1)


def kernel(x, norm_w, w_in, rg_conv_w, rg_conv_b, rg_gate_w, rg_gate_b, rg_lambda, ml_gate_b,
           ml_norm_w, gd_conv_w, gd_a_log, gd_dt_bias, gd_norm_w, w_out, final_norm_w):
    bsz, seq, _ = x.shape
    depth = w_in.shape[0]
    t = bsz * seq
    w_a = w_in[..., :WCOL_SPLIT].astype(BF16)
    w_b = _w_in_tail(w_in).astype(BF16)
    w_o = w_out.astype(BF16)
    rgp = jnp.concatenate([rg_conv_w, rg_conv_b[:, None], rg_gate_b, rg_lambda[:, None]],
                          axis=1)
    rgw = _rg_gate_blockdiag(rg_gate_w).astype(BF16)
    gp = _gate_params(ml_gate_b, gd_dt_bias, gd_a_log)
    cs = _cumsum_matrix(TS_BLOCK)

    x2 = x.reshape(t, D_MODEL)
    for l in range(depth):
        x2 = _layer(x2, norm_w[:, None], w_a, w_b, rgp, rgw, gp, gd_conv_w, w_o, final_norm_w[None],
                    ml_norm_w[:, None], gd_norm_w[:, None], cs, l, seq, l == depth - 1)
    return x2.reshape(bsz, seq, D_MODEL)
```

```python
import functools

import jax
import jax.numpy as jnp
from jax import lax
from jax.experimental import pallas as pl
from jax.experimental.pallas import tpu as pltpu

F32 = jnp.float32
BF16 = jnp.bfloat16

D_MODEL = 1024
CONV_K = 4
NORM_EPS = 1e-6
RG_WIDTH = 512
RG_BLOCKS = 8
RG_BLOCK = RG_WIDTH // RG_BLOCKS
RG_C = 8.0
ML_HEADS = 4
ML_DH = 128
ML_WIDTH = ML_HEADS * ML_DH
GD_HEADS = 4
GD_DK = 128
GD_DV = 128
GD_QK = GD_HEADS * GD_DK
GD_WIDTH = GD_HEADS * GD_DV
GD_CONV = 2 * GD_QK + GD_WIDTH
D_MIX = RG_WIDTH + ML_WIDTH + GD_WIDTH
GD_CHUNK = 64
PAIR = 2 * GD_CHUNK
SUBLANES = 8
LANES = 128

W = 512
WCOL_RG_X, WCOL_RG_Z = 0 * W, 1 * W
WCOL_ML_Q, WCOL_ML_K, WCOL_ML_V, WCOL_ML_O, WCOL_ML_Z = 2 * W, 3 * W, 4 * W, 5 * W, 6 * W
WCOL_GD_QKV, WCOL_GD_Z = 7 * W, 10 * W
WCOL_GATES = 11 * W
WCOL_SPLIT = WCOL_GD_QKV
D_PROJ = WCOL_GATES + LANES
OFF_RG_A, OFF_RG_B, OFF_RG_G = 0 * W, 1 * W, 2 * W
OFF_ML_Q, OFF_ML_K, OFF_ML_V, OFF_ML_G = 3 * W, 4 * W, 5 * W, 6 * W
OFF_GD_Q, OFF_GD_K, OFF_GD_V, OFF_GD_G = 7 * W, 8 * W, 9 * W, 10 * W
OFF_GATES = 11 * W
LANE_ML_I, LANE_ML_F, LANE_GD_A, LANE_GD_B = 0, 4, 8, 12
GATE_LANES = 16
SRC_ML_GATES = 2 * RG_WIDTH + 5 * ML_WIDTH
SRC_GD = SRC_ML_GATES + 2 * ML_HEADS
SRC_GD_GATES = SRC_GD + 2 * GD_QK + 2 * GD_WIDTH

VMEM_LIMIT = 56 * 1024 * 1024
TS_BLOCK = 256
LAYER_ORDER = "".join(k + "P" for k in "GRGMGRGMGRGMGRGMGRMGGGG")


def _dot(a, b):
    return jnp.dot(a.astype(BF16), b.astype(BF16), preferred_element_type=F32)


def _dot_nt(a, b):
    return lax.dot_general(a.astype(BF16), b.astype(BF16), (((1,), (1,)), ((), ())),
                           preferred_element_type=F32)


def _dot_tn(a, b):
    return lax.dot_general(a.astype(BF16), b.astype(BF16), (((0,), (0,)), ((), ())),
                           preferred_element_type=F32)


def _split3(x):
    hi = x.astype(BF16)
    r1 = x - hi.astype(F32)
    mid = r1.astype(BF16)
    lo = (r1 - mid.astype(F32)).astype(BF16)
    return hi, mid, lo


def _softplus(x):
    return jnp.maximum(x, 0.0) + jnp.log1p(jnp.exp(-jnp.abs(x)))


def _sigmoid(x):
    return 1.0 / (1.0 + jnp.exp(-x))


def _silu(x):
    return x * _sigmoid(x)


def _causal_conv(x_ref, col0, tail_ref, tcol0, cw_ref, width=W):
    ts = x_ref.shape[0]
    cols = slice(col0, col0 + width)
    tcols = slice(tcol0, tcol0 + width)
    head = jnp.concatenate([tail_ref[:, tcols], x_ref[0:SUBLANES, cols]], axis=0)
    acc = None
    acc0 = None
    for k in range(CONV_K):
        j = CONV_K - 1 - k
        wk = cw_ref[k:k + 1, tcols]
        term = x_ref[SUBLANES - j:ts - j, cols] * wk
        hs = head[SUBLANES:, :] if j == 0 else pltpu.roll(head, j, 0)[SUBLANES:, :]
        term0 = hs * wk
        acc = term if acc is None else acc + term
        acc0 = term0 if acc0 is None else acc0 + term0
    tail_ref[:, tcols] = x_ref[ts - SUBLANES:ts, cols]
    return jnp.concatenate([acc0, acc], axis=0)


def _inproj_stream(x_ref, nw_ref, wa_ref, wb_ref, rgp_ref, rgw_ref, gp_ref, gdcw_ref, p_ref,
                   hn_s, raw_z, xc_s, rg_tail, gd_tail):
    x = x_ref[...]
    r = lax.rsqrt(jnp.mean(x * x, axis=-1, keepdims=True) + NORM_EPS)
    hn_s[...] = ((x * r) * nw_ref[...]).astype(BF16)

    H = W // 2

    def proj(col, width):
        if col < WCOL_SPLIT:
            w = wa_ref[:, col:col + width]
        else:
            w = wb_ref[:, col - WCOL_SPLIT:col - WCOL_SPLIT + width]
        return jnp.dot(hn_s[...], w, preferred_element_type=F32)

    def mm(dst_ref, dst, src, scale=None):
        def run():
            y = proj(src, H)
            dst_ref[:, dst:dst + H] = y if scale is None else y * scale
        return run

    def mm_rg_gates(c):
        def run():
            g = jnp.dot(xc_s[:, c:c + H].astype(BF16), rgw_ref[c // H],
                        preferred_element_type=F32)
            p_ref[:, OFF_RG_B + c:OFF_RG_B + c + H] = g[:, :H]
            raw_z[:, c:c + H] = g[:, H:]
        return run

    def ep_rg_conv(c):
        def run():
            xc_s[:, c:c + H] = (_causal_conv(p_ref, OFF_RG_A + c, rg_tail, c, rgp_ref, H)
                                + rgp_ref[4:5, c:c + H])
        return run

    def ep_rg_ab(c):
        def run():
            gr = p_ref[:, OFF_RG_B + c:OFF_RG_B + c + H] + rgp_ref[5:6, c:c + H]
            gi = raw_z[:, c:c + H] + rgp_ref[6:7, c:c + H]
            log_a = (-RG_C * _sigmoid(gr)) * _softplus(-rgp_ref[7:8, c:c + H])
            a = jnp.exp(log_a)
            p_ref[:, OFF_RG_A + c:OFF_RG_A + c + H] = a
            p_ref[:, OFF_RG_B + c:OFF_RG_B + c + H] = (
                jnp.sqrt(-jnp.tanh(log_a) * (a * a + 1.0)) * (_sigmoid(gi) * xc_s[:, c:c + H]))
        return run

    def ep_silu(off):
        def run():
            p_ref[:, off:off + H] = _silu(p_ref[:, off:off + H])
        return run

    def ep_gd_conv(off, tcol, norm_scale):
        def run():
            y = _silu(_causal_conv(p_ref, off, gd_tail, tcol, gdcw_ref, H))
            if norm_scale is not None:
                heads = []
                for h in range(H // GD_DK):
                    yh = y[:, h * GD_DK:(h + 1) * GD_DK]
                    yh = yh * lax.rsqrt(jnp.sum(yh * yh, axis=-1, keepdims=True) + NORM_EPS)
                    heads.append(yh if norm_scale == 1.0 else yh * norm_scale)
                y = jnp.concatenate(heads, axis=1)
            p_ref[:, off:off + H] = y
        return run

    def ep_ml_gate(c):
        def run():
            p_ref[:, OFF_ML_G + c:OFF_ML_G + c + H] = (
                _sigmoid(p_ref[:, OFF_ML_G + c:OFF_ML_G + c + H]) * _silu(raw_z[:, c:c + H]))
        return run

    def mm_gates():
        p_ref[:, OFF_GATES:OFF_GATES + LANES] = proj(WCOL_GATES, LANES)

    def ep_gates():
        gates = p_ref[:, OFF_GATES:OFF_GATES + LANES] + gp_ref[0:1, :]
        lane = lax.broadcasted_iota(jnp.int32, gates.shape, 1)
        is_i = lane < LANE_ML_F
        is_f = (lane >= LANE_ML_F) & (lane < LANE_ML_F + ML_HEADS)
        is_a = (lane >= LANE_GD_A) & (lane < LANE_GD_A + GD_HEADS)
        log_f = -_softplus(-gates)
        g_gd = -jnp.exp(gp_ref[1:2, :]) * _softplus(gates)
        p_ref[:, OFF_GATES:OFF_GATES + LANES] = jnp.where(
            is_i, gates, jnp.where(is_f, log_f, jnp.where(is_a, g_gd, _sigmoid(gates))))

    units = []
    for c in range(0, W, H):
        units += [
            ([mm(p_ref, OFF_RG_A + c, WCOL_RG_X + c)], ep_rg_conv(c)),
            ([mm(p_ref, OFF_GD_Q + c, WCOL_GD_QKV + c)],
             ep_gd_conv(OFF_GD_Q + c, c, GD_DK ** -0.5)),
            ([mm(p_ref, OFF_ML_Q + c, WCOL_ML_Q + c, ML_DH ** -0.5)], None),
            ([mm_rg_gates(c)], ep_rg_ab(c)),
            ([mm(p_ref, OFF_GD_K + c, WCOL_GD_QKV + W + c)], ep_gd_conv(OFF_GD_K + c, W + c, 1.0)),
            ([mm(p_ref, OFF_ML_K + c, WCOL_ML_K + c)], None),
            ([mm(p_ref, OFF_GD_V + c, WCOL_GD_QKV + 2 * W + c)],
             ep_gd_conv(OFF_GD_V + c, 2 * W + c, None)),
            ([mm(p_ref, OFF_ML_V + c, WCOL_ML_V + c)], None),
            ([mm(p_ref, OFF_RG_G + c, WCOL_RG_Z + c)], ep_silu(OFF_RG_G + c)),
            ([mm(p_ref, OFF_ML_G + c, WCOL_ML_O + c), mm(raw_z, c, WCOL_ML_Z + c)], ep_ml_gate(c)),
            ([mm(p_ref, OFF_GD_G + c, WCOL_GD_Z + c)], ep_silu(OFF_GD_G + c)),
        ]
    units.append(([mm_gates], ep_gates))
    pending = None
    for mms, ep in units:
        for run in mms:
            run()
        if pending is not None:
            pending()
        pending = ep
        yield
    if pending is not None:
        pending()


def _rglru(p_ref, y_ref, rg_h, project):
    ts = p_ref.shape[0]
    ng = ts // SUBLANES
    a3 = p_ref[:, OFF_RG_A:OFF_RG_A + W].reshape(ng, SUBLANES, RG_WIDTH)
    b3 = p_ref[:, OFF_RG_B:OFF_RG_B + W].reshape(ng, SUBLANES, RG_WIDTH)
    sub = lax.broadcasted_iota(jnp.int32, (ng, SUBLANES, RG_WIDTH), 1)
    d = 1
    while d < SUBLANES:
        a_sh = pltpu.roll(a3, d, 1)
        b_sh = pltpu.roll(b3, d, 1)
        valid = sub >= d
        b3 = jnp.where(valid, a3 * b_sh + b3, b3)
        a3 = jnp.where(valid, a3 * a_sh, a3)
        d *= 2
        yield
    carry = rg_h[0:1, :]
    groups = []
    for g in range(ng):
        hg = b3[g] + a3[g] * carry
        groups.append(hg)
        carry = hg[SUBLANES - 1:SUBLANES, :]
    rg_h[...] = jnp.broadcast_to(carry, rg_h.shape)
    yield
    h = jnp.concatenate(groups, axis=0)
    y_ref[:, 0:RG_WIDTH] = (h * p_ref[:, OFF_RG_G:OFF_RG_G + W]).astype(y_ref.dtype)
    project(0)


def _gate_tables(p_ref, cs_ref):
    ts = p_ref.shape[0]
    gates = p_ref[:, OFF_GATES:OFF_GATES + LANES]
    lane = lax.broadcasted_iota(jnp.int32, (ts, LANES), 1)
    is_i = lane < LANE_ML_F
    is_f = (lane >= LANE_ML_F) & (lane < LANE_ML_F + ML_HEADS)
    is_a = (lane >= LANE_GD_A) & (lane < LANE_GD_A + GD_HEADS)
    x = jnp.where(is_f | is_a, gates, 0.0)
    hi, mid, lo = _split3(x)
    packed = (hi.astype(F32) + pltpu.roll(mid.astype(F32), GATE_LANES, 1)
              + pltpu.roll(lo.astype(F32), 2 * GATE_LANES, 1))
    cs = jnp.dot(cs_ref[...], packed.astype(BF16), preferred_element_type=F32)
    cs = cs + pltpu.roll(cs, LANES - GATE_LANES, 1) + pltpu.roll(cs, LANES - 2 * GATE_LANES, 1)
    csum_gd = cs[:ts]
    csum_ml = cs[ts:]
    z = jnp.where(is_i, gates, jnp.where(is_f, csum_ml, jnp.where(is_a, csum_gd, 0.0)))
    zs = jnp.concatenate(_split3(z), axis=0)
    sel = (lax.broadcasted_iota(jnp.int32, (GATE_LANES, LANES), 0)
           == lax.broadcasted_iota(jnp.int32, (GATE_LANES, LANES), 1)).astype(BF16)
    zt = lax.dot_general(sel, zs, (((1,), (1,)), ((), ())), preferred_element_type=F32)
    rows = zt[:, :ts] + zt[:, ts:2 * ts] + zt[:, 2 * ts:]
    return gates, csum_gd, csum_ml, rows


def _mlstm(p_ref, mlnw_ref, y_ref, ml_c, ml_m, gates, csum_ml, rows, project):
    ts = p_ref.shape[0]
    L = PAIR
    r_i = lax.broadcasted_iota(jnp.int32, (L, L), 0)
    c_i = lax.broadcasted_iota(jnp.int32, (L, L), 1)
    causal = r_i >= c_i
    ones_blk = jnp.ones((L, ML_DH), F32)
    units = [(h, c) for c in range(ts // L) for h in range(ML_HEADS)]

    st = {}
    for h, c in units:
        rs = slice(c * L, (c + 1) * L)
        q = p_ref[rs, OFF_ML_Q + h * ML_DH:OFF_ML_Q + (h + 1) * ML_DH]
        k = p_ref[rs, OFF_ML_K + h * ML_DH:OFF_ML_K + (h + 1) * ML_DH]
        li_c = gates[rs, LANE_ML_I + h:LANE_ML_I + h + 1]
        bq_c = csum_ml[rs, LANE_ML_F + h:LANE_ML_F + h + 1]
        r_row = (rows[LANE_ML_I + h:LANE_ML_I + h + 1, rs]
                 - rows[LANE_ML_F + h:LANE_ML_F + h + 1, rs])
        dm = jnp.where(causal, bq_c + r_row, -1e30)
        mx = jnp.max(dm, axis=-1, keepdims=True)
        g_last = bq_c[L - 1:L, :]
        ws = g_last - bq_c + li_c
        mws = jnp.max(ws, axis=0, keepdims=True)
        st[h, c] = dict(q=q, k=k, bq_c=bq_c, mx=mx, g_last=g_last, mws=mws,
                        pm=jnp.exp(dm - mx), wk=jnp.exp(ws - mws), qk=_dot_nt(q, k))
    yield
    for h, c in units:
        u = st[h, c]
        rs = slice(c * L, (c + 1) * L)
        v = p_ref[rs, OFF_ML_V + h * ML_DH:OFF_ML_V + (h + 1) * ML_DH]
        v_aug = jnp.concatenate([v, ones_blk], axis=1)
        u["intra"] = _dot(u["qk"] * u["pm"], v_aug)
        u["kv"] = _dot_tn(u["k"] * u["wk"], v_aug)

    yield
    c_aug = [ml_c[h] for h in range(ML_HEADS)]
    m_prev = [ml_m[h][0:1, 0:1] for h in range(ML_HEADS)]
    outs = [[] for _ in range(ML_HEADS)]
    for c in range(ts // L):
        qc = [_dot(st[h, c]["q"], c_aug[h]) for h in range(ML_HEADS)]
        for h in range(ML_HEADS):
            u = st[h, c]
            m_inter = u["bq_c"] + m_prev[h]
            m_t = jnp.maximum(m_inter, u["mx"])
            na = jnp.exp(u["mx"] - m_t) * u["intra"] + jnp.exp(m_inter - m_t) * qc[h]
            outs[h].append(na[:, :ML_DH] / jnp.maximum(jnp.abs(na[:, ML_DH:]), jnp.exp(-m_t)))
            m_new = jnp.maximum(u["g_last"] + m_prev[h], u["mws"])
            c_aug[h] = (jnp.exp(u["g_last"] + m_prev[h] - m_new) * c_aug[h]
                        + jnp.exp(u["mws"] - m_new) * u["kv"])
            m_prev[h] = m_new
        yield
    for h in range(ML_HEADS):
        sl = slice(h * ML_DH, (h + 1) * ML_DH)
        ml_c[h] = c_aug[h]
        ml_m[h] = jnp.broadcast_to(m_prev[h], ml_m.shape[1:])
        hh = jnp.concatenate(outs[h], axis=0)
        hn = hh * lax.rsqrt(jnp.mean(hh * hh, axis=-1, keepdims=True) + NORM_EPS)
        y_ml = (hn * mlnw_ref[0:1, sl]) * p_ref[:, OFF_ML_G + h * ML_DH:OFF_ML_G + (h + 1) * ML_DH]
        y_ref[:, RG_WIDTH + h * ML_DH:RG_WIDTH + (h + 1) * ML_DH] = y_ml.astype(y_ref.dtype)
    project(RG_WIDTH)


def _gdn(p_ref, gdnw_ref, y_ref, gd_s, gates, csum_gd, rows, project):
    ts = p_ref.shape[0]
    L = PAIR
    C = GD_CHUNK
    r_i = lax.broadcasted_iota(jnp.int32, (L, L), 0)
    c_i = lax.broadcasted_iota(jnp.int32, (L, L), 1)
    same = (r_i >= C) == (c_i >= C)
    incl = same & (r_i >= c_i)
    strict = same & (r_i > c_i)
    eye = (r_i == c_i).astype(F32)
    second = lax.broadcasted_iota(jnp.int32, (L, 1), 0) >= C
    zeros_c = jnp.zeros((C, GD_DV), F32)
    units = [(h, j) for j in range(ts // L) for h in range(GD_HEADS)]

    st = {}
    for h, j in units:
        rs = slice(j * L, (j + 1) * L)
        q = p_ref[rs, OFF_GD_Q + h * GD_DK:OFF_GD_Q + (h + 1) * GD_DK]
        k = p_ref[rs, OFF_GD_K + h * GD_DK:OFF_GD_K + (h + 1) * GD_DK]
        beta = gates[rs, LANE_GD_B + h:LANE_GD_B + h + 1]
        gc_c = csum_gd[rs, LANE_GD_A + h:LANE_GD_A + h + 1]
        gc_r = rows[LANE_GD_A + h:LANE_GD_A + h + 1, rs]
        gam = jnp.where(incl, jnp.exp(jnp.where(incl, gc_c - gc_r, 0.0)), 0.0)
        kb = k * beta
        egc = jnp.exp(gc_c)
        g_first = gc_c[C - 1:C, :]
        g_second = gc_c[L - 1:L, :]
        st[h, j] = dict(k=k, kb=kb, beta=beta, gam=gam, egc=egc, q_dec=q * egc,
                        g_last=(g_first, g_second),
                        k_dec=k * jnp.exp(jnp.where(second, g_second, g_first) - gc_c),
                        kq=_dot_nt(jnp.concatenate([kb, q], axis=0), k))
    yield
    for key in units:
        u = st[key]
        u["aqk"] = u["kq"][L:] * u["gam"]
        pw = jnp.where(strict, -(u["kq"][:L] * u["gam"]), 0.0)
        u["t_inv"] = eye + pw
        u["pw"] = _dot(pw, pw)
    yield
    span = 4
    while span < C:
        for key in units:
            u = st[key]
            both = _dot(jnp.concatenate([u["pw"], u["t_inv"]], axis=0), u["pw"])
            u["pw"] = both[:L]
            u["t_inv"] = u["t_inv"] + both[L:]
        span *= 2
        yield
    for key in units:
        u = st[key]
        u["t_inv"] = u["t_inv"] + _dot(u["t_inv"], u["pw"])
    yield
    for h, j in units:
        u = st[h, j]
        rs = slice(j * L, (j + 1) * L)
        v = p_ref[rs, OFF_GD_V + h * GD_DV:OFF_GD_V + (h + 1) * GD_DV]
        uw = _dot(u["t_inv"], jnp.concatenate([v * u["beta"], u["kb"] * u["egc"]], axis=1))
        u["u"] = uw[:, :GD_DV]
        u["w"] = uw[:, GD_DV:]

    yield
    state = [gd_s[h] for h in range(GD_HEADS)]
    outs = [[] for _ in range(GD_HEADS)]
    for j in range(ts // L):
        for half in range(2):
            hs = slice(half * C, (half + 1) * C)
            wq = [_dot(jnp.concatenate([st[h, j]["w"][hs], st[h, j]["q_dec"][hs]], axis=0), state[h])
                  for h in range(GD_HEADS)]
            v_new = [st[h, j]["u"][hs] - wq[h][:C] for h in range(GD_HEADS)]
            for h in range(GD_HEADS):
                v_pad = jnp.concatenate([v_new[h], zeros_c] if half == 0 else [zeros_c, v_new[h]],
                                        axis=0)
                outs[h].append(wq[h][C:] + _dot(st[h, j]["aqk"][hs], v_pad))
            for h in range(GD_HEADS):
                u = st[h, j]
                state[h] = (state[h] * jnp.exp(u["g_last"][half])
                            + _dot_tn(u["k_dec"][hs], v_new[h]))
            yield
    for h in range(GD_HEADS):
        gd_s[h] = state[h]
        o = jnp.concatenate(outs[h], axis=0)
        on = o * lax.rsqrt(jnp.mean(o * o, axis=-1, keepdims=True) + NORM_EPS)
        y_gd = on * gdnw_ref[0:1, :] * p_ref[:, OFF_GD_G + h * GD_DV:OFF_GD_G + (h + 1) * GD_DV]
        y_ref[:, RG_WIDTH + ML_WIDTH + h * GD_DV:RG_WIDTH + ML_WIDTH + (h + 1) * GD_DV] = (
            y_gd.astype(y_ref.dtype))
    project(RG_WIDTH + ML_WIDTH)


def _interleave(order, stages):
    live = dict(stages)

    def step(key):
        if key in live:
            try:
                next(live[key])
            except StopIteration:
                del live[key]

    for key in order:
        step(key)
    while live:
        for key in list(live):
            step(key)


def _layer_kernel(xn_ref, xc_ref, nw_ref, wa_ref, wb_ref, rgp_ref, rgw_ref, gp_ref, gdcw_ref,
                  wo_ref, fw_ref, mlnw_ref, gdnw_ref, cs_ref, o_ref,
                  p_s, hn_s, raw_z, xc_s, rg_tail, gd_tail, y_ref, rg_h, ml_c, ml_m, gd_s,
                  *, blocks_per_seq, final_norm):
    i = pl.program_id(0)
    slot = lax.rem(i, 2)
    p_new = p_s.at[slot]
    p_cur = p_s.at[1 - slot]

    @pl.when(i == 0)
    def _():
        p_s[1] = jnp.zeros(p_s.shape[1:], p_s.dtype)

    @pl.when(lax.rem(i, blocks_per_seq) == 0)
    def _():
        rg_tail[...] = jnp.zeros_like(rg_tail)
        gd_tail[...] = jnp.zeros_like(gd_tail)

    @pl.when(lax.rem(jnp.maximum(i - 1, 0), blocks_per_seq) == 0)
    def _():
        rg_h[...] = jnp.zeros_like(rg_h)
        ml_c[...] = jnp.zeros_like(ml_c)
        ml_m[...] = jnp.zeros_like(ml_m)
        gd_s[...] = jnp.zeros_like(gd_s)

    o_ref[...] = xc_ref[...]

    def project(k0):
        o_ref[...] += jnp.dot(y_ref[:, k0:k0 + W], wo_ref[k0:k0 + W, :],
                              preferred_element_type=F32)

    gates, csum_gd, csum_ml, rows = _gate_tables(p_cur, cs_ref)
    _interleave(LAYER_ORDER, dict(
        P=_inproj_stream(xn_ref, nw_ref, wa_ref, wb_ref, rgp_ref, rgw_ref, gp_ref, gdcw_ref,
                         p_new, hn_s, raw_z, xc_s, rg_tail, gd_tail),
        G=_gdn(p_cur, gdnw_ref, y_ref, gd_s, gates, csum_gd, rows, project),
        M=_mlstm(p_cur, mlnw_ref, y_ref, ml_c, ml_m, gates, csum_ml, rows, project),
        R=_rglru(p_cur, y_ref, rg_h, project)))
    if final_norm:
        xn = o_ref[...]
        r = lax.rsqrt(jnp.mean(xn * xn, axis=-1, keepdims=True) + NORM_EPS)
        o_ref[...] = (xn * r) * fw_ref[...]


def _cumsum_matrix(ts):
    r = jnp.arange(ts)[:, None]
    c = jnp.arange(ts)[None, :]
    m64 = (r >= c) & (r // GD_CHUNK == c // GD_CHUNK)
    m128 = (r >= c) & (r // PAIR == c // PAIR)
    return jnp.concatenate([m64, m128], axis=0).astype(BF16)


def _layer(x2, norm_w, w_a, w_b, rgp, rgw, gp, gdcw, w_out, final_w, mlnw, gdnw, cs, layer, seq,
           final_norm):
    t = x2.shape[0]
    ts = TS_BLOCK
    nb = t // ts
    per_layer = lambda a, **kw: pl.BlockSpec((None,) + a.shape[1:],
                                             lambda i: (layer,) + (0,) * (a.ndim - 1), **kw)
    once = dict(pipeline_mode=pl.Buffered(1))
    return pl.pallas_call(
        functools.partial(_layer_kernel, blocks_per_seq=seq // ts, final_norm=final_norm),
        grid=(nb + 1,),
        in_specs=[pl.BlockSpec((ts, D_MODEL), lambda i: (jnp.minimum(i, nb - 1), 0)),
                  pl.BlockSpec((ts, D_MODEL), lambda i: (jnp.maximum(i - 1, 0), 0)),
                  per_layer(norm_w),
                  pl.BlockSpec((None, D_MODEL, WCOL_SPLIT), lambda i: (layer, 0, 0), **once),
                  per_layer(w_b, **once), per_layer(rgp), per_layer(rgw), per_layer(gp),
                  per_layer(gdcw), per_layer(w_out, **once),
                  pl.BlockSpec(final_w.shape, lambda i: (0, 0)),
                  per_layer(mlnw), per_layer(gdnw), pl.BlockSpec(cs.shape, lambda i: (0, 0))],
        out_specs=pl.BlockSpec((ts, D_MODEL), lambda i: (jnp.maximum(i - 1, 0), 0)),
        out_shape=jax.ShapeDtypeStruct((t, D_MODEL), F32),
        scratch_shapes=[pltpu.VMEM((2, ts, D_PROJ), F32),
                        pltpu.VMEM((ts, D_MODEL), BF16),
                        pltpu.VMEM((ts, W), F32),
                        pltpu.VMEM((ts, RG_WIDTH), F32),
                        pltpu.VMEM((SUBLANES, RG_WIDTH), F32),
                        pltpu.VMEM((SUBLANES, GD_CONV), F32),
                        pltpu.VMEM((ts, D_MIX), BF16),
                        pltpu.VMEM((SUBLANES, RG_WIDTH), F32),
                        pltpu.VMEM((ML_HEADS, ML_DH, 2 * ML_DH), F32),
                        pltpu.VMEM((ML_HEADS, SUBLANES, LANES), F32),
                        pltpu.VMEM((GD_HEADS, GD_DK, GD_DV), F32)],
        compiler_params=pltpu.CompilerParams(dimension_semantics=("arbitrary",),
                                             vmem_limit_bytes=VMEM_LIMIT),
        name="layer",
    )(x2, x2, norm_w, w_a, w_b, rgp, rgw, gp, gdcw, w_out, final_w, mlnw, gdnw, cs)


def _cast_kernel(w_ref, o_ref):
    o_ref[...] = w_ref[...].astype(o_ref.dtype)


def _w_in_head(w_in):
    depth, d, _ = w_in.shape
    tr = TS_BLOCK
    return pl.pallas_call(
        _cast_kernel,
        grid=(depth, d // tr),
        in_specs=[pl.BlockSpec((None, tr, WCOL_SPLIT), lambda l, i: (l, i, 0))],
        out_specs=pl.BlockSpec((None, tr, WCOL_SPLIT), lambda l, i: (l, i, 0)),
        out_shape=jax.ShapeDtypeStruct((depth, d, WCOL_SPLIT), BF16),
        compiler_params=pltpu.CompilerParams(dimension_semantics=("parallel", "parallel")),
        name="cast_w_in",
    )(w_in)


def _w_in_tail(w_in):
    pad = jnp.zeros(w_in.shape[:-1] + (LANES - GATE_LANES,), w_in.dtype)
    return jnp.concatenate([
        w_in[..., SRC_GD:SRC_GD_GATES], w_in[..., SRC_ML_GATES:SRC_GD],
        w_in[..., SRC_GD_GATES:], pad], axis=-1)


def _rg_gate_blockdiag(gate_w):
    depth = gate_w.shape[0]
    nb = RG_BLOCKS // 2
    eye = jnp.eye(nb, dtype=gate_w.dtype)
    halves = []
    for hf in range(2):
        blk = gate_w[:, :, hf * nb:(hf + 1) * nb]
        bd = jnp.einsum('lgncd,nm->lgncmd', blk, eye)
        bd = bd.reshape(depth, 2, nb * RG_BLOCK, nb * RG_BLOCK)
        halves.append(jnp.concatenate([bd[:, 0], bd[:, 1]], axis=-1))
    return jnp.stack(halves, axis=1)


def _gate_params(ml_gate_b, gd_dt_bias, gd_a_log):
    depth = ml_gate_b.shape[0]
    zeros = lambda n: jnp.zeros((depth, n), F32)
    row0 = jnp.concatenate([ml_gate_b[:, 0], ml_gate_b[:, 1], gd_dt_bias,
                            zeros(LANES - LANE_GD_B)], axis=-1)
    row1 = jnp.concatenate([zeros(LANE_GD_A), gd_a_log, zeros(LANES - LANE_GD_B)], axis=-1)
    rest = jnp.zeros((depth, SUBLANES - 2, LANES), F32)
    return jnp.concatenate([row0[:, None], row1[:, None], rest], axis=1)


def kernel(x, norm_w, w_in, rg_conv_w, rg_conv_b, rg_gate_w, rg_gate_b, rg_lambda, ml_gate_b,
           ml_norm_w, gd_conv_w, gd_a_log, gd_dt_bias, gd_norm_w, w_out, final_norm_w):
    bsz, seq, _ = x.shape
    depth = w_in.shape[0]
    t = bsz * seq
    w_a = _w_in_head(w_in)
    w_b = _w_in_tail(w_in).astype(BF16)
    w_o = w_out.astype(BF16)
    rgp = jnp.concatenate([rg_conv_w, rg_conv_b[:, None], rg_gate_b, rg_lambda[:, None]],
                          axis=1)
    rgw = _rg_gate_blockdiag(rg_gate_w).astype(BF16)
    gp = _gate_params(ml_gate_b, gd_dt_bias, gd_a_log)
    cs = _cumsum_matrix(TS_BLOCK)

    x2 = x.reshape(t, D_MODEL)
    for l in range(depth):
        x2 = _layer(x2, norm_w[:, None], w_a, w_b, rgp, rgw, gp, gd_conv_w, w_o, final_norm_w[None],
                    ml_norm_w[:, None], gd_norm_w[:, None], cs, l, seq, l == depth - 1)
    return x2.reshape(bsz, seq, D_MODEL)
```

```python
import functools

import jax
import jax.numpy as jnp
from jax import lax
from jax.experimental import pallas as pl
from jax.experimental.pallas import tpu as pltpu

F32 = jnp.float32
BF16 = jnp.bfloat16

D_MODEL = 1024
CONV_K = 4
NORM_EPS = 1e-6
RG_WIDTH = 512
RG_BLOCKS = 8
RG_BLOCK = RG_WIDTH // RG_BLOCKS
RG_C = 8.0
ML_HEADS = 4
ML_DH = 128
ML_WIDTH = ML_HEADS * ML_DH
GD_HEADS = 4
GD_DK = 128
GD_DV = 128
GD_QK = GD_HEADS * GD_DK
GD_WIDTH = GD_HEADS * GD_DV
GD_CONV = 2 * GD_QK + GD_WIDTH
D_MIX = RG_WIDTH + ML_WIDTH + GD_WIDTH
GD_CHUNK = 64
PAIR = 2 * GD_CHUNK
SUBLANES = 8
LANES = 128

W = 512
WCOL_RG_X, WCOL_RG_Z = 0 * W, 1 * W
WCOL_ML_Q, WCOL_ML_K, WCOL_ML_V, WCOL_ML_O, WCOL_ML_Z = 2 * W, 3 * W, 4 * W, 5 * W, 6 * W
WCOL_GD_QKV, WCOL_GD_Z = 7 * W, 10 * W
WCOL_GATES = 11 * W
WCOL_SPLIT = WCOL_GD_QKV
D_PROJ = WCOL_GATES + LANES
OFF_RG_A, OFF_RG_B, OFF_RG_G = 0 * W, 1 * W, 2 * W
OFF_ML_Q, OFF_ML_K, OFF_ML_V, OFF_ML_G = 3 * W, 4 * W, 5 * W, 6 * W
OFF_GD_Q, OFF_GD_K, OFF_GD_V, OFF_GD_G = 7 * W, 8 * W, 9 * W, 10 * W
OFF_GATES = 11 * W
LANE_ML_I, LANE_ML_F, LANE_GD_A, LANE_GD_B = 0, 4, 8, 12
GATE_LANES = 16
RAW_RG, RAW_GD = 0, RG_WIDTH // LANES
RAW_TILES = RAW_GD + GD_CONV // LANES
SRC_ML_GATES = 2 * RG_WIDTH + 5 * ML_WIDTH
SRC_GD = SRC_ML_GATES + 2 * ML_HEADS
SRC_GD_GATES = SRC_GD + 2 * GD_QK + 2 * GD_WIDTH

VMEM_LIMIT = 56 * 1024 * 1024
TS_BLOCK = 256
LAYER_ORDER = "GPP" * 9 + "MRPG" * 5


def _dot(a, b):
    return jnp.dot(a.astype(BF16), b.astype(BF16), preferred_element_type=F32)


def _dot_nt(a, b):
    return lax.dot_general(a.astype(BF16), b.astype(BF16), (((1,), (1,)), ((), ())),
                           preferred_element_type=F32)


def _dot_tn(a, b):
    return lax.dot_general(a.astype(BF16), b.astype(BF16), (((0,), (0,)), ((), ())),
                           preferred_element_type=F32)


def _split3(x):
    hi = x.astype(BF16)
    r1 = x - hi.astype(F32)
    mid = r1.astype(BF16)
    lo = (r1 - mid.astype(F32)).astype(BF16)
    return hi, mid, lo


def _softplus(x):
    return jnp.maximum(x, 0.0) + jnp.log1p(jnp.exp(-jnp.abs(x)))


def _sigmoid(x):
    return 1.0 / (1.0 + jnp.exp(-x))


def _silu(x):
    return x * _sigmoid(x)


def _causal_conv(x_ref, tile, tail_ref, tcol0, cw_ref, out_ref, out_tile, post):
    ts = x_ref.shape[1]
    n = ts // SUBLANES
    tcols = slice(tcol0, tcol0 + LANES)
    tail = tail_ref[:, tcols]
    new_tail = x_ref[tile, ts - SUBLANES:ts, :]
    cls = [x_ref[tile, pl.ds(r, n, stride=SUBLANES), :] for r in range(SUBLANES)]
    first = lax.broadcasted_iota(jnp.int32, (n, LANES), 0) == 0
    prev = {r: jnp.where(first, tail[r:r + 1, :], pltpu.roll(cls[r], 1, 0))
            for r in range(SUBLANES - CONV_K + 1, SUBLANES)}
    for r in range(SUBLANES):
        acc = None
        for k in range(CONV_K):
            j = CONV_K - 1 - k
            tap = cls[r - j] if r >= j else prev[r - j + SUBLANES]
            term = tap * cw_ref[k:k + 1, tcols]
            acc = term if acc is None else acc + term
        out_ref[out_tile, pl.ds(r, n, stride=SUBLANES), :] = post(acc)
    tail_ref[:, tcols] = new_tail


def _inproj_stream(x_ref, nw_ref, wa_ref, wb_ref, rgp_ref, rgw_ref, gp_ref, gdcw_ref, p_ref,
                   gdo_ref, hn_s, raw_s, raw_z, xc_s, rg_tail, gd_tail):
    x = x_ref[...]
    r = lax.rsqrt(jnp.mean(x * x, axis=-1, keepdims=True) + NORM_EPS)
    hn_s[...] = ((x * r) * nw_ref[...]).astype(BF16)

    H = W // 2

    def proj(col, width):
        if col < WCOL_SPLIT:
            w = wa_ref[:, col:col + width]
        else:
            w = wb_ref[:, col - WCOL_SPLIT:col - WCOL_SPLIT + width]
        return jnp.dot(hn_s[...], w, preferred_element_type=F32)

    def mm(dst_ref, dst, src, scale=None):
        def run():
            y = proj(src, H)
            dst_ref[:, dst:dst + H] = y if scale is None else y * scale
        return run

    def mm_raw(tile0, src):
        def run():
            y = proj(src, H)
            for i in range(H // LANES):
                raw_s[tile0 + i] = y[:, i * LANES:(i + 1) * LANES]
        return run

    def xc_half(c):
        return jnp.concatenate([xc_s[c // LANES + i] for i in range(H // LANES)], axis=1)

    def mm_rg_gates(c):
        def run():
            g = jnp.dot(xc_half(c).astype(BF16), rgw_ref[c // H],
                        preferred_element_type=F32)
            p_ref[:, OFF_RG_B + c:OFF_RG_B + c + H] = g[:, :H]
            raw_z[:, c:c + H] = g[:, H:]
        return run

    def ep_rg_conv(c):
        def run():
            for t in range(c // LANES, (c + H) // LANES):
                bias = rgp_ref[4:5, t * LANES:(t + 1) * LANES]
                _causal_conv(raw_s, RAW_RG + t, rg_tail, t * LANES, rgp_ref, xc_s, t,
                             lambda y, bias=bias: y + bias)
        return run

    def ep_rg_ab(c):
        def run():
            gr = p_ref[:, OFF_RG_B + c:OFF_RG_B + c + H] + rgp_ref[5:6, c:c + H]
            gi = raw_z[:, c:c + H] + rgp_ref[6:7, c:c + H]
            log_a = (-RG_C * _sigmoid(gr)) * _softplus(-rgp_ref[7:8, c:c + H])
            a = jnp.exp(log_a)
            p_ref[:, OFF_RG_A + c:OFF_RG_A + c + H] = a
            p_ref[:, OFF_RG_B + c:OFF_RG_B + c + H] = (
                jnp.sqrt(-jnp.tanh(log_a) * (a * a + 1.0)) * (_sigmoid(gi) * xc_half(c)))
        return run

    def ep_silu(off):
        def run():
            p_ref[:, off:off + H] = _silu(p_ref[:, off:off + H])
        return run

    def ep_gd_conv(tcol, norm_scale):
        def post(y):
            y = _silu(y)
            if norm_scale is None:
                return y
            y = y * lax.rsqrt(jnp.sum(y * y, axis=-1, keepdims=True) + NORM_EPS)
            return y if norm_scale == 1.0 else y * norm_scale

        def run():
            for t in range(tcol // LANES, (tcol + H) // LANES):
                _causal_conv(raw_s, RAW_GD + t, gd_tail, t * LANES, gdcw_ref, gdo_ref, t, post)
        return run

    def ep_ml_gate(c):
        def run():
            p_ref[:, OFF_ML_G + c:OFF_ML_G + c + H] = (
                _sigmoid(p_ref[:, OFF_ML_G + c:OFF_ML_G + c + H]) * _silu(raw_z[:, c:c + H]))
        return run

    def mm_gates():
        p_ref[:, OFF_GATES:OFF_GATES + LANES] = proj(WCOL_GATES, LANES)

    def ep_gates():
        gates = p_ref[:, OFF_GATES:OFF_GATES + LANES] + gp_ref[0:1, :]
        lane = lax.broadcasted_iota(jnp.int32, gates.shape, 1)
        is_i = lane < LANE_ML_F
        is_f = (lane >= LANE_ML_F) & (lane < LANE_ML_F + ML_HEADS)
        is_a = (lane >= LANE_GD_A) & (lane < LANE_GD_A + GD_HEADS)
        log_f = -_softplus(-gates)
        g_gd = -jnp.exp(gp_ref[1:2, :]) * _softplus(gates)
        p_ref[:, OFF_GATES:OFF_GATES + LANES] = jnp.where(
            is_i, gates, jnp.where(is_f, log_f, jnp.where(is_a, g_gd, _sigmoid(gates))))

    units = []
    for c in range(0, W, H):
        units += [
            ([mm_raw(RAW_RG + c // LANES, WCOL_RG_X + c)], ep_rg_conv(c)),
            ([mm_raw(RAW_GD + c // LANES, WCOL_GD_QKV + c)], ep_gd_conv(c, GD_DK ** -0.5)),
            ([mm(p_ref, OFF_ML_Q + c, WCOL_ML_Q + c, ML_DH ** -0.5)], None),
            ([mm_rg_gates(c)], ep_rg_ab(c)),
            ([mm_raw(RAW_GD + (W + c) // LANES, WCOL_GD_QKV + W + c)], ep_gd_conv(W + c, 1.0)),
            ([mm(p_ref, OFF_ML_K + c, WCOL_ML_K + c)], None),
            ([mm_raw(RAW_GD + (2 * W + c) // LANES, WCOL_GD_QKV + 2 * W + c)],
             ep_gd_conv(2 * W + c, None)),
            ([mm(p_ref, OFF_ML_V + c, WCOL_ML_V + c)], None),
            ([mm(p_ref, OFF_RG_G + c, WCOL_RG_Z + c)], ep_silu(OFF_RG_G + c)),
            ([mm(p_ref, OFF_ML_G + c, WCOL_ML_O + c), mm(raw_z, c, WCOL_ML_Z + c)], ep_ml_gate(c)),
            ([mm(p_ref, OFF_GD_G + c, WCOL_GD_Z + c)], ep_silu(OFF_GD_G + c)),
        ]
    units.append(([mm_gates], ep_gates))
    pending = None
    for mms, ep in units:
        for run in mms:
            run()
        if pending is not None:
            pending()
        pending = ep
        yield
    if pending is not None:
        pending()


def _rglru(p_ref, y_ref, rg_h, project):
    ts = p_ref.shape[0]
    ng = ts // SUBLANES
    a3 = p_ref[:, OFF_RG_A:OFF_RG_A + W].reshape(ng, SUBLANES, RG_WIDTH)
    b3 = p_ref[:, OFF_RG_B:OFF_RG_B + W].reshape(ng, SUBLANES, RG_WIDTH)
    sub = lax.broadcasted_iota(jnp.int32, (ng, SUBLANES, RG_WIDTH), 1)
    d = 1
    while d < SUBLANES:
        a_sh = pltpu.roll(a3, d, 1)
        b_sh = pltpu.roll(b3, d, 1)
        valid = sub >= d
        b3 = jnp.where(valid, a3 * b_sh + b3, b3)
        a3 = jnp.where(valid, a3 * a_sh, a3)
        d *= 2
        yield
    carry = rg_h[0:1, :]
    groups = []
    for g in range(ng):
        hg = b3[g] + a3[g] * carry
        groups.append(hg)
        carry = hg[SUBLANES - 1:SUBLANES, :]
    rg_h[...] = jnp.broadcast_to(carry, rg_h.shape)
    yield
    h = jnp.concatenate(groups, axis=0)
    y_ref[:, 0:RG_WIDTH] = (h * p_ref[:, OFF_RG_G:OFF_RG_G + W]).astype(y_ref.dtype)
    project(0)


def _gate_tables(p_ref, cs_ref):
    ts = p_ref.shape[0]
    gates = p_ref[:, OFF_GATES:OFF_GATES + LANES]
    lane = lax.broadcasted_iota(jnp.int32, (ts, LANES), 1)
    is_i = lane < LANE_ML_F
    is_f = (lane >= LANE_ML_F) & (lane < LANE_ML_F + ML_HEADS)
    is_a = (lane >= LANE_GD_A) & (lane < LANE_GD_A + GD_HEADS)
    x = jnp.where(is_f | is_a, gates, 0.0)
    hi, mid, lo = _split3(x)
    packed = (hi.astype(F32) + pltpu.roll(mid.astype(F32), GATE_LANES, 1)
              + pltpu.roll(lo.astype(F32), 2 * GATE_LANES, 1))
    cs = jnp.dot(cs_ref[...], packed.astype(BF16), preferred_element_type=F32)
    cs = cs + pltpu.roll(cs, LANES - GATE_LANES, 1) + pltpu.roll(cs, LANES - 2 * GATE_LANES, 1)
    csum_gd = cs[:ts]
    csum_ml = cs[ts:]
    z = jnp.where(is_i, gates, jnp.where(is_f, csum_ml, jnp.where(is_a, csum_gd, 0.0)))
    zs = jnp.concatenate(_split3(z), axis=0)
    sel = (lax.broadcasted_iota(jnp.int32, (GATE_LANES, LANES), 0)
           == lax.broadcasted_iota(jnp.int32, (GATE_LANES, LANES), 1)).astype(BF16)
    zt = lax.dot_general(sel, zs, (((1,), (1,)), ((), ())), preferred_element_type=F32)
    rows = zt[:, :ts] + zt[:, ts:2 * ts] + zt[:, 2 * ts:]
    return gates, csum_gd, csum_ml, rows


def _mlstm(p_ref, mlnw_ref, y_ref, ml_c, ml_m, gates, csum_ml, rows, project):
    ts = p_ref.shape[0]
    L = PAIR
    r_i = lax.broadcasted_iota(jnp.int32, (L, L), 0)
    c_i = lax.broadcasted_iota(jnp.int32, (L, L), 1)
    causal = r_i >= c_i
    ones_blk = jnp.ones((L, ML_DH), F32)
    units = [(h, c) for c in range(ts // L) for h in range(ML_HEADS)]

    st = {}
    for h, c in units:
        rs = slice(c * L, (c + 1) * L)
        q = p_ref[rs, OFF_ML_Q + h * ML_DH:OFF_ML_Q + (h + 1) * ML_DH]
        k = p_ref[rs, OFF_ML_K + h * ML_DH:OFF_ML_K + (h + 1) * ML_DH]
        li_c = gates[rs, LANE_ML_I + h:LANE_ML_I + h + 1]
        bq_c = csum_ml[rs, LANE_ML_F + h:LANE_ML_F + h + 1]
        r_row = (rows[LANE_ML_I + h:LANE_ML_I + h + 1, rs]
                 - rows[LANE_ML_F + h:LANE_ML_F + h + 1, rs])
        dm = jnp.where(causal, bq_c + r_row, -1e30)
        mx = jnp.max(dm, axis=-1, keepdims=True)
        g_last = bq_c[L - 1:L, :]
        ws = g_last - bq_c + li_c
        mws = jnp.max(ws, axis=0, keepdims=True)
        st[h, c] = dict(q=q, k=k, bq_c=bq_c, mx=mx, g_last=g_last, mws=mws,
                        pm=jnp.exp(dm - mx), wk=jnp.exp(ws - mws), qk=_dot_nt(q, k))
    yield
    for h, c in units:
        u = st[h, c]
        rs = slice(c * L, (c + 1) * L)
        v = p_ref[rs, OFF_ML_V + h * ML_DH:OFF_ML_V + (h + 1) * ML_DH]
        v_aug = jnp.concatenate([v, ones_blk], axis=1)
        u["intra"] = _dot(u["qk"] * u["pm"], v_aug)
        u["kv"] = _dot_tn(u["k"] * u["wk"], v_aug)

    yield
    c_aug = [ml_c[h] for h in range(ML_HEADS)]
    m_prev = [ml_m[h][0:1, 0:1] for h in range(ML_HEADS)]
    outs = [[] for _ in range(ML_HEADS)]
    for c in range(ts // L):
        qc = [_dot(st[h, c]["q"], c_aug[h]) for h in range(ML_HEADS)]
        for h in range(ML_HEADS):
            u = st[h, c]
            m_inter = u["bq_c"] + m_prev[h]
            m_t = jnp.maximum(m_inter, u["mx"])
            na = jnp.exp(u["mx"] - m_t) * u["intra"] + jnp.exp(m_inter - m_t) * qc[h]
            outs[h].append(na[:, :ML_DH] / jnp.maximum(jnp.abs(na[:, ML_DH:]), jnp.exp(-m_t)))
            m_new = jnp.maximum(u["g_last"] + m_prev[h], u["mws"])
            c_aug[h] = (jnp.exp(u["g_last"] + m_prev[h] - m_new) * c_aug[h]
                        + jnp.exp(u["mws"] - m_new) * u["kv"])
            m_prev[h] = m_new
        yield
    for h in range(ML_HEADS):
        sl = slice(h * ML_DH, (h + 1) * ML_DH)
        ml_c[h] = c_aug[h]
        ml_m[h] = jnp.broadcast_to(m_prev[h], ml_m.shape[1:])
        hh = jnp.concatenate(outs[h], axis=0)
        hn = hh * lax.rsqrt(jnp.mean(hh * hh, axis=-1, keepdims=True) + NORM_EPS)
        y_ml = (hn * mlnw_ref[0:1, sl]) * p_ref[:, OFF_ML_G + h * ML_DH:OFF_ML_G + (h + 1) * ML_DH]
        y_ref[:, RG_WIDTH + h * ML_DH:RG_WIDTH + (h + 1) * ML_DH] = y_ml.astype(y_ref.dtype)
    project(RG_WIDTH)


def _gdn(p_ref, gdo_ref, gdnw_ref, y_ref, gd_s, gates, csum_gd, rows, project):
    ts = p_ref.shape[0]
    L = PAIR
    C = GD_CHUNK
    r_i = lax.broadcasted_iota(jnp.int32, (L, L), 0)
    c_i = lax.broadcasted_iota(jnp.int32, (L, L), 1)
    same = (r_i >= C) == (c_i >= C)
    incl = same & (r_i >= c_i)
    strict = same & (r_i > c_i)
    eye = (r_i == c_i).astype(F32)
    second = lax.broadcasted_iota(jnp.int32, (L, 1), 0) >= C
    zeros_c = jnp.zeros((C, GD_DV), F32)
    units = [(h, j) for j in range(ts // L) for h in range(GD_HEADS)]

    st = {}
    for h, j in units:
        rs = slice(j * L, (j + 1) * L)
        q = gdo_ref[h, rs, :]
        k = gdo_ref[GD_HEADS + h, rs, :]
        beta = gates[rs, LANE_GD_B + h:LANE_GD_B + h + 1]
        gc_c = csum_gd[rs, LANE_GD_A + h:LANE_GD_A + h + 1]
        gc_r = rows[LANE_GD_A + h:LANE_GD_A + h + 1, rs]
        gam = jnp.where(incl, jnp.exp(jnp.where(incl, gc_c - gc_r, 0.0)), 0.0)
        kb = k * beta
        egc = jnp.exp(gc_c)
        g_first = gc_c[C - 1:C, :]
        g_second = gc_c[L - 1:L, :]
        st[h, j] = dict(k=k, kb=kb, beta=beta, gam=gam, egc=egc, q_dec=q * egc,
                        g_last=(g_first, g_second),
                        k_dec=k * jnp.exp(jnp.where(second, g_second, g_first) - gc_c),
                        kq=_dot_nt(jnp.concatenate([kb, q], axis=0), k))
    yield
    for key in units:
        u = st[key]
        u["aqk"] = u["kq"][L:] * u["gam"]
        pw = jnp.where(strict, -(u["kq"][:L] * u["gam"]), 0.0)
        u["t_inv"] = eye + pw
        u["pw"] = _dot(pw, pw)
    yield
    span = 4
    while span < C:
        for key in units:
            u = st[key]
            both = _dot(jnp.concatenate([u["pw"], u["t_inv"]], axis=0), u["pw"])
            u["pw"] = both[:L]
            u["t_inv"] = u["t_inv"] + both[L:]
        span *= 2
        yield
    for key in units:
        u = st[key]
        u["t_inv"] = u["t_inv"] + _dot(u["t_inv"], u["pw"])
    yield
    for h, j in units:
        u = st[h, j]
        rs = slice(j * L, (j + 1) * L)
        v = gdo_ref[2 * GD_HEADS + h, rs, :]
        uw = _dot(u["t_inv"], jnp.concatenate([v * u["beta"], u["kb"] * u["egc"]], axis=1))
        u["u"] = uw[:, :GD_DV]
        u["w"] = uw[:, GD_DV:]

    yield
    state = [gd_s[h] for h in range(GD_HEADS)]
    outs = [[] for _ in range(GD_HEADS)]
    for j in range(ts // L):
        for half in range(2):
            hs = slice(half * C, (half + 1) * C)
            wq = [_dot(jnp.concatenate([st[h, j]["w"][hs], st[h, j]["q_dec"][hs]], axis=0), state[h])
                  for h in range(GD_HEADS)]
            v_new = [st[h, j]["u"][hs] - wq[h][:C] for h in range(GD_HEADS)]
            for h in range(GD_HEADS):
                v_pad = jnp.concatenate([v_new[h], zeros_c] if half == 0 else [zeros_c, v_new[h]],
                                        axis=0)
                outs[h].append(wq[h][C:] + _dot(st[h, j]["aqk"][hs], v_pad))
            for h in range(GD_HEADS):
                u = st[h, j]
                state[h] = (state[h] * jnp.exp(u["g_last"][half])
                            + _dot_tn(u["k_dec"][hs], v_new[h]))
            yield
    for h in range(GD_HEADS):
        gd_s[h] = state[h]
        o = jnp.concatenate(outs[h], axis=0)
        on = o * lax.rsqrt(jnp.mean(o * o, axis=-1, keepdims=True) + NORM_EPS)
        y_gd = on * gdnw_ref[0:1, :] * p_ref[:, OFF_GD_G + h * GD_DV:OFF_GD_G + (h + 1) * GD_DV]
        y_ref[:, RG_WIDTH + ML_WIDTH + h * GD_DV:RG_WIDTH + ML_WIDTH + (h + 1) * GD_DV] = (
            y_gd.astype(y_ref.dtype))
    project(RG_WIDTH + ML_WIDTH)


def _interleave(order, stages):
    live = dict(stages)

    def step(key):
        if key in live:
            try:
                next(live[key])
            except StopIteration:
                del live[key]

    for key in order:
        step(key)
    while live:
        for key in list(live):
            step(key)


def _layer_kernel(xn_ref, xc_ref, nw_ref, wa_ref, wb_ref, rgp_ref, rgw_ref, gp_ref, gdcw_ref,
                  wo_ref, fw_ref, mlnw_ref, gdnw_ref, cs_ref, o_ref,
                  p_s, gdo_s, hn_s, raw_s, raw_z, xc_s, rg_tail, gd_tail,
                  y_ref, rg_h, ml_c, ml_m, gd_s,
                  *, blocks_per_seq, final_norm):
    i = pl.program_id(0)
    slot = lax.rem(i, 2)
    p_new = p_s.at[slot]
    p_cur = p_s.at[1 - slot]

    @pl.when(i == 0)
    def _():
        p_s[1] = jnp.zeros(p_s.shape[1:], p_s.dtype)
        gdo_s[1] = jnp.zeros(gdo_s.shape[1:], gdo_s.dtype)

    @pl.when(lax.rem(i, blocks_per_seq) == 0)
    def _():
        rg_tail[...] = jnp.zeros_like(rg_tail)
        gd_tail[...] = jnp.zeros_like(gd_tail)

    @pl.when(lax.rem(jnp.maximum(i - 1, 0), blocks_per_seq) == 0)
    def _():
        rg_h[...] = jnp.zeros_like(rg_h)
        ml_c[...] = jnp.zeros_like(ml_c)
        ml_m[...] = jnp.zeros_like(ml_m)
        gd_s[...] = jnp.zeros_like(gd_s)

    o_ref[...] = xc_ref[...]

    def project(k0):
        o_ref[...] += jnp.dot(y_ref[:, k0:k0 + W], wo_ref[k0:k0 + W, :],
                              preferred_element_type=F32)

    gates, csum_gd, csum_ml, rows = _gate_tables(p_cur, cs_ref)
    _interleave(LAYER_ORDER, dict(
        P=_inproj_stream(xn_ref, nw_ref, wa_ref, wb_ref, rgp_ref, rgw_ref, gp_ref, gdcw_ref,
                         p_new, gdo_s.at[slot], hn_s, raw_s, raw_z, xc_s, rg_tail, gd_tail),
        G=_gdn(p_cur, gdo_s.at[1 - slot], gdnw_ref, y_ref, gd_s, gates, csum_gd, rows, project),
        M=_mlstm(p_cur, mlnw_ref, y_ref, ml_c, ml_m, gates, csum_ml, rows, project),
        R=_rglru(p_cur, y_ref, rg_h, project)))
    if final_norm:
        xn = o_ref[...]
        r = lax.rsqrt(jnp.mean(xn * xn, axis=-1, keepdims=True) + NORM_EPS)
        o_ref[...] = (xn * r) * fw_ref[...]


def _cumsum_matrix(ts):
    r = jnp.arange(ts)[:, None]
    c = jnp.arange(ts)[None, :]
    m64 = (r >= c) & (r // GD_CHUNK == c // GD_CHUNK)
    m128 = (r >= c) & (r // PAIR == c // PAIR)
    return jnp.concatenate([m64, m128], axis=0).astype(BF16)


def _layer(x2, norm_w, w_a, w_b, rgp, rgw, gp, gdcw, w_out, final_w, mlnw, gdnw, cs, layer, seq,
           final_norm):
    t = x2.shape[0]
    ts = TS_BLOCK
    nb = t // ts
    per_layer = lambda a, **kw: pl.BlockSpec((None,) + a.shape[1:],
                                             lambda i: (layer,) + (0,) * (a.ndim - 1), **kw)
    once = dict(pipeline_mode=pl.Buffered(1))
    return pl.pallas_call(
        functools.partial(_layer_kernel, blocks_per_seq=seq // ts, final_norm=final_norm),
        grid=(nb + 1,),
        in_specs=[pl.BlockSpec((ts, D_MODEL), lambda i: (jnp.minimum(i, nb - 1), 0)),
                  pl.BlockSpec((ts, D_MODEL), lambda i: (jnp.maximum(i - 1, 0), 0)),
                  per_layer(norm_w),
                  pl.BlockSpec((None, D_MODEL, WCOL_SPLIT), lambda i: (layer, 0, 0), **once),
                  per_layer(w_b, **once), per_layer(rgp), per_layer(rgw), per_layer(gp),
                  per_layer(gdcw), per_layer(w_out, **once),
                  pl.BlockSpec(final_w.shape, lambda i: (0, 0)),
                  per_layer(mlnw), per_layer(gdnw), pl.BlockSpec(cs.shape, lambda i: (0, 0))],
        out_specs=pl.BlockSpec((ts, D_MODEL), lambda i: (jnp.maximum(i - 1, 0), 0)),
        out_shape=jax.ShapeDtypeStruct((t, D_MODEL), F32),
        scratch_shapes=[pltpu.VMEM((2, ts, D_PROJ), F32),
                        pltpu.VMEM((2, GD_CONV // LANES, ts, LANES), F32),
                        pltpu.VMEM((ts, D_MODEL), BF16),
                        pltpu.VMEM((RAW_TILES, ts, LANES), F32),
                        pltpu.VMEM((ts, W), F32),
                        pltpu.VMEM((RG_WIDTH // LANES, ts, LANES), F32),
                        pltpu.VMEM((SUBLANES, RG_WIDTH), F32),
                        pltpu.VMEM((SUBLANES, GD_CONV), F32),
                        pltpu.VMEM((ts, D_MIX), BF16),
                        pltpu.VMEM((SUBLANES, RG_WIDTH), F32),
                        pltpu.VMEM((ML_HEADS, ML_DH, 2 * ML_DH), F32),
                        pltpu.VMEM((ML_HEADS, SUBLANES, LANES), F32),
                        pltpu.VMEM((GD_HEADS, GD_DK, GD_DV), F32)],
        compiler_params=pltpu.CompilerParams(dimension_semantics=("arbitrary",),
                                             vmem_limit_bytes=VMEM_LIMIT),
        name="layer",
    )(x2, x2, norm_w, w_a, w_b, rgp, rgw, gp, gdcw, w_out, final_w, mlnw, gdnw, cs)


def _w_in_tail(w_in):
    pad = jnp.zeros(w_in.shape[:-1] + (LANES - GATE_LANES,), w_in.dtype)
    return jnp.concatenate([
        w_in[..., SRC_GD:SRC_GD_GATES], w_in[..., SRC_ML_GATES:SRC_GD],
        w_in[..., SRC_GD_GATES:], pad], axis=-1)


def _rg_gate_blockdiag(gate_w):
    depth = gate_w.shape[0]
    nb = RG_BLOCKS // 2
    eye = jnp.eye(nb, dtype=gate_w.dtype)
    halves = []
    for hf in range(2):
        blk = gate_w[:, :, hf * nb:(hf + 1) * nb]
        bd = jnp.einsum('lgncd,nm->lgncmd', blk, eye)
        bd = bd.reshape(depth, 2, nb * RG_BLOCK, nb * RG_BLOCK)
        halves.append(jnp.concatenate([bd[:, 0], bd[:, 1]], axis=-1))
    return jnp.stack(halves, axis=1)


def _gate_params(ml_gate_b, gd_dt_bias, gd_a_log):
    depth = ml_gate_b.shape[0]
    zeros = lambda n: jnp.zeros((depth, n), F32)
    row0 = jnp.concatenate([ml_gate_b[:, 0], ml_gate_b[:, 1], gd_dt_bias,
                            zeros(LANES - LANE_GD_B)], axis=-1)
    row1 = jnp.concatenate([zeros(LANE_GD_A), gd_a_log, zeros(LANES - LANE_GD_B)], axis=-1)
    rest = jnp.zeros((depth, SUBLANES - 2, LANES), F32)
    return jnp.concatenate([row0[:, None], row1[:, None], rest], axis=1)


def kernel(x, norm_w, w_in, rg_conv_w, rg_conv_b, rg_gate_w, rg_gate_b, rg_lambda, ml_gate_b,
           ml_norm_w, gd_conv_w, gd_a_log, gd_dt_bias, gd_norm_w, w_out, final_norm_w):
    bsz, seq, _ = x.shape
    depth = w_in.shape[0]
    t = bsz * seq
    w_a = w_in.astype(BF16)
    w_b = _w_in_tail(w_in).astype(BF16)
    w_o = w_out.astype(BF16)
    rgp = jnp.concatenate([rg_conv_w, rg_conv_b[:, None], rg_gate_b, rg_lambda[:, None]],
                          axis=1)
    rgw = _rg_gate_blockdiag(rg_gate_w).astype(BF16)
    gp = _gate_params(ml_gate_b, gd_dt_bias, gd_a_log)
    cs = _cumsum_matrix(TS_BLOCK)

    x2 = x.reshape(t, D_MODEL)
    for l in range(depth):
        x2 = _layer(x2, norm_w[:, None], w_a, w_b, rgp, rgw, gp, gd_conv_w, w_o, final_norm_w[None],
                    ml_norm_w[:, None], gd_norm_w[:, None], cs, l, seq, l == depth - 1)
    return x2.reshape(bsz, seq, D_MODEL)
```

```python
import functools

import jax
import jax.numpy as jnp
from jax import lax
from jax.experimental import pallas as pl
from jax.experimental.pallas import tpu as pltpu

F32 = jnp.float32
BF16 = jnp.bfloat16

D_MODEL = 1024
CONV_K = 4
CONV_CLASSES = 4
NORM_EPS = 1e-6
RG_WIDTH = 512
RG_BLOCKS = 8
RG_BLOCK = RG_WIDTH // RG_BLOCKS
RG_C = 8.0
ML_HEADS = 4
ML_DH = 128
ML_WIDTH = ML_HEADS * ML_DH
GD_HEADS = 4
GD_DK = 128
GD_DV = 128
GD_QK = GD_HEADS * GD_DK
GD_WIDTH = GD_HEADS * GD_DV
GD_CONV = 2 * GD_QK + GD_WIDTH
D_MIX = RG_WIDTH + ML_WIDTH + GD_WIDTH
GD_CHUNK = 64
PAIR = 2 * GD_CHUNK
SUBLANES = 8
LANES = 128

W = 512
WCOL_RG_X, WCOL_RG_Z = 0 * W, 1 * W
WCOL_ML_Q, WCOL_ML_K, WCOL_ML_V, WCOL_ML_O, WCOL_ML_Z = 2 * W, 3 * W, 4 * W, 5 * W, 6 * W
WCOL_GD_QKV, WCOL_GD_Z = 7 * W, 10 * W
WCOL_GATES = 11 * W
WCOL_SPLIT = WCOL_GD_QKV
D_PROJ = WCOL_GATES + LANES
OFF_RG_A, OFF_RG_B, OFF_RG_G = 0 * W, 1 * W, 2 * W
OFF_ML_Q, OFF_ML_K, OFF_ML_V, OFF_ML_G = 3 * W, 4 * W, 5 * W, 6 * W
OFF_GD_Q, OFF_GD_K, OFF_GD_V, OFF_GD_G = 7 * W, 8 * W, 9 * W, 10 * W
OFF_GATES = 11 * W
LANE_ML_I, LANE_ML_F, LANE_GD_A, LANE_GD_B = 0, 4, 8, 12
GATE_LANES = 16
RAW_RG, RAW_GD = 0, RG_WIDTH // LANES
RAW_TILES = RAW_GD + GD_CONV // LANES
SRC_ML_GATES = 2 * RG_WIDTH + 5 * ML_WIDTH
SRC_GD = SRC_ML_GATES + 2 * ML_HEADS
SRC_GD_GATES = SRC_GD + 2 * GD_QK + 2 * GD_WIDTH

VMEM_LIMIT = 56 * 1024 * 1024
TS_BLOCK = 256
LAYER_ORDER = "GPP" * 9 + "MRPG" * 5


def _dot(a, b):
    return jnp.dot(a.astype(BF16), b.astype(BF16), preferred_element_type=F32)


def _dot_nt(a, b):
    return lax.dot_general(a.astype(BF16), b.astype(BF16), (((1,), (1,)), ((), ())),
                           preferred_element_type=F32)


def _dot_tn(a, b):
    return lax.dot_general(a.astype(BF16), b.astype(BF16), (((0,), (0,)), ((), ())),
                           preferred_element_type=F32)


def _split3(x):
    hi = x.astype(BF16)
    r1 = x - hi.astype(F32)
    mid = r1.astype(BF16)
    lo = (r1 - mid.astype(F32)).astype(BF16)
    return hi, mid, lo


def _softplus(x):
    return jnp.maximum(x, 0.0) + jnp.log1p(jnp.exp(-jnp.abs(x)))


def _sigmoid(x):
    return 1.0 / (1.0 + jnp.exp(-x))


def _silu(x):
    return x * _sigmoid(x)


def _causal_conv(x_ref, tile, tail_ref, tcol0, cw_ref, out_ref, out_tile, post):
    ts = x_ref.shape[1]
    nc = CONV_CLASSES
    n = ts // nc
    tcols = slice(tcol0, tcol0 + LANES)
    tail = tail_ref[:, tcols]
    new_tail = x_ref[tile, ts - SUBLANES:ts, :]
    cls = [x_ref[tile, pl.ds(r, n, stride=nc), :] for r in range(nc)]
    first = lax.broadcasted_iota(jnp.int32, (n, LANES), 0) == 0
    prev = {r: jnp.where(first, tail[SUBLANES - nc + r:SUBLANES - nc + r + 1, :],
                         pltpu.roll(cls[r], 1, 0))
            for r in range(nc - CONV_K + 1, nc)}
    for r in range(nc):
        acc = None
        for k in range(CONV_K):
            j = CONV_K - 1 - k
            tap = cls[r - j] if r >= j else prev[r - j + nc]
            term = tap * cw_ref[k:k + 1, tcols]
            acc = term if acc is None else acc + term
        out_ref[out_tile, pl.ds(r, n, stride=nc), :] = post(acc)
    tail_ref[:, tcols] = new_tail


def _inproj_stream(x_ref, nw_ref, wa_ref, wb_ref, rgp_ref, rgw_ref, gp_ref, gdcw_ref, p_ref,
                   gdo_ref, hn_s, raw_s, raw_z, xc_s, rg_tail, gd_tail):
    x = x_ref[...]
    r = lax.rsqrt(jnp.mean(x * x, axis=-1, keepdims=True) + NORM_EPS)
    hn_s[...] = ((x * r) * nw_ref[...]).astype(BF16)

    H = W // 2

    def proj(col, width):
        if col < WCOL_SPLIT:
            w = wa_ref[:, col:col + width]
        else:
            w = wb_ref[:, col - WCOL_SPLIT:col - WCOL_SPLIT + width]
        return jnp.dot(hn_s[...], w, preferred_element_type=F32)

    def mm(dst_ref, dst, src, scale=None):
        def run():
            y = proj(src, H)
            dst_ref[:, dst:dst + H] = y if scale is None else y * scale
        return run

    def mm_raw(tile0, src):
        def run():
            y = proj(src, H)
            for i in range(H // LANES):
                raw_s[tile0 + i] = y[:, i * LANES:(i + 1) * LANES]
        return run

    def xc_half(c):
        return jnp.concatenate([xc_s[c // LANES + i] for i in range(H // LANES)], axis=1)

    def mm_rg_gates(c):
        def run():
            g = jnp.dot(xc_half(c).astype(BF16), rgw_ref[c // H],
                        preferred_element_type=F32)
            p_ref[:, OFF_RG_B + c:OFF_RG_B + c + H] = g[:, :H]
            raw_z[:, c:c + H] = g[:, H:]
        return run

    def ep_rg_conv(c):
        def run():
            for t in range(c // LANES, (c + H) // LANES):
                bias = rgp_ref[4:5, t * LANES:(t + 1) * LANES]
                _causal_conv(raw_s, RAW_RG + t, rg_tail, t * LANES, rgp_ref, xc_s, t,
                             lambda y, bias=bias: y + bias)
        return run

    def ep_rg_ab(c):
        def run():
            gr = p_ref[:, OFF_RG_B + c:OFF_RG_B + c + H] + rgp_ref[5:6, c:c + H]
            gi = raw_z[:, c:c + H] + rgp_ref[6:7, c:c + H]
            log_a = (-RG_C * _sigmoid(gr)) * _softplus(-rgp_ref[7:8, c:c + H])
            a = jnp.exp(log_a)
            p_ref[:, OFF_RG_A + c:OFF_RG_A + c + H] = a
            p_ref[:, OFF_RG_B + c:OFF_RG_B + c + H] = (
                jnp.sqrt(-jnp.tanh(log_a) * (a * a + 1.0)) * (_sigmoid(gi) * xc_half(c)))
        return run

    def ep_silu(off):
        def run():
            p_ref[:, off:off + H] = _silu(p_ref[:, off:off + H])
        return run

    def ep_gd_conv(tcol, norm_scale):
        def post(y):
            y = _silu(y)
            if norm_scale is None:
                return y
            y = y * lax.rsqrt(jnp.sum(y * y, axis=-1, keepdims=True) + NORM_EPS)
            return y if norm_scale == 1.0 else y * norm_scale

        def run():
            for t in range(tcol // LANES, (tcol + H) // LANES):
                _causal_conv(raw_s, RAW_GD + t, gd_tail, t * LANES, gdcw_ref, gdo_ref, t, post)
        return run

    def ep_ml_gate(c):
        def run():
            p_ref[:, OFF_ML_G + c:OFF_ML_G + c + H] = (
                _sigmoid(p_ref[:, OFF_ML_G + c:OFF_ML_G + c + H]) * _silu(raw_z[:, c:c + H]))
        return run

    def mm_gates():
        p_ref[:, OFF_GATES:OFF_GATES + LANES] = proj(WCOL_GATES, LANES)

    def ep_gates():
        gates = p_ref[:, OFF_GATES:OFF_GATES + LANES] + gp_ref[0:1, :]
        lane = lax.broadcasted_iota(jnp.int32, gates.shape, 1)
        is_i = lane < LANE_ML_F
        is_f = (lane >= LANE_ML_F) & (lane < LANE_ML_F + ML_HEADS)
        is_a = (lane >= LANE_GD_A) & (lane < LANE_GD_A + GD_HEADS)
        log_f = -_softplus(-gates)
        g_gd = -jnp.exp(gp_ref[1:2, :]) * _softplus(gates)
        p_ref[:, OFF_GATES:OFF_GATES + LANES] = jnp.where(
            is_i, gates, jnp.where(is_f, log_f, jnp.where(is_a, g_gd, _sigmoid(gates))))

    units = []
    for c in range(0, W, H):
        units += [
            ([mm_raw(RAW_RG + c // LANES, WCOL_RG_X + c)], ep_rg_conv(c)),
            ([mm_raw(RAW_GD + c // LANES, WCOL_GD_QKV + c)], ep_gd_conv(c, GD_DK ** -0.5)),
            ([mm(p_ref, OFF_ML_Q + c, WCOL_ML_Q + c, ML_DH ** -0.5)], None),
            ([mm_rg_gates(c)], ep_rg_ab(c)),
            ([mm_raw(RAW_GD + (W + c) // LANES, WCOL_GD_QKV + W + c)], ep_gd_conv(W + c, 1.0)),
            ([mm(p_ref, OFF_ML_K + c, WCOL_ML_K + c)], None),
            ([mm_raw(RAW_GD + (2 * W + c) // LANES, WCOL_GD_QKV + 2 * W + c)],
             ep_gd_conv(2 * W + c, None)),
            ([mm(p_ref, OFF_ML_V + c, WCOL_ML_V + c)], None),
            ([mm(p_ref, OFF_RG_G + c, WCOL_RG_Z + c)], ep_silu(OFF_RG_G + c)),
            ([mm(p_ref, OFF_ML_G + c, WCOL_ML_O + c), mm(raw_z, c, WCOL_ML_Z + c)], ep_ml_gate(c)),
            ([mm(p_ref, OFF_GD_G + c, WCOL_GD_Z + c)], ep_silu(OFF_GD_G + c)),
        ]
    units.append(([mm_gates], ep_gates))
    pending = None
    for mms, ep in units:
        for run in mms:
            run()
        if pending is not None:
            pending()
        pending = ep
        yield
    if pending is not None:
        pending()


def _rglru(p_ref, y_ref, rg_h, project):
    ts = p_ref.shape[0]
    ng = ts // SUBLANES
    a3 = p_ref[:, OFF_RG_A:OFF_RG_A + W].reshape(ng, SUBLANES, RG_WIDTH)
    b3 = p_ref[:, OFF_RG_B:OFF_RG_B + W].reshape(ng, SUBLANES, RG_WIDTH)
    sub = lax.broadcasted_iota(jnp.int32, (ng, SUBLANES, RG_WIDTH), 1)
    d = 1
    while d < SUBLANES:
        a_sh = pltpu.roll(a3, d, 1)
        b_sh = pltpu.roll(b3, d, 1)
        valid = sub >= d
        b3 = jnp.where(valid, a3 * b_sh + b3, b3)
        a3 = jnp.where(valid, a3 * a_sh, a3)
        d *= 2
        yield
    carry = rg_h[0:1, :]
    groups = []
    for g in range(ng):
        hg = b3[g] + a3[g] * carry
        groups.append(hg)
        carry = hg[SUBLANES - 1:SUBLANES, :]
    rg_h[...] = jnp.broadcast_to(carry, rg_h.shape)
    yield
    h = jnp.concatenate(groups, axis=0)
    y_ref[:, 0:RG_WIDTH] = (h * p_ref[:, OFF_RG_G:OFF_RG_G + W]).astype(y_ref.dtype)
    project(0)


def _gate_tables(p_ref, cs_ref):
    ts = p_ref.shape[0]
    gates = p_ref[:, OFF_GATES:OFF_GATES + LANES]
    lane = lax.broadcasted_iota(jnp.int32, (ts, LANES), 1)
    is_i = lane < LANE_ML_F
    is_f = (lane >= LANE_ML_F) & (lane < LANE_ML_F + ML_HEADS)
    is_a = (lane >= LANE_GD_A) & (lane < LANE_GD_A + GD_HEADS)
    x = jnp.where(is_f | is_a, gates, 0.0)
    hi, mid, lo = _split3(x)
    packed = (hi.astype(F32) + pltpu.roll(mid.astype(F32), GATE_LANES, 1)
              + pltpu.roll(lo.astype(F32), 2 * GATE_LANES, 1))
    cs = jnp.dot(cs_ref[...], packed.astype(BF16), preferred_element_type=F32)
    cs = cs + pltpu.roll(cs, LANES - GATE_LANES, 1) + pltpu.roll(cs, LANES - 2 * GATE_LANES, 1)
    csum_gd = cs[:ts]
    csum_ml = cs[ts:]
    z = jnp.where(is_i, gates, jnp.where(is_f, csum_ml, jnp.where(is_a, csum_gd, 0.0)))
    zs = jnp.concatenate(_split3(z), axis=0)
    sel = (lax.broadcasted_iota(jnp.int32, (GATE_LANES, LANES), 0)
           == lax.broadcasted_iota(jnp.int32, (GATE_LANES, LANES), 1)).astype(BF16)
    zt = lax.dot_general(sel, zs, (((1,), (1,)), ((), ())), preferred_element_type=F32)
    rows = zt[:, :ts] + zt[:, ts:2 * ts] + zt[:, 2 * ts:]
    return gates, csum_gd, csum_ml, rows


def _mlstm(p_ref, mlnw_ref, y_ref, ml_c, ml_m, gates, csum_ml, rows, project):
    ts = p_ref.shape[0]
    L = PAIR
    r_i = lax.broadcasted_iota(jnp.int32, (L, L), 0)
    c_i = lax.broadcasted_iota(jnp.int32, (L, L), 1)
    causal = r_i >= c_i
    ones_blk = jnp.ones((L, ML_DH), F32)
    units = [(h, c) for c in range(ts // L) for h in range(ML_HEADS)]

    st = {}
    for h, c in units:
        rs = slice(c * L, (c + 1) * L)
        q = p_ref[rs, OFF_ML_Q + h * ML_DH:OFF_ML_Q + (h + 1) * ML_DH]
        k = p_ref[rs, OFF_ML_K + h * ML_DH:OFF_ML_K + (h + 1) * ML_DH]
        li_c = gates[rs, LANE_ML_I + h:LANE_ML_I + h + 1]
        bq_c = csum_ml[rs, LANE_ML_F + h:LANE_ML_F + h + 1]
        r_row = (rows[LANE_ML_I + h:LANE_ML_I + h + 1, rs]
                 - rows[LANE_ML_F + h:LANE_ML_F + h + 1, rs])
        dm = jnp.where(causal, bq_c + r_row, -1e30)
        mx = jnp.max(dm, axis=-1, keepdims=True)
        g_last = bq_c[L - 1:L, :]
        ws = g_last - bq_c + li_c
        mws = jnp.max(ws, axis=0, keepdims=True)
        st[h, c] = dict(q=q, k=k, bq_c=bq_c, mx=mx, g_last=g_last, mws=mws,
                        pm=jnp.exp(dm - mx), wk=jnp.exp(ws - mws), qk=_dot_nt(q, k))
    yield
    for h, c in units:
        u = st[h, c]
        rs = slice(c * L, (c + 1) * L)
        v = p_ref[rs, OFF_ML_V + h * ML_DH:OFF_ML_V + (h + 1) * ML_DH]
        v_aug = jnp.concatenate([v, ones_blk], axis=1)
        u["intra"] = _dot(u["qk"] * u["pm"], v_aug)
        u["kv"] = _dot_tn(u["k"] * u["wk"], v_aug)

    yield
    c_aug = [ml_c[h] for h in range(ML_HEADS)]
    m_prev = [ml_m[h][0:1, 0:1] for h in range(ML_HEADS)]
    outs = [[] for _ in range(ML_HEADS)]
    for c in range(ts // L):
        qc = [_dot(st[h, c]["q"], c_aug[h]) for h in range(ML_HEADS)]
        for h in range(ML_HEADS):
            u = st[h, c]
            m_inter = u["bq_c"] + m_prev[h]
            m_t = jnp.maximum(m_inter, u["mx"])
            na = jnp.exp(u["mx"] - m_t) * u["intra"] + jnp.exp(m_inter - m_t) * qc[h]
            outs[h].append(na[:, :ML_DH] / jnp.maximum(jnp.abs(na[:, ML_DH:]), jnp.exp(-m_t)))
            m_new = jnp.maximum(u["g_last"] + m_prev[h], u["mws"])
            c_aug[h] = (jnp.exp(u["g_last"] + m_prev[h] - m_new) * c_aug[h]
                        + jnp.exp(u["mws"] - m_new) * u["kv"])
            m_prev[h] = m_new
        yield
    for h in range(ML_HEADS):
        sl = slice(h * ML_DH, (h + 1) * ML_DH)
        ml_c[h] = c_aug[h]
        ml_m[h] = jnp.broadcast_to(m_prev[h], ml_m.shape[1:])
        hh = jnp.concatenate(outs[h], axis=0)
        hn = hh * lax.rsqrt(jnp.mean(hh * hh, axis=-1, keepdims=True) + NORM_EPS)
        y_ml = (hn * mlnw_ref[0:1, sl]) * p_ref[:, OFF_ML_G + h * ML_DH:OFF_ML_G + (h + 1) * ML_DH]
        y_ref[:, RG_WIDTH + h * ML_DH:RG_WIDTH + (h + 1) * ML_DH] = y_ml.astype(y_ref.dtype)
    project(RG_WIDTH)


def _gdn(p_ref, gdo_ref, gdnw_ref, y_ref, gd_s, gates, csum_gd, rows, project):
    ts = p_ref.shape[0]
    L = PAIR
    C = GD_CHUNK
    r_i = lax.broadcasted_iota(jnp.int32, (L, L), 0)
    c_i = lax.broadcasted_iota(jnp.int32, (L, L), 1)
    same = (r_i >= C) == (c_i >= C)
    incl = same & (r_i >= c_i)
    strict = same & (r_i > c_i)
    eye = (r_i == c_i).astype(F32)
    second = lax.broadcasted_iota(jnp.int32, (L, 1), 0) >= C
    zeros_c = jnp.zeros((C, GD_DV), F32)
    units = [(h, j) for j in range(ts // L) for h in range(GD_HEADS)]

    st = {}
    for h, j in units:
        rs = slice(j * L, (j + 1) * L)
        q = gdo_ref[h, rs, :]
        k = gdo_ref[GD_HEADS + h, rs, :]
        beta = gates[rs, LANE_GD_B + h:LANE_GD_B + h + 1]
        gc_c = csum_gd[rs, LANE_GD_A + h:LANE_GD_A + h + 1]
        gc_r = rows[LANE_GD_A + h:LANE_GD_A + h + 1, rs]
        gam = jnp.where(incl, jnp.exp(jnp.where(incl, gc_c - gc_r, 0.0)), 0.0)
        kb = k * beta
        egc = jnp.exp(gc_c)
        g_first = gc_c[C - 1:C, :]
        g_second = gc_c[L - 1:L, :]
        st[h, j] = dict(k=k, kb=kb, beta=beta, gam=gam, egc=egc, q_dec=q * egc,
                        g_last=(g_first, g_second),
                        k_dec=k * jnp.exp(jnp.where(second, g_second, g_first) - gc_c),
                        kq=_dot_nt(jnp.concatenate([kb, q], axis=0), k))
    yield
    for key in units:
        u = st[key]
        u["aqk"] = u["kq"][L:] * u["gam"]
        pw = jnp.where(strict, -(u["kq"][:L] * u["gam"]), 0.0)
        u["t_inv"] = eye + pw
        u["pw"] = _dot(pw, pw)
    yield
    span = 4
    while span < C:
        for key in units:
            u = st[key]
            both = _dot(jnp.concatenate([u["pw"], u["t_inv"]], axis=0), u["pw"])
            u["pw"] = both[:L]
            u["t_inv"] = u["t_inv"] + both[L:]
        span *= 2
        yield
    for key in units:
        u = st[key]
        u["t_inv"] = u["t_inv"] + _dot(u["t_inv"], u["pw"])
    yield
    for h, j in units:
        u = st[h, j]
        rs = slice(j * L, (j + 1) * L)
        v = gdo_ref[2 * GD_HEADS + h, rs, :]
        uw = _dot(u["t_inv"], jnp.concatenate([v * u["beta"], u["kb"] * u["egc"]], axis=1))
        u["u"] = uw[:, :GD_DV]
        u["w"] = uw[:, GD_DV:]

    yield
    state = [gd_s[h] for h in range(GD_HEADS)]
    outs = [[] for _ in range(GD_HEADS)]
    for j in range(ts // L):
        for half in range(2):
            hs = slice(half * C, (half + 1) * C)
            wq = [_dot(jnp.concatenate([st[h, j]["w"][hs], st[h, j]["q_dec"][hs]], axis=0), state[h])
                  for h in range(GD_HEADS)]
            v_new = [st[h, j]["u"][hs] - wq[h][:C] for h in range(GD_HEADS)]
            for h in range(GD_HEADS):
                v_pad = jnp.concatenate([v_new[h], zeros_c] if half == 0 else [zeros_c, v_new[h]],
                                        axis=0)
                outs[h].append(wq[h][C:] + _dot(st[h, j]["aqk"][hs], v_pad))
            for h in range(GD_HEADS):
                u = st[h, j]
                state[h] = (state[h] * jnp.exp(u["g_last"][half])
                            + _dot_tn(u["k_dec"][hs], v_new[h]))
            yield
    for h in range(GD_HEADS):
        gd_s[h] = state[h]
        o = jnp.concatenate(outs[h], axis=0)
        on = o * lax.rsqrt(jnp.mean(o * o, axis=-1, keepdims=True) + NORM_EPS)
        y_gd = on * gdnw_ref[0:1, :] * p_ref[:, OFF_GD_G + h * GD_DV:OFF_GD_G + (h + 1) * GD_DV]
        y_ref[:, RG_WIDTH + ML_WIDTH + h * GD_DV:RG_WIDTH + ML_WIDTH + (h + 1) * GD_DV] = (
            y_gd.astype(y_ref.dtype))
    project(RG_WIDTH + ML_WIDTH)


def _interleave(order, stages):
    live = dict(stages)

    def step(key):
        if key in live:
            try:
                next(live[key])
            except StopIteration:
                del live[key]

    for key in order:
        step(key)
    while live:
        for key in list(live):
            step(key)


def _layer_kernel(xn_ref, xc_ref, nw_ref, wa_ref, wb_ref, rgp_ref, rgw_ref, gp_ref, gdcw_ref,
                  wo_ref, fw_ref, mlnw_ref, gdnw_ref, cs_ref, o_ref,
                  p_s, gdo_s, hn_s, raw_s, raw_z, xc_s, rg_tail, gd_tail,
                  y_ref, rg_h, ml_c, ml_m, gd_s,
                  *, blocks_per_seq, final_norm):
    i = pl.program_id(0)
    slot = lax.rem(i, 2)
    p_new = p_s.at[slot]
    p_cur = p_s.at[1 - slot]

    @pl.when(i == 0)
    def _():
        p_s[1] = jnp.zeros(p_s.shape[1:], p_s.dtype)
        gdo_s[1] = jnp.zeros(gdo_s.shape[1:], gdo_s.dtype)

    @pl.when(lax.rem(i, blocks_per_seq) == 0)
    def _():
        rg_tail[...] = jnp.zeros_like(rg_tail)
        gd_tail[...] = jnp.zeros_like(gd_tail)

    @pl.when(lax.rem(jnp.maximum(i - 1, 0), blocks_per_seq) == 0)
    def _():
        rg_h[...] = jnp.zeros_like(rg_h)
        ml_c[...] = jnp.zeros_like(ml_c)
        ml_m[...] = jnp.zeros_like(ml_m)
        gd_s[...] = jnp.zeros_like(gd_s)

    o_ref[...] = xc_ref[...]

    def project(k0):
        o_ref[...] += jnp.dot(y_ref[:, k0:k0 + W], wo_ref[k0:k0 + W, :],
                              preferred_element_type=F32)

    gates, csum_gd, csum_ml, rows = _gate_tables(p_cur, cs_ref)
    _interleave(LAYER_ORDER, dict(
        P=_inproj_stream(xn_ref, nw_ref, wa_ref, wb_ref, rgp_ref, rgw_ref, gp_ref, gdcw_ref,
                         p_new, gdo_s.at[slot], hn_s, raw_s, raw_z, xc_s, rg_tail, gd_tail),
        G=_gdn(p_cur, gdo_s.at[1 - slot], gdnw_ref, y_ref, gd_s, gates, csum_gd, rows, project),
        M=_mlstm(p_cur, mlnw_ref, y_ref, ml_c, ml_m, gates, csum_ml, rows, project),
        R=_rglru(p_cur, y_ref, rg_h, project)))
    if final_norm:
        xn = o_ref[...]
        r = lax.rsqrt(jnp.mean(xn * xn, axis=-1, keepdims=True) + NORM_EPS)
        o_ref[...] = (xn * r) * fw_ref[...]


def _cumsum_matrix(ts):
    r = jnp.arange(ts)[:, None]
    c = jnp.arange(ts)[None, :]
    m64 = (r >= c) & (r // GD_CHUNK == c // GD_CHUNK)
    m128 = (r >= c) & (r // PAIR == c // PAIR)
    return jnp.concatenate([m64, m128], axis=0).astype(BF16)


def _layer(x2, norm_w, w_a, w_b, rgp, rgw, gp, gdcw, w_out, final_w, mlnw, gdnw, cs, layer, seq,
           final_norm):
    t = x2.shape[0]
    ts = TS_BLOCK
    nb = t // ts
    per_layer = lambda a, **kw: pl.BlockSpec((None,) + a.shape[1:],
                                             lambda i: (layer,) + (0,) * (a.ndim - 1), **kw)
    once = dict(pipeline_mode=pl.Buffered(1))
    return pl.pallas_call(
        functools.partial(_layer_kernel, blocks_per_seq=seq // ts, final_norm=final_norm),
        grid=(nb + 1,),
        in_specs=[pl.BlockSpec((ts, D_MODEL), lambda i: (jnp.minimum(i, nb - 1), 0)),
                  pl.BlockSpec((ts, D_MODEL), lambda i: (jnp.maximum(i - 1, 0), 0)),
                  per_layer(norm_w),
                  pl.BlockSpec((None, D_MODEL, WCOL_SPLIT), lambda i: (layer, 0, 0), **once),
                  per_layer(w_b, **once), per_layer(rgp), per_layer(rgw), per_layer(gp),
                  per_layer(gdcw), per_layer(w_out, **once),
                  pl.BlockSpec(final_w.shape, lambda i: (0, 0)),
                  per_layer(mlnw), per_layer(gdnw), pl.BlockSpec(cs.shape, lambda i: (0, 0))],
        out_specs=pl.BlockSpec((ts, D_MODEL), lambda i: (jnp.maximum(i - 1, 0), 0)),
        out_shape=jax.ShapeDtypeStruct((t, D_MODEL), F32),
        scratch_shapes=[pltpu.VMEM((2, ts, D_PROJ), F32),
                        pltpu.VMEM((2, GD_CONV // LANES, ts, LANES), F32),
                        pltpu.VMEM((ts, D_MODEL), BF16),
                        pltpu.VMEM((RAW_TILES, ts, LANES), F32),
                        pltpu.VMEM((ts, W), F32),
                        pltpu.VMEM((RG_WIDTH // LANES, ts, LANES), F32),
                        pltpu.VMEM((SUBLANES, RG_WIDTH), F32),
                        pltpu.VMEM((SUBLANES, GD_CONV), F32),
                        pltpu.VMEM((ts, D_MIX), BF16),
                        pltpu.VMEM((SUBLANES, RG_WIDTH), F32),
                        pltpu.VMEM((ML_HEADS, ML_DH, 2 * ML_DH), F32),
                        pltpu.VMEM((ML_HEADS, SUBLANES, LANES), F32),
                        pltpu.VMEM((GD_HEADS, GD_DK, GD_DV), F32)],
        compiler_params=pltpu.CompilerParams(dimension_semantics=("arbitrary",),
                                             vmem_limit_bytes=VMEM_LIMIT),
        name="layer",
    )(x2, x2, norm_w, w_a, w_b, rgp, rgw, gp, gdcw, w_out, final_w, mlnw, gdnw, cs)


def _w_in_tail(w_in):
    pad = jnp.zeros(w_in.shape[:-1] + (LANES - GATE_LANES,), w_in.dtype)
    return jnp.concatenate([
        w_in[..., SRC_GD:SRC_GD_GATES], w_in[..., SRC_ML_GATES:SRC_GD],
        w_in[..., SRC_GD_GATES:], pad], axis=-1)


def _rg_gate_blockdiag(gate_w):
    depth = gate_w.shape[0]
    nb = RG_BLOCKS // 2
    eye = jnp.eye(nb, dtype=gate_w.dtype)
    halves = []
    for hf in range(2):
        blk = gate_w[:, :, hf * nb:(hf + 1) * nb]
        bd = jnp.einsum('lgncd,nm->lgncmd', blk, eye)
        bd = bd.reshape(depth, 2, nb * RG_BLOCK, nb * RG_BLOCK)
        halves.append(jnp.concatenate([bd[:, 0], bd[:, 1]], axis=-1))
    return jnp.stack(halves, axis=1)


def _gate_params(ml_gate_b, gd_dt_bias, gd_a_log):
    depth = ml_gate_b.shape[0]
    zeros = lambda n: jnp.zeros((depth, n), F32)
    row0 = jnp.concatenate([ml_gate_b[:, 0], ml_gate_b[:, 1], gd_dt_bias,
                            zeros(LANES - LANE_GD_B)], axis=-1)
    row1 = jnp.concatenate([zeros(LANE_GD_A), gd_a_log, zeros(LANES - LANE_GD_B)], axis=-1)
    rest = jnp.zeros((depth, SUBLANES - 2, LANES), F32)
    return jnp.concatenate([row0[:, None], row1[:, None], rest], axis=1)


def kernel(x, norm_w, w_in, rg_conv_w, rg_conv_b, rg_gate_w, rg_gate_b, rg_lambda, ml_gate_b,
           ml_norm_w, gd_conv_w, gd_a_log, gd_dt_bias, gd_norm_w, w_out, final_norm_w):
    bsz, seq, _ = x.shape
    depth = w_in.shape[0]
    t = bsz * seq
    w_a = w_in.astype(BF16)
    w_b = _w_in_tail(w_in).astype(BF16)
    w_o = w_out.astype(BF16)
    rgp = jnp.concatenate([rg_conv_w, rg_conv_b[:, None], rg_gate_b, rg_lambda[:, None]],
                          axis=1)
    rgw = _rg_gate_blockdiag(rg_gate_w).astype(BF16)
    gp = _gate_params(ml_gate_b, gd_dt_bias, gd_a_log)
    cs = _cumsum_matrix(TS_BLOCK)

    x2 = x.reshape(t, D_MODEL)
    for l in range(depth):
        x2 = _layer(x2, norm_w[:, None], w_a, w_b, rgp, rgw, gp, gd_conv_w, w_o, final_norm_w[None],
                    ml_norm_w[:, None], gd_norm_w[:, None], cs, l, seq, l == depth - 1)
    return x2.reshape(bsz, seq, D_MODEL)
```

```python
import functools

import jax
import jax.numpy as jnp
from jax import lax
from jax.experimental import pallas as pl
from jax.experimental.pallas import tpu as pltpu

F32 = jnp.float32
BF16 = jnp.bfloat16

D_MODEL = 1024
CONV_K = 4
CONV_CLASSES = 4
NORM_EPS = 1e-6
RG_WIDTH = 512
RG_BLOCKS = 8
RG_BLOCK = RG_WIDTH // RG_BLOCKS
RG_C = 8.0
ML_HEADS = 4
ML_DH = 128
ML_WIDTH = ML_HEADS * ML_DH
GD_HEADS = 4
GD_DK = 128
GD_DV = 128
GD_QK = GD_HEADS * GD_DK
GD_WIDTH = GD_HEADS * GD_DV
GD_CONV = 2 * GD_QK + GD_WIDTH
D_MIX = RG_WIDTH + ML_WIDTH + GD_WIDTH
GD_CHUNK = 64
PAIR = 2 * GD_CHUNK
SUBLANES = 8
LANES = 128

W = 512
WCOL_RG_X, WCOL_RG_Z = 0 * W, 1 * W
WCOL_ML_Q, WCOL_ML_K, WCOL_ML_V, WCOL_ML_O, WCOL_ML_Z = 2 * W, 3 * W, 4 * W, 5 * W, 6 * W
WCOL_GD_QKV, WCOL_GD_Z = 7 * W, 10 * W
WCOL_GATES = 11 * W
WCOL_SPLIT = WCOL_GD_QKV
D_PROJ = WCOL_GATES + LANES
OFF_RG_A, OFF_RG_B, OFF_RG_G = 0 * W, 1 * W, 2 * W
OFF_ML_Q, OFF_ML_K, OFF_ML_V, OFF_ML_G = 3 * W, 4 * W, 5 * W, 6 * W
OFF_GD_Q, OFF_GD_K, OFF_GD_V, OFF_GD_G = 7 * W, 8 * W, 9 * W, 10 * W
OFF_GATES = 11 * W
LANE_ML_I, LANE_ML_F, LANE_GD_A, LANE_GD_B = 0, 4, 8, 12
GATE_LANES = 16
RAW_RG, RAW_GD = 0, RG_WIDTH // LANES
RAW_TILES = RAW_GD + GD_CONV // LANES
RG_TILES = RG_WIDTH // LANES
RGT_A, RGT_B, RGT_G = 0, RG_TILES, 2 * RG_TILES
RG_CLASSES = 4
SRC_ML_GATES = 2 * RG_WIDTH + 5 * ML_WIDTH
SRC_GD = SRC_ML_GATES + 2 * ML_HEADS
SRC_GD_GATES = SRC_GD + 2 * GD_QK + 2 * GD_WIDTH

VMEM_LIMIT = 56 * 1024 * 1024
TS_BLOCK = 256
LAYER_ORDER = "GPP" * 9 + "MRPG" * 5


def _dot(a, b):
    return jnp.dot(a.astype(BF16), b.astype(BF16), preferred_element_type=F32)


def _dot_nt(a, b):
    return lax.dot_general(a.astype(BF16), b.astype(BF16), (((1,), (1,)), ((), ())),
                           preferred_element_type=F32)


def _dot_tn(a, b):
    return lax.dot_general(a.astype(BF16), b.astype(BF16), (((0,), (0,)), ((), ())),
                           preferred_element_type=F32)


def _split3(x):
    hi = x.astype(BF16)
    r1 = x - hi.astype(F32)
    mid = r1.astype(BF16)
    lo = (r1 - mid.astype(F32)).astype(BF16)
    return hi, mid, lo


def _softplus(x):
    return jnp.maximum(x, 0.0) + jnp.log1p(jnp.exp(-jnp.abs(x)))


def _sigmoid(x):
    return 1.0 / (1.0 + jnp.exp(-x))


def _silu(x):
    return x * _sigmoid(x)


def _causal_conv(x_ref, tile, tail_ref, tcol0, cw_ref, out_ref, out_tile, post):
    ts = x_ref.shape[1]
    nc = CONV_CLASSES
    n = ts // nc
    tcols = slice(tcol0, tcol0 + LANES)
    tail = tail_ref[:, tcols]
    new_tail = x_ref[tile, ts - SUBLANES:ts, :]
    cls = [x_ref[tile, pl.ds(r, n, stride=nc), :] for r in range(nc)]
    first = lax.broadcasted_iota(jnp.int32, (n, LANES), 0) == 0
    prev = {r: jnp.where(first, tail[SUBLANES - nc + r:SUBLANES - nc + r + 1, :],
                         pltpu.roll(cls[r], 1, 0))
            for r in range(nc - CONV_K + 1, nc)}
    for r in range(nc):
        acc = None
        for k in range(CONV_K):
            j = CONV_K - 1 - k
            tap = cls[r - j] if r >= j else prev[r - j + nc]
            term = tap * cw_ref[k:k + 1, tcols]
            acc = term if acc is None else acc + term
        out_ref[out_tile, pl.ds(r, n, stride=nc), :] = post(acc)
    tail_ref[:, tcols] = new_tail


def _inproj_stream(x_ref, nw_ref, wa_ref, wb_ref, rgp_ref, rgw_ref, gp_ref, gdcw_ref, p_ref,
                   gdo_ref, rgo_ref, hn_s, raw_s, raw_z, xc_s, rg_tail, gd_tail):
    x = x_ref[...]
    r = lax.rsqrt(jnp.mean(x * x, axis=-1, keepdims=True) + NORM_EPS)
    hn_s[...] = ((x * r) * nw_ref[...]).astype(BF16)

    H = W // 2

    def proj(col, width):
        if col < WCOL_SPLIT:
            w = wa_ref[:, col:col + width]
        else:
            w = wb_ref[:, col - WCOL_SPLIT:col - WCOL_SPLIT + width]
        return jnp.dot(hn_s[...], w, preferred_element_type=F32)

    def mm(dst_ref, dst, src, scale=None):
        def run():
            y = proj(src, H)
            dst_ref[:, dst:dst + H] = y if scale is None else y * scale
        return run

    def mm_raw(tile0, src):
        def run():
            y = proj(src, H)
            for i in range(H // LANES):
                raw_s[tile0 + i] = y[:, i * LANES:(i + 1) * LANES]
        return run

    def xc_half(c):
        return jnp.concatenate([xc_s[c // LANES + i] for i in range(H // LANES)], axis=1)

    def mm_rg_gates(c):
        def run():
            g = jnp.dot(xc_half(c).astype(BF16), rgw_ref[c // H],
                        preferred_element_type=F32)
            p_ref[:, OFF_RG_B + c:OFF_RG_B + c + H] = g[:, :H]
            raw_z[:, c:c + H] = g[:, H:]
        return run

    def ep_rg_conv(c):
        def run():
            for t in range(c // LANES, (c + H) // LANES):
                bias = rgp_ref[4:5, t * LANES:(t + 1) * LANES]
                _causal_conv(raw_s, RAW_RG + t, rg_tail, t * LANES, rgp_ref, xc_s, t,
                             lambda y, bias=bias: y + bias)
        return run

    def ep_rg_ab(c):
        def run():
            gr = p_ref[:, OFF_RG_B + c:OFF_RG_B + c + H] + rgp_ref[5:6, c:c + H]
            gi = raw_z[:, c:c + H] + rgp_ref[6:7, c:c + H]
            log_a = (-RG_C * _sigmoid(gr)) * _softplus(-rgp_ref[7:8, c:c + H])
            a = jnp.exp(log_a)
            b = jnp.sqrt(-jnp.tanh(log_a) * (a * a + 1.0)) * (_sigmoid(gi) * xc_half(c))
            for i in range(H // LANES):
                rgo_ref[RGT_A + c // LANES + i] = a[:, i * LANES:(i + 1) * LANES]
                rgo_ref[RGT_B + c // LANES + i] = b[:, i * LANES:(i + 1) * LANES]
        return run

    def ep_rg_gate(c):
        def run():
            g = _silu(p_ref[:, OFF_RG_G + c:OFF_RG_G + c + H])
            for i in range(H // LANES):
                rgo_ref[RGT_G + c // LANES + i] = g[:, i * LANES:(i + 1) * LANES]
        return run

    def ep_silu(off):
        def run():
            p_ref[:, off:off + H] = _silu(p_ref[:, off:off + H])
        return run

    def ep_gd_conv(tcol, norm_scale):
        def post(y):
            y = _silu(y)
            if norm_scale is None:
                return y
            y = y * lax.rsqrt(jnp.sum(y * y, axis=-1, keepdims=True) + NORM_EPS)
            return y if norm_scale == 1.0 else y * norm_scale

        def run():
            for t in range(tcol // LANES, (tcol + H) // LANES):
                _causal_conv(raw_s, RAW_GD + t, gd_tail, t * LANES, gdcw_ref, gdo_ref, t, post)
        return run

    def ep_ml_gate(c):
        def run():
            p_ref[:, OFF_ML_G + c:OFF_ML_G + c + H] = (
                _sigmoid(p_ref[:, OFF_ML_G + c:OFF_ML_G + c + H]) * _silu(raw_z[:, c:c + H]))
        return run

    def mm_gates():
        p_ref[:, OFF_GATES:OFF_GATES + LANES] = proj(WCOL_GATES, LANES)

    def ep_gates():
        gates = p_ref[:, OFF_GATES:OFF_GATES + LANES] + gp_ref[0:1, :]
        lane = lax.broadcasted_iota(jnp.int32, gates.shape, 1)
        is_i = lane < LANE_ML_F
        is_f = (lane >= LANE_ML_F) & (lane < LANE_ML_F + ML_HEADS)
        is_a = (lane >= LANE_GD_A) & (lane < LANE_GD_A + GD_HEADS)
        log_f = -_softplus(-gates)
        g_gd = -jnp.exp(gp_ref[1:2, :]) * _softplus(gates)
        p_ref[:, OFF_GATES:OFF_GATES + LANES] = jnp.where(
            is_i, gates, jnp.where(is_f, log_f, jnp.where(is_a, g_gd, _sigmoid(gates))))

    units = []
    for c in range(0, W, H):
        units += [
            ([mm_raw(RAW_RG + c // LANES, WCOL_RG_X + c)], ep_rg_conv(c)),
            ([mm_raw(RAW_GD + c // LANES, WCOL_GD_QKV + c)], ep_gd_conv(c, GD_DK ** -0.5)),
            ([mm(p_ref, OFF_ML_Q + c, WCOL_ML_Q + c, ML_DH ** -0.5)], None),
            ([mm_rg_gates(c)], ep_rg_ab(c)),
            ([mm_raw(RAW_GD + (W + c) // LANES, WCOL_GD_QKV + W + c)], ep_gd_conv(W + c, 1.0)),
            ([mm(p_ref, OFF_ML_K + c, WCOL_ML_K + c)], None),
            ([mm_raw(RAW_GD + (2 * W + c) // LANES, WCOL_GD_QKV + 2 * W + c)],
             ep_gd_conv(2 * W + c, None)),
            ([mm(p_ref, OFF_ML_V + c, WCOL_ML_V + c)], None),
            ([mm(p_ref, OFF_RG_G + c, WCOL_RG_Z + c)], ep_rg_gate(c)),
            ([mm(p_ref, OFF_ML_G + c, WCOL_ML_O + c), mm(raw_z, c, WCOL_ML_Z + c)], ep_ml_gate(c)),
            ([mm(p_ref, OFF_GD_G + c, WCOL_GD_Z + c)], ep_silu(OFF_GD_G + c)),
        ]
    units.append(([mm_gates], ep_gates))
    pending = None
    for mms, ep in units:
        for run in mms:
            run()
        if pending is not None:
            pending()
        pending = ep
        yield
    if pending is not None:
        pending()


def _rglru(rgo_ref, yrg_s, y_ref, rg_h, project):
    nc = RG_CLASSES
    ts = rgo_ref.shape[1]
    n = ts // nc
    ng = n // SUBLANES
    tiles = range(RG_TILES)
    ld = lambda t, r: rgo_ref[t, pl.ds(r, n, stride=nc), :]
    pa, pb = {}, {}
    for lt in tiles:
        pa[lt, 0] = ld(RGT_A + lt, 0)
        pb[lt, 0] = ld(RGT_B + lt, 0)
        for r in range(1, nc):
            a = ld(RGT_A + lt, r)
            pb[lt, r] = a * pb[lt, r - 1] + ld(RGT_B + lt, r)
            pa[lt, r] = a * pa[lt, r - 1]
    yield
    sub = lax.broadcasted_iota(jnp.int32, (ng, SUBLANES, LANES), 1)
    a3 = {lt: pa[lt, nc - 1].reshape(ng, SUBLANES, LANES) for lt in tiles}
    b3 = {lt: pb[lt, nc - 1].reshape(ng, SUBLANES, LANES) for lt in tiles}
    d = 1
    while d < SUBLANES:
        for lt in tiles:
            a_sh = pltpu.roll(a3[lt], d, 1)
            b_sh = pltpu.roll(b3[lt], d, 1)
            valid = sub >= d
            b3[lt] = jnp.where(valid, a3[lt] * b_sh + b3[lt], b3[lt])
            a3[lt] = jnp.where(valid, a3[lt] * a_sh, a3[lt])
        d *= 2
    yield
    first = lax.broadcasted_iota(jnp.int32, (n, LANES), 0) == 0
    for lt in tiles:
        cols = slice(lt * LANES, (lt + 1) * LANES)
        carry_in = rg_h[0:1, cols]
        carry = carry_in
        groups = []
        for g in range(ng):
            hg = b3[lt][g] + a3[lt][g] * carry
            groups.append(hg)
            carry = hg[SUBLANES - 1:SUBLANES, :]
        rg_h[:, cols] = jnp.broadcast_to(carry, (SUBLANES, LANES))
        h_last = jnp.concatenate(groups, axis=0)
        h_in = jnp.where(first, carry_in, pltpu.roll(h_last, 1, 0))
        for r in range(nc):
            h = h_last if r == nc - 1 else pb[lt, r] + pa[lt, r] * h_in
            yrg_s[lt, pl.ds(r, n, stride=nc), :] = h * ld(RGT_G + lt, r)
    yield
    y_ref[:, 0:RG_WIDTH] = jnp.concatenate([yrg_s[lt] for lt in tiles], axis=1).astype(y_ref.dtype)
    project(0)


def _gate_tables(p_ref, cs_ref):
    ts = p_ref.shape[0]
    gates = p_ref[:, OFF_GATES:OFF_GATES + LANES]
    lane = lax.broadcasted_iota(jnp.int32, (ts, LANES), 1)
    is_i = lane < LANE_ML_F
    is_f = (lane >= LANE_ML_F) & (lane < LANE_ML_F + ML_HEADS)
    is_a = (lane >= LANE_GD_A) & (lane < LANE_GD_A + GD_HEADS)
    x = jnp.where(is_f | is_a, gates, 0.0)
    hi, mid, lo = _split3(x)
    packed = (hi.astype(F32) + pltpu.roll(mid.astype(F32), GATE_LANES, 1)
              + pltpu.roll(lo.astype(F32), 2 * GATE_LANES, 1))
    cs = jnp.dot(cs_ref[...], packed.astype(BF16), preferred_element_type=F32)
    cs = cs + pltpu.roll(cs, LANES - GATE_LANES, 1) + pltpu.roll(cs, LANES - 2 * GATE_LANES, 1)
    csum_gd = cs[:ts]
    csum_ml = cs[ts:]
    z = jnp.where(is_i, gates, jnp.where(is_f, csum_ml, jnp.where(is_a, csum_gd, 0.0)))
    zs = jnp.concatenate(_split3(z), axis=0)
    sel = (lax.broadcasted_iota(jnp.int32, (GATE_LANES, LANES), 0)
           == lax.broadcasted_iota(jnp.int32, (GATE_LANES, LANES), 1)).astype(BF16)
    zt = lax.dot_general(sel, zs, (((1,), (1,)), ((), ())), preferred_element_type=F32)
    rows = zt[:, :ts] + zt[:, ts:2 * ts] + zt[:, 2 * ts:]
    return gates, csum_gd, csum_ml, rows


def _mlstm(p_ref, mlnw_ref, y_ref, ml_c, ml_m, gates, csum_ml, rows, project):
    ts = p_ref.shape[0]
    L = PAIR
    r_i = lax.broadcasted_iota(jnp.int32, (L, L), 0)
    c_i = lax.broadcasted_iota(jnp.int32, (L, L), 1)
    causal = r_i >= c_i
    ones_blk = jnp.ones((L, ML_DH), F32)
    units = [(h, c) for c in range(ts // L) for h in range(ML_HEADS)]

    st = {}
    for h, c in units:
        rs = slice(c * L, (c + 1) * L)
        q = p_ref[rs, OFF_ML_Q + h * ML_DH:OFF_ML_Q + (h + 1) * ML_DH]
        k = p_ref[rs, OFF_ML_K + h * ML_DH:OFF_ML_K + (h + 1) * ML_DH]
        li_c = gates[rs, LANE_ML_I + h:LANE_ML_I + h + 1]
        bq_c = csum_ml[rs, LANE_ML_F + h:LANE_ML_F + h + 1]
        r_row = (rows[LANE_ML_I + h:LANE_ML_I + h + 1, rs]
                 - rows[LANE_ML_F + h:LANE_ML_F + h + 1, rs])
        dm = jnp.where(causal, bq_c + r_row, -1e30)
        mx = jnp.max(dm, axis=-1, keepdims=True)
        g_last = bq_c[L - 1:L, :]
        ws = g_last - bq_c + li_c
        mws = jnp.max(ws, axis=0, keepdims=True)
        st[h, c] = dict(q=q, k=k, bq_c=bq_c, mx=mx, g_last=g_last, mws=mws,
                        pm=jnp.exp(dm - mx), wk=jnp.exp(ws - mws), qk=_dot_nt(q, k))
    yield
    for h, c in units:
        u = st[h, c]
        rs = slice(c * L, (c + 1) * L)
        v = p_ref[rs, OFF_ML_V + h * ML_DH:OFF_ML_V + (h + 1) * ML_DH]
        v_aug = jnp.concatenate([v, ones_blk], axis=1)
        u["intra"] = _dot(u["qk"] * u["pm"], v_aug)
        u["kv"] = _dot_tn(u["k"] * u["wk"], v_aug)

    yield
    c_aug = [ml_c[h] for h in range(ML_HEADS)]
    m_prev = [ml_m[h][0:1, 0:1] for h in range(ML_HEADS)]
    outs = [[] for _ in range(ML_HEADS)]
    for c in range(ts // L):
        qc = [_dot(st[h, c]["q"], c_aug[h]) for h in range(ML_HEADS)]
        for h in range(ML_HEADS):
            u = st[h, c]
            m_inter = u["bq_c"] + m_prev[h]
            m_t = jnp.maximum(m_inter, u["mx"])
            na = jnp.exp(u["mx"] - m_t) * u["intra"] + jnp.exp(m_inter - m_t) * qc[h]
            outs[h].append(na[:, :ML_DH] / jnp.maximum(jnp.abs(na[:, ML_DH:]), jnp.exp(-m_t)))
            m_new = jnp.maximum(u["g_last"] + m_prev[h], u["mws"])
            c_aug[h] = (jnp.exp(u["g_last"] + m_prev[h] - m_new) * c_aug[h]
                        + jnp.exp(u["mws"] - m_new) * u["kv"])
            m_prev[h] = m_new
        yield
    for h in range(ML_HEADS):
        sl = slice(h * ML_DH, (h + 1) * ML_DH)
        ml_c[h] = c_aug[h]
        ml_m[h] = jnp.broadcast_to(m_prev[h], ml_m.shape[1:])
        hh = jnp.concatenate(outs[h], axis=0)
        hn = hh * lax.rsqrt(jnp.mean(hh * hh, axis=-1, keepdims=True) + NORM_EPS)
        y_ml = (hn * mlnw_ref[0:1, sl]) * p_ref[:, OFF_ML_G + h * ML_DH:OFF_ML_G + (h + 1) * ML_DH]
        y_ref[:, RG_WIDTH + h * ML_DH:RG_WIDTH + (h + 1) * ML_DH] = y_ml.astype(y_ref.dtype)
    project(RG_WIDTH)


def _gdn(p_ref, gdo_ref, gdnw_ref, y_ref, gd_s, gates, csum_gd, rows, project):
    ts = p_ref.shape[0]
    L = PAIR
    C = GD_CHUNK
    r_i = lax.broadcasted_iota(jnp.int32, (L, L), 0)
    c_i = lax.broadcasted_iota(jnp.int32, (L, L), 1)
    same = (r_i >= C) == (c_i >= C)
    incl = same & (r_i >= c_i)
    strict = same & (r_i > c_i)
    eye = (r_i == c_i).astype(F32)
    second = lax.broadcasted_iota(jnp.int32, (L, 1), 0) >= C
    zeros_c = jnp.zeros((C, GD_DV), F32)
    units = [(h, j) for j in range(ts // L) for h in range(GD_HEADS)]

    st = {}
    for h, j in units:
        rs = slice(j * L, (j + 1) * L)
        q = gdo_ref[h, rs, :]
        k = gdo_ref[GD_HEADS + h, rs, :]
        beta = gates[rs, LANE_GD_B + h:LANE_GD_B + h + 1]
        gc_c = csum_gd[rs, LANE_GD_A + h:LANE_GD_A + h + 1]
        gc_r = rows[LANE_GD_A + h:LANE_GD_A + h + 1, rs]
        gam = jnp.where(incl, jnp.exp(jnp.where(incl, gc_c - gc_r, 0.0)), 0.0)
        kb = k * beta
        egc = jnp.exp(gc_c)
        g_first = gc_c[C - 1:C, :]
        g_second = gc_c[L - 1:L, :]
        st[h, j] = dict(k=k, kb=kb, beta=beta, gam=gam, egc=egc, q_dec=q * egc,
                        g_last=(g_first, g_second),
                        k_dec=k * jnp.exp(jnp.where(second, g_second, g_first) - gc_c),
                        kq=_dot_nt(jnp.concatenate([kb, q], axis=0), k))
    yield
    for key in units:
        u = st[key]
        u["aqk"] = u["kq"][L:] * u["gam"]
        pw = jnp.where(strict, -(u["kq"][:L] * u["gam"]), 0.0)
        u["t_inv"] = eye + pw
        u["pw"] = _dot(pw, pw)
    yield
    span = 4
    while span < C:
        for key in units:
            u = st[key]
            both = _dot(jnp.concatenate([u["pw"], u["t_inv"]], axis=0), u["pw"])
            u["pw"] = both[:L]
            u["t_inv"] = u["t_inv"] + both[L:]
        span *= 2
        yield
    for key in units:
        u = st[key]
        u["t_inv"] = u["t_inv"] + _dot(u["t_inv"], u["pw"])
    yield
    for h, j in units:
        u = st[h, j]
        rs = slice(j * L, (j + 1) * L)
        v = gdo_ref[2 * GD_HEADS + h, rs, :]
        uw = _dot(u["t_inv"], jnp.concatenate([v * u["beta"], u["kb"] * u["egc"]], axis=1))
        u["u"] = uw[:, :GD_DV]
        u["w"] = uw[:, GD_DV:]

    yield
    state = [gd_s[h] for h in range(GD_HEADS)]
    outs = [[] for _ in range(GD_HEADS)]
    for j in range(ts // L):
        for half in range(2):
            hs = slice(half * C, (half + 1) * C)
            wq = [_dot(jnp.concatenate([st[h, j]["w"][hs], st[h, j]["q_dec"][hs]], axis=0), state[h])
                  for h in range(GD_HEADS)]
            v_new = [st[h, j]["u"][hs] - wq[h][:C] for h in range(GD_HEADS)]
            for h in range(GD_HEADS):
                v_pad = jnp.concatenate([v_new[h], zeros_c] if half == 0 else [zeros_c, v_new[h]],
                                        axis=0)
                outs[h].append(wq[h][C:] + _dot(st[h, j]["aqk"][hs], v_pad))
            for h in range(GD_HEADS):
                u = st[h, j]
                state[h] = (state[h] * jnp.exp(u["g_last"][half])
                            + _dot_tn(u["k_dec"][hs], v_new[h]))
            yield
    for h in range(GD_HEADS):
        gd_s[h] = state[h]
        o = jnp.concatenate(outs[h], axis=0)
        on = o * lax.rsqrt(jnp.mean(o * o, axis=-1, keepdims=True) + NORM_EPS)
        y_gd = on * gdnw_ref[0:1, :] * p_ref[:, OFF_GD_G + h * GD_DV:OFF_GD_G + (h + 1) * GD_DV]
        y_ref[:, RG_WIDTH + ML_WIDTH + h * GD_DV:RG_WIDTH + ML_WIDTH + (h + 1) * GD_DV] = (
            y_gd.astype(y_ref.dtype))
    project(RG_WIDTH + ML_WIDTH)


def _interleave(order, stages):
    live = dict(stages)

    def step(key):
        if key in live:
            try:
                next(live[key])
            except StopIteration:
                del live[key]

    for key in order:
        step(key)
    while live:
        for key in list(live):
            step(key)


def _layer_kernel(xn_ref, xc_ref, nw_ref, wa_ref, wb_ref, rgp_ref, rgw_ref, gp_ref, gdcw_ref,
                  wo_ref, fw_ref, mlnw_ref, gdnw_ref, cs_ref, o_ref,
                  p_s, gdo_s, rgo_s, yrg_s, hn_s, raw_s, raw_z, xc_s, rg_tail, gd_tail,
                  y_ref, rg_h, ml_c, ml_m, gd_s,
                  *, blocks_per_seq, final_norm):
    i = pl.program_id(0)
    slot = lax.rem(i, 2)
    p_new = p_s.at[slot]
    p_cur = p_s.at[1 - slot]

    @pl.when(i == 0)
    def _():
        p_s[1] = jnp.zeros(p_s.shape[1:], p_s.dtype)
        gdo_s[1] = jnp.zeros(gdo_s.shape[1:], gdo_s.dtype)
        rgo_s[1] = jnp.zeros(rgo_s.shape[1:], rgo_s.dtype)

    @pl.when(lax.rem(i, blocks_per_seq) == 0)
    def _():
        rg_tail[...] = jnp.zeros_like(rg_tail)
        gd_tail[...] = jnp.zeros_like(gd_tail)

    @pl.when(lax.rem(jnp.maximum(i - 1, 0), blocks_per_seq) == 0)
    def _():
        rg_h[...] = jnp.zeros_like(rg_h)
        ml_c[...] = jnp.zeros_like(ml_c)
        ml_m[...] = jnp.zeros_like(ml_m)
        gd_s[...] = jnp.zeros_like(gd_s)

    o_ref[...] = xc_ref[...]

    def project(k0):
        o_ref[...] += jnp.dot(y_ref[:, k0:k0 + W], wo_ref[k0:k0 + W, :],
                              preferred_element_type=F32)

    gates, csum_gd, csum_ml, rows = _gate_tables(p_cur, cs_ref)
    _interleave(LAYER_ORDER, dict(
        P=_inproj_stream(xn_ref, nw_ref, wa_ref, wb_ref, rgp_ref, rgw_ref, gp_ref, gdcw_ref,
                         p_new, gdo_s.at[slot], rgo_s.at[slot], hn_s, raw_s, raw_z, xc_s,
                         rg_tail, gd_tail),
        G=_gdn(p_cur, gdo_s.at[1 - slot], gdnw_ref, y_ref, gd_s, gates, csum_gd, rows, project),
        M=_mlstm(p_cur, mlnw_ref, y_ref, ml_c, ml_m, gates, csum_ml, rows, project),
        R=_rglru(rgo_s.at[1 - slot], yrg_s, y_ref, rg_h, project)))
    if final_norm:
        xn = o_ref[...]
        r = lax.rsqrt(jnp.mean(xn * xn, axis=-1, keepdims=True) + NORM_EPS)
        o_ref[...] = (xn * r) * fw_ref[...]


def _cumsum_matrix(ts):
    r = jnp.arange(ts)[:, None]
    c = jnp.arange(ts)[None, :]
    m64 = (r >= c) & (r // GD_CHUNK == c // GD_CHUNK)
    m128 = (r >= c) & (r // PAIR == c // PAIR)
    return jnp.concatenate([m64, m128], axis=0).astype(BF16)


def _layer(x2, norm_w, w_a, w_b, rgp, rgw, gp, gdcw, w_out, final_w, mlnw, gdnw, cs, layer, seq,
           final_norm):
    t = x2.shape[0]
    ts = TS_BLOCK
    nb = t // ts
    per_layer = lambda a, **kw: pl.BlockSpec((None,) + a.shape[1:],
                                             lambda i: (layer,) + (0,) * (a.ndim - 1), **kw)
    once = dict(pipeline_mode=pl.Buffered(1))
    return pl.pallas_call(
        functools.partial(_layer_kernel, blocks_per_seq=seq // ts, final_norm=final_norm),
        grid=(nb + 1,),
        in_specs=[pl.BlockSpec((ts, D_MODEL), lambda i: (jnp.minimum(i, nb - 1), 0)),
                  pl.BlockSpec((ts, D_MODEL), lambda i: (jnp.maximum(i - 1, 0), 0)),
                  per_layer(norm_w),
                  pl.BlockSpec((None, D_MODEL, WCOL_SPLIT), lambda i: (layer, 0, 0), **once),
                  per_layer(w_b, **once), per_layer(rgp), per_layer(rgw), per_layer(gp),
                  per_layer(gdcw), per_layer(w_out, **once),
                  pl.BlockSpec(final_w.shape, lambda i: (0, 0)),
                  per_layer(mlnw), per_layer(gdnw), pl.BlockSpec(cs.shape, lambda i: (0, 0))],
        out_specs=pl.BlockSpec((ts, D_MODEL), lambda i: (jnp.maximum(i - 1, 0), 0)),
        out_shape=jax.ShapeDtypeStruct((t, D_MODEL), F32),
        scratch_shapes=[pltpu.VMEM((2, ts, D_PROJ), F32),
                        pltpu.VMEM((2, GD_CONV // LANES, ts, LANES), F32),
                        pltpu.VMEM((2, 3 * RG_TILES, ts, LANES), F32),
                        pltpu.VMEM((RG_TILES, ts, LANES), F32),
                        pltpu.VMEM((ts, D_MODEL), BF16),
                        pltpu.VMEM((RAW_TILES, ts, LANES), F32),
                        pltpu.VMEM((ts, W), F32),
                        pltpu.VMEM((RG_WIDTH // LANES, ts, LANES), F32),
                        pltpu.VMEM((SUBLANES, RG_WIDTH), F32),
                        pltpu.VMEM((SUBLANES, GD_CONV), F32),
                        pltpu.VMEM((ts, D_MIX), BF16),
                        pltpu.VMEM((SUBLANES, RG_WIDTH), F32),
                        pltpu.VMEM((ML_HEADS, ML_DH, 2 * ML_DH), F32),
                        pltpu.VMEM((ML_HEADS, SUBLANES, LANES), F32),
                        pltpu.VMEM((GD_HEADS, GD_DK, GD_DV), F32)],
        compiler_params=pltpu.CompilerParams(dimension_semantics=("arbitrary",),
                                             vmem_limit_bytes=VMEM_LIMIT),
        name="layer",
    )(x2, x2, norm_w, w_a, w_b, rgp, rgw, gp, gdcw, w_out, final_w, mlnw, gdnw, cs)


def _w_in_tail(w_in):
    pad = jnp.zeros(w_in.shape[:-1] + (LANES - GATE_LANES,), w_in.dtype)
    return jnp.concatenate([
        w_in[..., SRC_GD:SRC_GD_GATES], w_in[..., SRC_ML_GATES:SRC_GD],
        w_in[..., SRC_GD_GATES:], pad], axis=-1)


def _rg_gate_blockdiag(gate_w):
    depth = gate_w.shape[0]
    nb = RG_BLOCKS // 2
    eye = jnp.eye(nb, dtype=gate_w.dtype)
    halves = []
    for hf in range(2):
        blk = gate_w[:, :, hf * nb:(hf + 1) * nb]
        bd = jnp.einsum('lgncd,nm->lgncmd', blk, eye)
        bd = bd.reshape(depth, 2, nb * RG_BLOCK, nb * RG_BLOCK)
        halves.append(jnp.concatenate([bd[:, 0], bd[:, 1]], axis=-1))
    return jnp.stack(halves, axis=1)


def _gate_params(ml_gate_b, gd_dt_bias, gd_a_log):
    depth = ml_gate_b.shape[0]
    zeros = lambda n: jnp.zeros((depth, n), F32)
    row0 = jnp.concatenate([ml_gate_b[:, 0], ml_gate_b[:, 1], gd_dt_bias,
                            zeros(LANES - LANE_GD_B)], axis=-1)
    row1 = jnp.concatenate([zeros(LANE_GD_A), gd_a_log, zeros(LANES - LANE_GD_B)], axis=-1)
    rest = jnp.zeros((depth, SUBLANES - 2, LANES), F32)
    return jnp.concatenate([row0[:, None], row1[:, None], rest], axis=1)


def kernel(x, norm_w, w_in, rg_conv_w, rg_conv_b, rg_gate_w, rg_gate_b, rg_lambda, ml_gate_b,
           ml_norm_w, gd_conv_w, gd_a_log, gd_dt_bias, gd_norm_w, w_out, final_norm_w):
    bsz, seq, _ = x.shape
    depth = w_in.shape[0]
    t = bsz * seq
    w_a = w_in.astype(BF16)
    w_b = _w_in_tail(w_in).astype(BF16)
    w_o = w_out.astype(BF16)
    rgp = jnp.concatenate([rg_conv_w, rg_conv_b[:, None], rg_gate_b, rg_lambda[:, None]],
                          axis=1)
    rgw = _rg_gate_blockdiag(rg_gate_w).astype(BF16)
    gp = _gate_params(ml_gate_b, gd_dt_bias, gd_a_log)
    cs = _cumsum_matrix(TS_BLOCK)

    x2 = x.reshape(t, D_MODEL)
    for l in range(depth):
        x2 = _layer(x2, norm_w[:, None], w_a, w_b, rgp, rgw, gp, gd_conv_w, w_o, final_norm_w[None],
                    ml_norm_w[:, None], gd_norm_w[:, None], cs, l, seq, l == depth - 1)
    return x2.reshape(bsz, seq, D_MODEL)
```

```python
import functools

import jax
import jax.numpy as jnp
from jax import lax
from jax.experimental import pallas as pl
from jax.experimental.pallas import tpu as pltpu

F32 = jnp.float32
BF16 = jnp.bfloat16

D_MODEL = 1024
CONV_K = 4
CONV_CLASSES = 4
NORM_EPS = 1e-6
RG_WIDTH = 512
RG_BLOCKS = 8
RG_BLOCK = RG_WIDTH // RG_BLOCKS
RG_C = 8.0
ML_HEADS = 4
ML_DH = 128
ML_WIDTH = ML_HEADS * ML_DH
GD_HEADS = 4
GD_DK = 128
GD_DV = 128
GD_QK = GD_HEADS * GD_DK
GD_WIDTH = GD_HEADS * GD_DV
GD_CONV = 2 * GD_QK + GD_WIDTH
D_MIX = RG_WIDTH + ML_WIDTH + GD_WIDTH
GD_CHUNK = 64
PAIR = 2 * GD_CHUNK
SUBLANES = 8
LANES = 128

W = 512
WCOL_RG_X, WCOL_RG_Z = 0 * W, 1 * W
WCOL_ML_Q, WCOL_ML_K, WCOL_ML_V, WCOL_ML_O, WCOL_ML_Z = 2 * W, 3 * W, 4 * W, 5 * W, 6 * W
WCOL_GD_QKV, WCOL_GD_Z = 7 * W, 10 * W
WCOL_GATES = 11 * W
WCOL_SPLIT = WCOL_GD_QKV
D_PROJ = WCOL_GATES + LANES
OFF_RG_A, OFF_RG_B, OFF_RG_G = 0 * W, 1 * W, 2 * W
OFF_ML_Q, OFF_ML_K, OFF_ML_V, OFF_ML_G = 3 * W, 4 * W, 5 * W, 6 * W
OFF_GD_Q, OFF_GD_K, OFF_GD_V, OFF_GD_G = 7 * W, 8 * W, 9 * W, 10 * W
OFF_GATES = 11 * W
LANE_ML_I, LANE_ML_F, LANE_GD_A, LANE_GD_B = 0, 4, 8, 12
GATE_LANES = 16
RAW_RG, RAW_GD = 0, RG_WIDTH // LANES
RAW_TILES = RAW_GD + GD_CONV // LANES
RG_TILES = RG_WIDTH // LANES
RGT_A, RGT_B, RGT_G = 0, RG_TILES, 2 * RG_TILES
RG_CLASSES = 4
SRC_ML_GATES = 2 * RG_WIDTH + 5 * ML_WIDTH
SRC_GD = SRC_ML_GATES + 2 * ML_HEADS
SRC_GD_GATES = SRC_GD + 2 * GD_QK + 2 * GD_WIDTH

VMEM_LIMIT = 56 * 1024 * 1024
TS_BLOCK = 256
LAYER_ORDER = "GPP" * 9 + "MRPG" * 5


def _dot(a, b):
    return jnp.dot(a.astype(BF16), b.astype(BF16), preferred_element_type=F32)


def _dot_nt(a, b):
    return lax.dot_general(a.astype(BF16), b.astype(BF16), (((1,), (1,)), ((), ())),
                           preferred_element_type=F32)


def _dot_tn(a, b):
    return lax.dot_general(a.astype(BF16), b.astype(BF16), (((0,), (0,)), ((), ())),
                           preferred_element_type=F32)


def _split3(x):
    hi = x.astype(BF16)
    r1 = x - hi.astype(F32)
    mid = r1.astype(BF16)
    lo = (r1 - mid.astype(F32)).astype(BF16)
    return hi, mid, lo


def _softplus(x):
    return jnp.maximum(x, 0.0) + jnp.log1p(jnp.exp(-jnp.abs(x)))


NEG_LOG2_E = -1.4426950408889634


def _sigmoid(x):
    return 1.0 / (1.0 + jnp.exp2(x * NEG_LOG2_E))


def _silu(x):
    return x * _sigmoid(x)


def _causal_conv(x_ref, tile, tail_ref, tcol0, cw_ref, out_ref, out_tile, post):
    ts = x_ref.shape[1]
    nc = CONV_CLASSES
    n = ts // nc
    tcols = slice(tcol0, tcol0 + LANES)
    tail = tail_ref[:, tcols]
    new_tail = x_ref[tile, ts - SUBLANES:ts, :]
    cls = [x_ref[tile, pl.ds(r, n, stride=nc), :] for r in range(nc)]
    first = lax.broadcasted_iota(jnp.int32, (n, LANES), 0) == 0
    prev = {r: jnp.where(first, tail[SUBLANES - nc + r:SUBLANES - nc + r + 1, :],
                         pltpu.roll(cls[r], 1, 0))
            for r in range(nc - CONV_K + 1, nc)}
    for r in range(nc):
        acc = None
        for k in range(CONV_K):
            j = CONV_K - 1 - k
            tap = cls[r - j] if r >= j else prev[r - j + nc]
            term = tap * cw_ref[k:k + 1, tcols]
            acc = term if acc is None else acc + term
        out_ref[out_tile, pl.ds(r, n, stride=nc), :] = post(acc)
    tail_ref[:, tcols] = new_tail


def _inproj_stream(x_ref, nw_ref, wa_ref, wb_ref, rgp_ref, rgw_ref, gp_ref, gdcw_ref, p_ref,
                   gdo_ref, rgo_ref, hn_s, raw_s, raw_z, xc_s, rg_tail, gd_tail):
    x = x_ref[...]
    r = lax.rsqrt(jnp.mean(x * x, axis=-1, keepdims=True) + NORM_EPS)
    hn_s[...] = ((x * r) * nw_ref[...]).astype(BF16)

    H = W // 2

    def proj(col, width):
        if col < WCOL_SPLIT:
            w = wa_ref[:, col:col + width]
        else:
            w = wb_ref[:, col - WCOL_SPLIT:col - WCOL_SPLIT + width]
        return jnp.dot(hn_s[...], w, preferred_element_type=F32)

    def mm(dst_ref, dst, src, scale=None):
        def run():
            y = proj(src, H)
            dst_ref[:, dst:dst + H] = y if scale is None else y * scale
        return run

    def mm_raw(tile0, src):
        def run():
            y = proj(src, H)
            for i in range(H // LANES):
                raw_s[tile0 + i] = y[:, i * LANES:(i + 1) * LANES]
        return run

    def xc_half(c):
        return jnp.concatenate([xc_s[c // LANES + i] for i in range(H // LANES)], axis=1)

    def mm_rg_gates(c):
        def run():
            g = jnp.dot(xc_half(c).astype(BF16), rgw_ref[c // H],
                        preferred_element_type=F32)
            p_ref[:, OFF_RG_B + c:OFF_RG_B + c + H] = g[:, :H]
            raw_z[:, c:c + H] = g[:, H:]
        return run

    def ep_rg_conv(c):
        def run():
            for t in range(c // LANES, (c + H) // LANES):
                bias = rgp_ref[4:5, t * LANES:(t + 1) * LANES]
                _causal_conv(raw_s, RAW_RG + t, rg_tail, t * LANES, rgp_ref, xc_s, t,
                             lambda y, bias=bias: y + bias)
        return run

    def ep_rg_ab(c):
        def run():
            gr = p_ref[:, OFF_RG_B + c:OFF_RG_B + c + H] + rgp_ref[5:6, c:c + H]
            gi = raw_z[:, c:c + H] + rgp_ref[6:7, c:c + H]
            log_a = (-RG_C * _sigmoid(gr)) * _softplus(-rgp_ref[7:8, c:c + H])
            a = jnp.exp(log_a)
            b = jnp.sqrt(-jnp.tanh(log_a) * (a * a + 1.0)) * (_sigmoid(gi) * xc_half(c))
            for i in range(H // LANES):
                rgo_ref[RGT_A + c // LANES + i] = a[:, i * LANES:(i + 1) * LANES]
                rgo_ref[RGT_B + c // LANES + i] = b[:, i * LANES:(i + 1) * LANES]
        return run

    def ep_rg_gate(c):
        def run():
            g = _silu(p_ref[:, OFF_RG_G + c:OFF_RG_G + c + H])
            for i in range(H // LANES):
                rgo_ref[RGT_G + c // LANES + i] = g[:, i * LANES:(i + 1) * LANES]
        return run

    def ep_silu(off):
        def run():
            p_ref[:, off:off + H] = _silu(p_ref[:, off:off + H])
        return run

    def ep_gd_conv(tcol, norm_scale):
        def post(y):
            y = _silu(y)
            if norm_scale is None:
                return y
            y = y * lax.rsqrt(jnp.sum(y * y, axis=-1, keepdims=True) + NORM_EPS)
            return y if norm_scale == 1.0 else y * norm_scale

        def run():
            for t in range(tcol // LANES, (tcol + H) // LANES):
                _causal_conv(raw_s, RAW_GD + t, gd_tail, t * LANES, gdcw_ref, gdo_ref, t, post)
        return run

    def ep_ml_gate(c):
        def run():
            p_ref[:, OFF_ML_G + c:OFF_ML_G + c + H] = (
                _sigmoid(p_ref[:, OFF_ML_G + c:OFF_ML_G + c + H]) * _silu(raw_z[:, c:c + H]))
        return run

    def mm_gates():
        p_ref[:, OFF_GATES:OFF_GATES + LANES] = proj(WCOL_GATES, LANES)

    def ep_gates():
        gates = p_ref[:, OFF_GATES:OFF_GATES + LANES] + gp_ref[0:1, :]
        lane = lax.broadcasted_iota(jnp.int32, gates.shape, 1)
        is_i = lane < LANE_ML_F
        is_f = (lane >= LANE_ML_F) & (lane < LANE_ML_F + ML_HEADS)
        is_a = (lane >= LANE_GD_A) & (lane < LANE_GD_A + GD_HEADS)
        log_f = -_softplus(-gates)
        g_gd = -jnp.exp(gp_ref[1:2, :]) * _softplus(gates)
        p_ref[:, OFF_GATES:OFF_GATES + LANES] = jnp.where(
            is_i, gates, jnp.where(is_f, log_f, jnp.where(is_a, g_gd, _sigmoid(gates))))

    units = []
    for c in range(0, W, H):
        units += [
            ([mm_raw(RAW_RG + c // LANES, WCOL_RG_X + c)], ep_rg_conv(c)),
            ([mm_raw(RAW_GD + c // LANES, WCOL_GD_QKV + c)], ep_gd_conv(c, GD_DK ** -0.5)),
            ([mm(p_ref, OFF_ML_Q + c, WCOL_ML_Q + c, ML_DH ** -0.5)], None),
            ([mm_rg_gates(c)], ep_rg_ab(c)),
            ([mm_raw(RAW_GD + (W + c) // LANES, WCOL_GD_QKV + W + c)], ep_gd_conv(W + c, 1.0)),
            ([mm(p_ref, OFF_ML_K + c, WCOL_ML_K + c)], None),
            ([mm_raw(RAW_GD + (2 * W + c) // LANES, WCOL_GD_QKV + 2 * W + c)],
             ep_gd_conv(2 * W + c, None)),
            ([mm(p_ref, OFF_ML_V + c, WCOL_ML_V + c)], None),
            ([mm(p_ref, OFF_RG_G + c, WCOL_RG_Z + c)], ep_rg_gate(c)),
            ([mm(p_ref, OFF_ML_G + c, WCOL_ML_O + c), mm(raw_z, c, WCOL_ML_Z + c)], ep_ml_gate(c)),
            ([mm(p_ref, OFF_GD_G + c, WCOL_GD_Z + c)], ep_silu(OFF_GD_G + c)),
        ]
    units.append(([mm_gates], ep_gates))
    pending = None
    for mms, ep in units:
        for run in mms:
            run()
        if pending is not None:
            pending()
        pending = ep
        yield
    if pending is not None:
        pending()


def _rglru(rgo_ref, yrg_s, y_ref, rg_h, project):
    nc = RG_CLASSES
    ts = rgo_ref.shape[1]
    n = ts // nc
    ng = n // SUBLANES
    tiles = range(RG_TILES)
    ld = lambda t, r: rgo_ref[t, pl.ds(r, n, stride=nc), :]
    pa, pb = {}, {}
    for lt in tiles:
        pa[lt, 0] = ld(RGT_A + lt, 0)
        pb[lt, 0] = ld(RGT_B + lt, 0)
        for r in range(1, nc):
            a = ld(RGT_A + lt, r)
            pb[lt, r] = a * pb[lt, r - 1] + ld(RGT_B + lt, r)
            pa[lt, r] = a * pa[lt, r - 1]
    yield
    sub = lax.broadcasted_iota(jnp.int32, (ng, SUBLANES, LANES), 1)
    a3 = {lt: pa[lt, nc - 1].reshape(ng, SUBLANES, LANES) for lt in tiles}
    b3 = {lt: pb[lt, nc - 1].reshape(ng, SUBLANES, LANES) for lt in tiles}
    d = 1
    while d < SUBLANES:
        for lt in tiles:
            a_sh = pltpu.roll(a3[lt], d, 1)
            b_sh = pltpu.roll(b3[lt], d, 1)
            valid = sub >= d
            b3[lt] = jnp.where(valid, a3[lt] * b_sh + b3[lt], b3[lt])
            a3[lt] = jnp.where(valid, a3[lt] * a_sh, a3[lt])
        d *= 2
    yield
    first = lax.broadcasted_iota(jnp.int32, (n, LANES), 0) == 0
    for lt in tiles:
        cols = slice(lt * LANES, (lt + 1) * LANES)
        carry_in = rg_h[0:1, cols]
        carry = carry_in
        groups = []
        for g in range(ng):
            hg = b3[lt][g] + a3[lt][g] * carry
            groups.append(hg)
            carry = hg[SUBLANES - 1:SUBLANES, :]
        rg_h[:, cols] = jnp.broadcast_to(carry, (SUBLANES, LANES))
        h_last = jnp.concatenate(groups, axis=0)
        h_in = jnp.where(first, carry_in, pltpu.roll(h_last, 1, 0))
        for r in range(nc):
            h = h_last if r == nc - 1 else pb[lt, r] + pa[lt, r] * h_in
            yrg_s[lt, pl.ds(r, n, stride=nc), :] = h * ld(RGT_G + lt, r)
    yield
    y_ref[:, 0:RG_WIDTH] = jnp.concatenate([yrg_s[lt] for lt in tiles], axis=1).astype(y_ref.dtype)
    project(0)


def _gate_tables(p_ref, cs_ref):
    ts = p_ref.shape[0]
    gates = p_ref[:, OFF_GATES:OFF_GATES + LANES]
    lane = lax.broadcasted_iota(jnp.int32, (ts, LANES), 1)
    is_i = lane < LANE_ML_F
    is_f = (lane >= LANE_ML_F) & (lane < LANE_ML_F + ML_HEADS)
    is_a = (lane >= LANE_GD_A) & (lane < LANE_GD_A + GD_HEADS)
    x = jnp.where(is_f | is_a, gates, 0.0)
    hi, mid, lo = _split3(x)
    packed = (hi.astype(F32) + pltpu.roll(mid.astype(F32), GATE_LANES, 1)
              + pltpu.roll(lo.astype(F32), 2 * GATE_LANES, 1))
    cs = jnp.dot(cs_ref[...], packed.astype(BF16), preferred_element_type=F32)
    cs = cs + pltpu.roll(cs, LANES - GATE_LANES, 1) + pltpu.roll(cs, LANES - 2 * GATE_LANES, 1)
    csum_gd = cs[:ts]
    csum_ml = cs[ts:]
    z = jnp.where(is_i, gates, jnp.where(is_f, csum_ml, jnp.where(is_a, csum_gd, 0.0)))
    zs = jnp.concatenate(_split3(z), axis=0)
    sel = (lax.broadcasted_iota(jnp.int32, (GATE_LANES, LANES), 0)
           == lax.broadcasted_iota(jnp.int32, (GATE_LANES, LANES), 1)).astype(BF16)
    zt = lax.dot_general(sel, zs, (((1,), (1,)), ((), ())), preferred_element_type=F32)
    rows = zt[:, :ts] + zt[:, ts:2 * ts] + zt[:, 2 * ts:]
    return gates, csum_gd, csum_ml, rows


def _mlstm(p_ref, mlnw_ref, y_ref, ml_c, ml_m, gates, csum_ml, rows, project):
    ts = p_ref.shape[0]
    L = PAIR
    r_i = lax.broadcasted_iota(jnp.int32, (L, L), 0)
    c_i = lax.broadcasted_iota(jnp.int32, (L, L), 1)
    causal = r_i >= c_i
    ones_blk = jnp.ones((L, ML_DH), F32)
    units = [(h, c) for c in range(ts // L) for h in range(ML_HEADS)]

    st = {}
    for h, c in units:
        rs = slice(c * L, (c + 1) * L)
        q = p_ref[rs, OFF_ML_Q + h * ML_DH:OFF_ML_Q + (h + 1) * ML_DH]
        k = p_ref[rs, OFF_ML_K + h * ML_DH:OFF_ML_K + (h + 1) * ML_DH]
        li_c = gates[rs, LANE_ML_I + h:LANE_ML_I + h + 1]
        bq_c = csum_ml[rs, LANE_ML_F + h:LANE_ML_F + h + 1]
        r_row = (rows[LANE_ML_I + h:LANE_ML_I + h + 1, rs]
                 - rows[LANE_ML_F + h:LANE_ML_F + h + 1, rs])
        dm = jnp.where(causal, bq_c + r_row, -1e30)
        mx = jnp.max(dm, axis=-1, keepdims=True)
        g_last = bq_c[L - 1:L, :]
        ws = g_last - bq_c + li_c
        mws = jnp.max(ws, axis=0, keepdims=True)
        st[h, c] = dict(q=q, k=k, bq_c=bq_c, mx=mx, g_last=g_last, mws=mws,
                        pm=jnp.exp(dm - mx), wk=jnp.exp(ws - mws), qk=_dot_nt(q, k))
    yield
    for h, c in units:
        u = st[h, c]
        rs = slice(c * L, (c + 1) * L)
        v = p_ref[rs, OFF_ML_V + h * ML_DH:OFF_ML_V + (h + 1) * ML_DH]
        v_aug = jnp.concatenate([v, ones_blk], axis=1)
        u["intra"] = _dot(u["qk"] * u["pm"], v_aug)
        u["kv"] = _dot_tn(u["k"] * u["wk"], v_aug)

    yield
    c_aug = [ml_c[h] for h in range(ML_HEADS)]
    m_prev = [ml_m[h][0:1, 0:1] for h in range(ML_HEADS)]
    outs = [[] for _ in range(ML_HEADS)]
    for c in range(ts // L):
        qc = [_dot(st[h, c]["q"], c_aug[h]) for h in range(ML_HEADS)]
        for h in range(ML_HEADS):
            u = st[h, c]
            m_inter = u["bq_c"] + m_prev[h]
            m_t = jnp.maximum(m_inter, u["mx"])
            na = jnp.exp(u["mx"] - m_t) * u["intra"] + jnp.exp(m_inter - m_t) * qc[h]
            outs[h].append(na[:, :ML_DH] / jnp.maximum(jnp.abs(na[:, ML_DH:]), jnp.exp(-m_t)))
            m_new = jnp.maximum(u["g_last"] + m_prev[h], u["mws"])
            c_aug[h] = (jnp.exp(u["g_last"] + m_prev[h] - m_new) * c_aug[h]
                        + jnp.exp(u["mws"] - m_new) * u["kv"])
            m_prev[h] = m_new
        yield
    for h in range(ML_HEADS):
        sl = slice(h * ML_DH, (h + 1) * ML_DH)
        ml_c[h] = c_aug[h]
        ml_m[h] = jnp.broadcast_to(m_prev[h], ml_m.shape[1:])
        hh = jnp.concatenate(outs[h], axis=0)
        hn = hh * lax.rsqrt(jnp.mean(hh * hh, axis=-1, keepdims=True) + NORM_EPS)
        y_ml = (hn * mlnw_ref[0:1, sl]) * p_ref[:, OFF_ML_G + h * ML_DH:OFF_ML_G + (h + 1) * ML_DH]
        y_ref[:, RG_WIDTH + h * ML_DH:RG_WIDTH + (h + 1) * ML_DH] = y_ml.astype(y_ref.dtype)
    project(RG_WIDTH)


def _gdn(p_ref, gdo_ref, gdnw_ref, y_ref, gd_s, gates, csum_gd, rows, project):
    ts = p_ref.shape[0]
    L = PAIR
    C = GD_CHUNK
    r_i = lax.broadcasted_iota(jnp.int32, (L, L), 0)
    c_i = lax.broadcasted_iota(jnp.int32, (L, L), 1)
    same = (r_i >= C) == (c_i >= C)
    incl = same & (r_i >= c_i)
    strict = same & (r_i > c_i)
    eye = (r_i == c_i).astype(F32)
    second = lax.broadcasted_iota(jnp.int32, (L, 1), 0) >= C
    zeros_c = jnp.zeros((C, GD_DV), F32)
    units = [(h, j) for j in range(ts // L) for h in range(GD_HEADS)]

    st = {}
    for h, j in units:
        rs = slice(j * L, (j + 1) * L)
        q = gdo_ref[h, rs, :]
        k = gdo_ref[GD_HEADS + h, rs, :]
        beta = gates[rs, LANE_GD_B + h:LANE_GD_B + h + 1]
        gc_c = csum_gd[rs, LANE_GD_A + h:LANE_GD_A + h + 1]
        gc_r = rows[LANE_GD_A + h:LANE_GD_A + h + 1, rs]
        gam = jnp.where(incl, jnp.exp(jnp.where(incl, gc_c - gc_r, 0.0)), 0.0)
        kb = k * beta
        egc = jnp.exp(gc_c)
        g_first = gc_c[C - 1:C, :]
        g_second = gc_c[L - 1:L, :]
        st[h, j] = dict(k=k, kb=kb, beta=beta, gam=gam, egc=egc, q_dec=q * egc,
                        g_last=(g_first, g_second),
                        k_dec=k * jnp.exp(jnp.where(second, g_second, g_first) - gc_c),
                        kq=_dot_nt(jnp.concatenate([kb, q], axis=0), k))
    yield
    for key in units:
        u = st[key]
        u["aqk"] = u["kq"][L:] * u["gam"]
        pw = jnp.where(strict, -(u["kq"][:L] * u["gam"]), 0.0)
        u["t_inv"] = eye + pw
        u["pw"] = _dot(pw, pw)
    yield
    span = 4
    while span < C:
        for key in units:
            u = st[key]
            both = _dot(jnp.concatenate([u["pw"], u["t_inv"]], axis=0), u["pw"])
            u["pw"] = both[:L]
            u["t_inv"] = u["t_inv"] + both[L:]
        span *= 2
        yield
    for key in units:
        u = st[key]
        u["t_inv"] = u["t_inv"] + _dot(u["t_inv"], u["pw"])
    yield
    for h, j in units:
        u = st[h, j]
        rs = slice(j * L, (j + 1) * L)
        v = gdo_ref[2 * GD_HEADS + h, rs, :]
        uw = _dot(u["t_inv"], jnp.concatenate([v * u["beta"], u["kb"] * u["egc"]], axis=1))
        u["u"] = uw[:, :GD_DV]
        u["w"] = uw[:, GD_DV:]

    yield
    state = [gd_s[h] for h in range(GD_HEADS)]
    outs = [[] for _ in range(GD_HEADS)]
    for j in range(ts // L):
        for half in range(2):
            hs = slice(half * C, (half + 1) * C)
            wq = [_dot(jnp.concatenate([st[h, j]["w"][hs], st[h, j]["q_dec"][hs]], axis=0), state[h])
                  for h in range(GD_HEADS)]
            v_new = [st[h, j]["u"][hs] - wq[h][:C] for h in range(GD_HEADS)]
            for h in range(GD_HEADS):
                v_pad = jnp.concatenate([v_new[h], zeros_c] if half == 0 else [zeros_c, v_new[h]],
                                        axis=0)
                outs[h].append(wq[h][C:] + _dot(st[h, j]["aqk"][hs], v_pad))
            for h in range(GD_HEADS):
                u = st[h, j]
                state[h] = (state[h] * jnp.exp(u["g_last"][half])
                            + _dot_tn(u["k_dec"][hs], v_new[h]))
            yield
    for h in range(GD_HEADS):
        gd_s[h] = state[h]
        o = jnp.concatenate(outs[h], axis=0)
        on = o * lax.rsqrt(jnp.mean(o * o, axis=-1, keepdims=True) + NORM_EPS)
        y_gd = on * gdnw_ref[0:1, :] * p_ref[:, OFF_GD_G + h * GD_DV:OFF_GD_G + (h + 1) * GD_DV]
        y_ref[:, RG_WIDTH + ML_WIDTH + h * GD_DV:RG_WIDTH + ML_WIDTH + (h + 1) * GD_DV] = (
            y_gd.astype(y_ref.dtype))
    project(RG_WIDTH + ML_WIDTH)


def _interleave(order, stages):
    live = dict(stages)

    def step(key):
        if key in live:
            try:
                next(live[key])
            except StopIteration:
                del live[key]

    for key in order:
        step(key)
    while live:
        for key in list(live):
            step(key)


def _layer_kernel(xn_ref, xc_ref, nw_ref, wa_ref, wb_ref, rgp_ref, rgw_ref, gp_ref, gdcw_ref,
                  wo_ref, fw_ref, mlnw_ref, gdnw_ref, cs_ref, o_ref,
                  p_s, gdo_s, rgo_s, yrg_s, hn_s, raw_s, raw_z, xc_s, rg_tail, gd_tail,
                  y_ref, rg_h, ml_c, ml_m, gd_s,
                  *, blocks_per_seq, final_norm):
    i = pl.program_id(0)
    slot = lax.rem(i, 2)
    p_new = p_s.at[slot]
    p_cur = p_s.at[1 - slot]

    @pl.when(i == 0)
    def _():
        p_s[1] = jnp.zeros(p_s.shape[1:], p_s.dtype)
        gdo_s[1] = jnp.zeros(gdo_s.shape[1:], gdo_s.dtype)
        rgo_s[1] = jnp.zeros(rgo_s.shape[1:], rgo_s.dtype)

    @pl.when(lax.rem(i, blocks_per_seq) == 0)
    def _():
        rg_tail[...] = jnp.zeros_like(rg_tail)
        gd_tail[...] = jnp.zeros_like(gd_tail)

    @pl.when(lax.rem(jnp.maximum(i - 1, 0), blocks_per_seq) == 0)
    def _():
        rg_h[...] = jnp.zeros_like(rg_h)
        ml_c[...] = jnp.zeros_like(ml_c)
        ml_m[...] = jnp.zeros_like(ml_m)
        gd_s[...] = jnp.zeros_like(gd_s)

    o_ref[...] = xc_ref[...]

    def project(k0):
        o_ref[...] += jnp.dot(y_ref[:, k0:k0 + W], wo_ref[k0:k0 + W, :],
                              preferred_element_type=F32)

    gates, csum_gd, csum_ml, rows = _gate_tables(p_cur, cs_ref)
    _interleave(LAYER_ORDER, dict(
        P=_inproj_stream(xn_ref, nw_ref, wa_ref, wb_ref, rgp_ref, rgw_ref, gp_ref, gdcw_ref,
                         p_new, gdo_s.at[slot], rgo_s.at[slot], hn_s, raw_s, raw_z, xc_s,
                         rg_tail, gd_tail),
        G=_gdn(p_cur, gdo_s.at[1 - slot], gdnw_ref, y_ref, gd_s, gates, csum_gd, rows, project),
        M=_mlstm(p_cur, mlnw_ref, y_ref, ml_c, ml_m, gates, csum_ml, rows, project),
        R=_rglru(rgo_s.at[1 - slot], yrg_s, y_ref, rg_h, project)))
    if final_norm:
        xn = o_ref[...]
        r = lax.rsqrt(jnp.mean(xn * xn, axis=-1, keepdims=True) + NORM_EPS)
        o_ref[...] = (xn * r) * fw_ref[...]


def _cumsum_matrix(ts):
    r = jnp.arange(ts)[:, None]
    c = jnp.arange(ts)[None, :]
    m64 = (r >= c) & (r // GD_CHUNK == c // GD_CHUNK)
    m128 = (r >= c) & (r // PAIR == c // PAIR)
    return jnp.concatenate([m64, m128], axis=0).astype(BF16)


def _layer(x2, norm_w, w_a, w_b, rgp, rgw, gp, gdcw, w_out, final_w, mlnw, gdnw, cs, layer, seq,
           final_norm):
    t = x2.shape[0]
    ts = TS_BLOCK
    nb = t // ts
    per_layer = lambda a, **kw: pl.BlockSpec((None,) + a.shape[1:],
                                             lambda i: (layer,) + (0,) * (a.ndim - 1), **kw)
    once = dict(pipeline_mode=pl.Buffered(1))
    return pl.pallas_call(
        functools.partial(_layer_kernel, blocks_per_seq=seq // ts, final_norm=final_norm),
        grid=(nb + 1,),
        in_specs=[pl.BlockSpec((ts, D_MODEL), lambda i: (jnp.minimum(i, nb - 1), 0)),
                  pl.BlockSpec((ts, D_MODEL), lambda i: (jnp.maximum(i - 1, 0), 0)),
                  per_layer(norm_w),
                  pl.BlockSpec((None, D_MODEL, WCOL_SPLIT), lambda i: (layer, 0, 0), **once),
                  per_layer(w_b, **once), per_layer(rgp), per_layer(rgw), per_layer(gp),
                  per_layer(gdcw), per_layer(w_out, **once),
                  pl.BlockSpec(final_w.shape, lambda i: (0, 0)),
                  per_layer(mlnw), per_layer(gdnw), pl.BlockSpec(cs.shape, lambda i: (0, 0))],
        out_specs=pl.BlockSpec((ts, D_MODEL), lambda i: (jnp.maximum(i - 1, 0), 0)),
        out_shape=jax.ShapeDtypeStruct((t, D_MODEL), F32),
        scratch_shapes=[pltpu.VMEM((2, ts, D_PROJ), F32),
                        pltpu.VMEM((2, GD_CONV // LANES, ts, LANES), F32),
                        pltpu.VMEM((2, 3 * RG_TILES, ts, LANES), F32),
                        pltpu.VMEM((RG_TILES, ts, LANES), F32),
                        pltpu.VMEM((ts, D_MODEL), BF16),
                        pltpu.VMEM((RAW_TILES, ts, LANES), F32),
                        pltpu.VMEM((ts, W), F32),
                        pltpu.VMEM((RG_WIDTH // LANES, ts, LANES), F32),
                        pltpu.VMEM((SUBLANES, RG_WIDTH), F32),
                        pltpu.VMEM((SUBLANES, GD_CONV), F32),
                        pltpu.VMEM((ts, D_MIX), BF16),
                        pltpu.VMEM((SUBLANES, RG_WIDTH), F32),
                        pltpu.VMEM((ML_HEADS, ML_DH, 2 * ML_DH), F32),
                        pltpu.VMEM((ML_HEADS, SUBLANES, LANES), F32),
                        pltpu.VMEM((GD_HEADS, GD_DK, GD_DV), F32)],
        compiler_params=pltpu.CompilerParams(dimension_semantics=("arbitrary",),
                                             vmem_limit_bytes=VMEM_LIMIT),
        name="layer",
    )(x2, x2, norm_w, w_a, w_b, rgp, rgw, gp, gdcw, w_out, final_w, mlnw, gdnw, cs)


def _w_in_tail(w_in):
    pad = jnp.zeros(w_in.shape[:-1] + (LANES - GATE_LANES,), w_in.dtype)
    return jnp.concatenate([
        w_in[..., SRC_GD:SRC_GD_GATES], w_in[..., SRC_ML_GATES:SRC_GD],
        w_in[..., SRC_GD_GATES:], pad], axis=-1)


def _rg_gate_blockdiag(gate_w):
    depth = gate_w.shape[0]
    nb = RG_BLOCKS // 2
    eye = jnp.eye(nb, dtype=gate_w.dtype)
    halves = []
    for hf in range(2):
        blk = gate_w[:, :, hf * nb:(hf + 1) * nb]
        bd = jnp.einsum('lgncd,nm->lgncmd', blk, eye)
        bd = bd.reshape(depth, 2, nb * RG_BLOCK, nb * RG_BLOCK)
        halves.append(jnp.concatenate([bd[:, 0], bd[:, 1]], axis=-1))
    return jnp.stack(halves, axis=1)


def _gate_params(ml_gate_b, gd_dt_bias, gd_a_log):
    depth = ml_gate_b.shape[0]
    zeros = lambda n: jnp.zeros((depth, n), F32)
    row0 = jnp.concatenate([ml_gate_b[:, 0], ml_gate_b[:, 1], gd_dt_bias,
                            zeros(LANES - LANE_GD_B)], axis=-1)
    row1 = jnp.concatenate([zeros(LANE_GD_A), gd_a_log, zeros(LANES - LANE_GD_B)], axis=-1)
    rest = jnp.zeros((depth, SUBLANES - 2, LANES), F32)
    return jnp.concatenate([row0[:, None], row1[:, None], rest], axis=1)


def kernel(x, norm_w, w_in, rg_conv_w, rg_conv_b, rg_gate_w, rg_gate_b, rg_lambda, ml_gate_b,
           ml_norm_w, gd_conv_w, gd_a_log, gd_dt_bias, gd_norm_w, w_out, final_norm_w):
    bsz, seq, _ = x.shape
    depth = w_in.shape[0]
    t = bsz * seq
    w_a = w_in.astype(BF16)
    w_b = _w_in_tail(w_in).astype(BF16)
    w_o = w_out.astype(BF16)
    rgp = jnp.concatenate([rg_conv_w, rg_conv_b[:, None], rg_gate_b, rg_lambda[:, None]],
                          axis=1)
    rgw = _rg_gate_blockdiag(rg_gate_w).astype(BF16)
    gp = _gate_params(ml_gate_b, gd_dt_bias, gd_a_log)
    cs = _cumsum_matrix(TS_BLOCK)

    x2 = x.reshape(t, D_MODEL)
    for l in range(depth):
        x2 = _layer(x2, norm_w[:, None], w_a, w_b, rgp, rgw, gp, gd_conv_w, w_o, final_norm_w[None],
                    ml_norm_w[:, None], gd_norm_w[:, None], cs, l, seq, l == depth - 1)
    return x2.reshape(bsz, seq, D_MODEL)
```

```python
import functools

import jax
import jax.numpy as jnp
from jax import lax
from jax.experimental import pallas as pl
from jax.experimental.pallas import tpu as pltpu

F32 = jnp.float32
BF16 = jnp.bfloat16

D_MODEL = 1024
CONV_K = 4
CONV_CLASSES = 4
NORM_EPS = 1e-6
RG_WIDTH = 512
RG_BLOCKS = 8
RG_BLOCK = RG_WIDTH // RG_BLOCKS
RG_C = 8.0
ML_HEADS = 4
ML_DH = 128
ML_WIDTH = ML_HEADS * ML_DH
GD_HEADS = 4
GD_DK = 128
GD_DV = 128
GD_QK = GD_HEADS * GD_DK
GD_WIDTH = GD_HEADS * GD_DV
GD_CONV = 2 * GD_QK + GD_WIDTH
D_MIX = RG_WIDTH + ML_WIDTH + GD_WIDTH
GD_CHUNK = 64
PAIR = 2 * GD_CHUNK
SUBLANES = 8
LANES = 128

W = 512
WCOL_RG_X, WCOL_RG_Z = 0 * W, 1 * W
WCOL_ML_Q, WCOL_ML_K, WCOL_ML_V, WCOL_ML_O, WCOL_ML_Z = 2 * W, 3 * W, 4 * W, 5 * W, 6 * W
WCOL_GD_QKV, WCOL_GD_Z = 7 * W, 10 * W
WCOL_GATES = 11 * W
WCOL_SPLIT = WCOL_GD_QKV
D_PROJ = WCOL_GATES + LANES
OFF_RG_A, OFF_RG_B, OFF_RG_G = 0 * W, 1 * W, 2 * W
OFF_ML_Q, OFF_ML_K, OFF_ML_V, OFF_ML_G = 3 * W, 4 * W, 5 * W, 6 * W
OFF_GD_Q, OFF_GD_K, OFF_GD_V, OFF_GD_G = 7 * W, 8 * W, 9 * W, 10 * W
OFF_GATES = 11 * W
LANE_ML_I, LANE_ML_F, LANE_GD_A, LANE_GD_B = 0, 4, 8, 12
GATE_LANES = 16
RAW_RG, RAW_GD = 0, RG_WIDTH // LANES
RAW_TILES = RAW_GD + GD_CONV // LANES
RG_TILES = RG_WIDTH // LANES
RGT_A, RGT_B, RGT_G = 0, RG_TILES, 2 * RG_TILES
RG_CLASSES = 4
SRC_ML_GATES = 2 * RG_WIDTH + 5 * ML_WIDTH
SRC_GD = SRC_ML_GATES + 2 * ML_HEADS
SRC_GD_GATES = SRC_GD + 2 * GD_QK + 2 * GD_WIDTH

VMEM_LIMIT = 56 * 1024 * 1024
TS_BLOCK = 256
LAYER_ORDER = "GPP" * 9 + "MRPG" * 5


def _dot(a, b):
    return jnp.dot(a.astype(BF16), b.astype(BF16), preferred_element_type=F32)


def _dot_nt(a, b):
    return lax.dot_general(a.astype(BF16), b.astype(BF16), (((1,), (1,)), ((), ())),
                           preferred_element_type=F32)


def _dot_tn(a, b):
    return lax.dot_general(a.astype(BF16), b.astype(BF16), (((0,), (0,)), ((), ())),
                           preferred_element_type=F32)


def _split3(x):
    hi = x.astype(BF16)
    r1 = x - hi.astype(F32)
    mid = r1.astype(BF16)
    lo = (r1 - mid.astype(F32)).astype(BF16)
    return hi, mid, lo


def _softplus(x):
    return jnp.maximum(x, 0.0) + jnp.log1p(jnp.exp(-jnp.abs(x)))


def _sigmoid(x):
    return 1.0 / (1.0 + jnp.exp(-x))


def _silu(x):
    return x * _sigmoid(x)


def _causal_conv(x_ref, tile, tail_ref, tcol0, cw_ref, out_ref, out_tile, post):
    ts = x_ref.shape[1]
    nc = CONV_CLASSES
    n = ts // nc
    tcols = slice(tcol0, tcol0 + LANES)
    tail = tail_ref[:, tcols]
    new_tail = x_ref[tile, ts - SUBLANES:ts, :]
    cls = [x_ref[tile, pl.ds(r, n, stride=nc), :] for r in range(nc)]
    first = lax.broadcasted_iota(jnp.int32, (n, LANES), 0) == 0
    prev = {r: jnp.where(first, tail[SUBLANES - nc + r:SUBLANES - nc + r + 1, :],
                         pltpu.roll(cls[r], 1, 0))
            for r in range(nc - CONV_K + 1, nc)}
    for r in range(nc):
        acc = None
        for k in range(CONV_K):
            j = CONV_K - 1 - k
            tap = cls[r - j] if r >= j else prev[r - j + nc]
            term = tap * cw_ref[k:k + 1, tcols]
            acc = term if acc is None else acc + term
        out_ref[out_tile, pl.ds(r, n, stride=nc), :] = post(acc)
    tail_ref[:, tcols] = new_tail


def _inproj_stream(x_ref, nw_ref, wa_ref, wb_ref, rgp_ref, rgw_ref, gp_ref, gdcw_ref, p_ref,
                   gdo_ref, rgo_ref, hn_s, raw_s, raw_z, xc_s, rg_tail, gd_tail):
    x = x_ref[...]
    r = lax.rsqrt(jnp.mean(x * x, axis=-1, keepdims=True) + NORM_EPS)
    hn_s[...] = ((x * r) * nw_ref[...]).astype(BF16)

    H = W // 2

    def proj(col, width):
        if col < WCOL_SPLIT:
            w = wa_ref[:, col:col + width]
        else:
            w = wb_ref[:, col - WCOL_SPLIT:col - WCOL_SPLIT + width]
        return jnp.dot(hn_s[...], w, preferred_element_type=F32)

    def mm(dst_ref, dst, src, scale=None):
        def run():
            y = proj(src, H)
            dst_ref[:, dst:dst + H] = y if scale is None else y * scale
        return run

    def mm_raw(tile0, src):
        def run():
            y = proj(src, H)
            for i in range(H // LANES):
                raw_s[tile0 + i] = y[:, i * LANES:(i + 1) * LANES]
        return run

    def xc_half(c):
        return jnp.concatenate([xc_s[c // LANES + i] for i in range(H // LANES)], axis=1)

    def mm_rg_gates(c):
        def run():
            g = jnp.dot(xc_half(c).astype(BF16), rgw_ref[c // H],
                        preferred_element_type=F32)
            p_ref[:, OFF_RG_B + c:OFF_RG_B + c + H] = g[:, :H]
            raw_z[:, c:c + H] = g[:, H:]
        return run

    def ep_rg_conv(c):
        def run():
            for t in range(c // LANES, (c + H) // LANES):
                bias = rgp_ref[4:5, t * LANES:(t + 1) * LANES]
                _causal_conv(raw_s, RAW_RG + t, rg_tail, t * LANES, rgp_ref, xc_s, t,
                             lambda y, bias=bias: y + bias)
        return run

    def ep_rg_ab(c):
        def run():
            gr = p_ref[:, OFF_RG_B + c:OFF_RG_B + c + H] + rgp_ref[5:6, c:c + H]
            gi = raw_z[:, c:c + H] + rgp_ref[6:7, c:c + H]
            log_a = (-RG_C * _sigmoid(gr)) * _softplus(-rgp_ref[7:8, c:c + H])
            a = jnp.exp(log_a)
            b = jnp.sqrt(-jnp.tanh(log_a) * (a * a + 1.0)) * (_sigmoid(gi) * xc_half(c))
            for i in range(H // LANES):
                rgo_ref[RGT_A + c // LANES + i] = a[:, i * LANES:(i + 1) * LANES]
                rgo_ref[RGT_B + c // LANES + i] = b[:, i * LANES:(i + 1) * LANES]
        return run

    def ep_rg_gate(c):
        def run():
            g = _silu(p_ref[:, OFF_RG_G + c:OFF_RG_G + c + H])
            for i in range(H // LANES):
                rgo_ref[RGT_G + c // LANES + i] = g[:, i * LANES:(i + 1) * LANES]
        return run

    def ep_silu(off):
        def run():
            p_ref[:, off:off + H] = _silu(p_ref[:, off:off + H])
        return run

    def ep_gd_conv(tcol, norm_scale):
        def post(y):
            y = _silu(y)
            if norm_scale is None:
                return y
            y = y * lax.rsqrt(jnp.sum(y * y, axis=-1, keepdims=True) + NORM_EPS)
            return y if norm_scale == 1.0 else y * norm_scale

        def run():
            for t in range(tcol // LANES, (tcol + H) // LANES):
                _causal_conv(raw_s, RAW_GD + t, gd_tail, t * LANES, gdcw_ref, gdo_ref, t, post)
        return run

    def ep_ml_gate(c):
        def run():
            p_ref[:, OFF_ML_G + c:OFF_ML_G + c + H] = (
                _sigmoid(p_ref[:, OFF_ML_G + c:OFF_ML_G + c + H]) * _silu(raw_z[:, c:c + H]))
        return run

    def mm_gates():
        p_ref[:, OFF_GATES:OFF_GATES + LANES] = proj(WCOL_GATES, LANES)

    def ep_gates():
        gates = p_ref[:, OFF_GATES:OFF_GATES + LANES] + gp_ref[0:1, :]
        lane = lax.broadcasted_iota(jnp.int32, gates.shape, 1)
        is_i = lane < LANE_ML_F
        is_f = (lane >= LANE_ML_F) & (lane < LANE_ML_F + ML_HEADS)
        is_a = (lane >= LANE_GD_A) & (lane < LANE_GD_A + GD_HEADS)
        log_f = -_softplus(-gates)
        g_gd = -jnp.exp(gp_ref[1:2, :]) * _softplus(gates)
        p_ref[:, OFF_GATES:OFF_GATES + LANES] = jnp.where(
            is_i, gates, jnp.where(is_f, log_f, jnp.where(is_a, g_gd, _sigmoid(gates))))

    units = []
    for c in range(0, W, H):
        units += [
            ([mm_raw(RAW_RG + c // LANES, WCOL_RG_X + c)], ep_rg_conv(c)),
            ([mm_raw(RAW_GD + c // LANES, WCOL_GD_QKV + c)], ep_gd_conv(c, GD_DK ** -0.5)),
            ([mm(p_ref, OFF_ML_Q + c, WCOL_ML_Q + c, ML_DH ** -0.5)], None),
            ([mm_rg_gates(c)], ep_rg_ab(c)),
            ([mm_raw(RAW_GD + (W + c) // LANES, WCOL_GD_QKV + W + c)], ep_gd_conv(W + c, 1.0)),
            ([mm(p_ref, OFF_ML_K + c, WCOL_ML_K + c)], None),
            ([mm_raw(RAW_GD + (2 * W + c) // LANES, WCOL_GD_QKV + 2 * W + c)],
             ep_gd_conv(2 * W + c, None)),
            ([mm(p_ref, OFF_ML_V + c, WCOL_ML_V + c)], None),
            ([mm(p_ref, OFF_RG_G + c, WCOL_RG_Z + c)], ep_rg_gate(c)),
            ([mm(p_ref, OFF_ML_G + c, WCOL_ML_O + c), mm(raw_z, c, WCOL_ML_Z + c)], ep_ml_gate(c)),
            ([mm(p_ref, OFF_GD_G + c, WCOL_GD_Z + c)], ep_silu(OFF_GD_G + c)),
        ]
    units.append(([mm_gates], ep_gates))
    pending = None
    for mms, ep in units:
        for run in mms:
            run()
        if pending is not None:
            pending()
        pending = ep
        yield
    if pending is not None:
        pending()


def _rglru(rgo_ref, yrg_s, y_ref, rg_h, project):
    nc = RG_CLASSES
    ts = rgo_ref.shape[1]
    n = ts // nc
    ng = n // SUBLANES
    tiles = range(RG_TILES)
    ld = lambda t, r: rgo_ref[t, pl.ds(r, n, stride=nc), :]
    pa, pb = {}, {}
    for lt in tiles:
        pa[lt, 0] = ld(RGT_A + lt, 0)
        pb[lt, 0] = ld(RGT_B + lt, 0)
        for r in range(1, nc):
            a = ld(RGT_A + lt, r)
            pb[lt, r] = a * pb[lt, r - 1] + ld(RGT_B + lt, r)
            pa[lt, r] = a * pa[lt, r - 1]
    yield
    sub = lax.broadcasted_iota(jnp.int32, (ng, SUBLANES, LANES), 1)
    a3 = {lt: pa[lt, nc - 1].reshape(ng, SUBLANES, LANES) for lt in tiles}
    b3 = {lt: pb[lt, nc - 1].reshape(ng, SUBLANES, LANES) for lt in tiles}
    d = 1
    while d < SUBLANES:
        for lt in tiles:
            a_sh = pltpu.roll(a3[lt], d, 1)
            b_sh = pltpu.roll(b3[lt], d, 1)
            valid = sub >= d
            b3[lt] = jnp.where(valid, a3[lt] * b_sh + b3[lt], b3[lt])
            a3[lt] = jnp.where(valid, a3[lt] * a_sh, a3[lt])
        d *= 2
    yield
    first = lax.broadcasted_iota(jnp.int32, (n, LANES), 0) == 0
    for lt in tiles:
        cols = slice(lt * LANES, (lt + 1) * LANES)
        carry_in = rg_h[0:1, cols]
        carry = carry_in
        groups = []
        for g in range(ng):
            hg = b3[lt][g] + a3[lt][g] * carry
            groups.append(hg)
            carry = hg[SUBLANES - 1:SUBLANES, :]
        rg_h[:, cols] = jnp.broadcast_to(carry, (SUBLANES, LANES))
        h_last = jnp.concatenate(groups, axis=0)
        h_in = jnp.where(first, carry_in, pltpu.roll(h_last, 1, 0))
        for r in range(nc):
            h = h_last if r == nc - 1 else pb[lt, r] + pa[lt, r] * h_in
            yrg_s[lt, pl.ds(r, n, stride=nc), :] = h * ld(RGT_G + lt, r)
    yield
    y_ref[:, 0:RG_WIDTH] = jnp.concatenate([yrg_s[lt] for lt in tiles], axis=1).astype(y_ref.dtype)
    project(0)


def _gate_tables(p_ref, cs_ref):
    ts = p_ref.shape[0]
    gates = p_ref[:, OFF_GATES:OFF_GATES + LANES]
    lane = lax.broadcasted_iota(jnp.int32, (ts, LANES), 1)
    is_i = lane < LANE_ML_F
    is_f = (lane >= LANE_ML_F) & (lane < LANE_ML_F + ML_HEADS)
    is_a = (lane >= LANE_GD_A) & (lane < LANE_GD_A + GD_HEADS)
    x = jnp.where(is_f | is_a, gates, 0.0)
    hi, mid, lo = _split3(x)
    packed = (hi.astype(F32) + pltpu.roll(mid.astype(F32), GATE_LANES, 1)
              + pltpu.roll(lo.astype(F32), 2 * GATE_LANES, 1))
    cs = jnp.dot(cs_ref[...], packed.astype(BF16), preferred_element_type=F32)
    cs = cs + pltpu.roll(cs, LANES - GATE_LANES, 1) + pltpu.roll(cs, LANES - 2 * GATE_LANES, 1)
    csum_gd = cs[:ts]
    csum_ml = cs[ts:]
    z = jnp.where(is_i, gates, jnp.where(is_f, csum_ml, jnp.where(is_a, csum_gd, 0.0)))
    zs = jnp.concatenate(_split3(z), axis=0)
    sel = (lax.broadcasted_iota(jnp.int32, (GATE_LANES, LANES), 0)
           == lax.broadcasted_iota(jnp.int32, (GATE_LANES, LANES), 1)).astype(BF16)
    zt = lax.dot_general(sel, zs, (((1,), (1,)), ((), ())), preferred_element_type=F32)
    rows = zt[:, :ts] + zt[:, ts:2 * ts] + zt[:, 2 * ts:]
    return gates, csum_gd, csum_ml, rows


def _mlstm(p_ref, mlnw_ref, y_ref, ml_c, ml_m, gates, csum_ml, rows, project):
    ts = p_ref.shape[0]
    L = PAIR
    r_i = lax.broadcasted_iota(jnp.int32, (L, L), 0)
    c_i = lax.broadcasted_iota(jnp.int32, (L, L), 1)
    causal = r_i >= c_i
    ones_blk = jnp.ones((L, ML_DH), F32)
    units = [(h, c) for c in range(ts // L) for h in range(ML_HEADS)]

    st = {}
    for h, c in units:
        rs = slice(c * L, (c + 1) * L)
        q = p_ref[rs, OFF_ML_Q + h * ML_DH:OFF_ML_Q + (h + 1) * ML_DH]
        k = p_ref[rs, OFF_ML_K + h * ML_DH:OFF_ML_K + (h + 1) * ML_DH]
        li_c = gates[rs, LANE_ML_I + h:LANE_ML_I + h + 1]
        bq_c = csum_ml[rs, LANE_ML_F + h:LANE_ML_F + h + 1]
        r_row = (rows[LANE_ML_I + h:LANE_ML_I + h + 1, rs]
                 - rows[LANE_ML_F + h:LANE_ML_F + h + 1, rs])
        dm = jnp.where(causal, bq_c + r_row, -1e30)
        mx = jnp.max(dm, axis=-1, keepdims=True)
        g_last = bq_c[L - 1:L, :]
        ws = g_last - bq_c + li_c
        mws = jnp.max(ws, axis=0, keepdims=True)
        st[h, c] = dict(q=q, k=k, bq_c=bq_c, mx=mx, g_last=g_last, mws=mws,
                        pm=jnp.exp(dm - mx), wk=jnp.exp(ws - mws), qk=_dot_nt(q, k))
    yield
    for h, c in units:
        u = st[h, c]
        rs = slice(c * L, (c + 1) * L)
        v = p_ref[rs, OFF_ML_V + h * ML_DH:OFF_ML_V + (h + 1) * ML_DH]
        v_aug = jnp.concatenate([v, ones_blk], axis=1)
        u["intra"] = _dot(u["qk"] * u["pm"], v_aug)
        u["kv"] = _dot_tn(u["k"] * u["wk"], v_aug)

    yield
    c_aug = [ml_c[h] for h in range(ML_HEADS)]
    m_prev = [ml_m[h][0:1, 0:1] for h in range(ML_HEADS)]
    outs = [[] for _ in range(ML_HEADS)]
    for c in range(ts // L):
        qc = [_dot(st[h, c]["q"], c_aug[h]) for h in range(ML_HEADS)]
        for h in range(ML_HEADS):
            u = st[h, c]
            m_inter = u["bq_c"] + m_prev[h]
            m_t = jnp.maximum(m_inter, u["mx"])
            na = jnp.exp(u["mx"] - m_t) * u["intra"] + jnp.exp(m_inter - m_t) * qc[h]
            outs[h].append(na[:, :ML_DH] / jnp.maximum(jnp.abs(na[:, ML_DH:]), jnp.exp(-m_t)))
            m_new = jnp.maximum(u["g_last"] + m_prev[h], u["mws"])
            c_aug[h] = (jnp.exp(u["g_last"] + m_prev[h] - m_new) * c_aug[h]
                        + jnp.exp(u["mws"] - m_new) * u["kv"])
            m_prev[h] = m_new
        yield
    for h in range(ML_HEADS):
        sl = slice(h * ML_DH, (h + 1) * ML_DH)
        ml_c[h] = c_aug[h]
        ml_m[h] = jnp.broadcast_to(m_prev[h], ml_m.shape[1:])
        hh = jnp.concatenate(outs[h], axis=0)
        hn = hh * lax.rsqrt(jnp.mean(hh * hh, axis=-1, keepdims=True) + NORM_EPS)
        y_ml = (hn * mlnw_ref[0:1, sl]) * p_ref[:, OFF_ML_G + h * ML_DH:OFF_ML_G + (h + 1) * ML_DH]
        y_ref[:, RG_WIDTH + h * ML_DH:RG_WIDTH + (h + 1) * ML_DH] = y_ml.astype(y_ref.dtype)
    project(RG_WIDTH)


def _gdn(p_ref, gdo_ref, gdnw_ref, y_ref, gd_s, gates, csum_gd, rows, project):
    ts = p_ref.shape[0]
    L = PAIR
    C = GD_CHUNK
    r_i = lax.broadcasted_iota(jnp.int32, (L, L), 0)
    c_i = lax.broadcasted_iota(jnp.int32, (L, L), 1)
    same = (r_i >= C) == (c_i >= C)
    incl = same & (r_i >= c_i)
    strict = same & (r_i > c_i)
    eye = (r_i == c_i).astype(F32)
    second = lax.broadcasted_iota(jnp.int32, (L, 1), 0) >= C
    zeros_c = jnp.zeros((C, GD_DV), F32)
    units = [(h, j) for j in range(ts // L) for h in range(GD_HEADS)]

    st = {}
    for h, j in units:
        rs = slice(j * L, (j + 1) * L)
        q = gdo_ref[h, rs, :]
        k = gdo_ref[GD_HEADS + h, rs, :]
        beta = gates[rs, LANE_GD_B + h:LANE_GD_B + h + 1]
        gc_c = csum_gd[rs, LANE_GD_A + h:LANE_GD_A + h + 1]
        gc_r = rows[LANE_GD_A + h:LANE_GD_A + h + 1, rs]
        gam = jnp.where(incl, jnp.exp(jnp.where(incl, gc_c - gc_r, 0.0)), 0.0)
        kb = k * beta
        egc = jnp.exp(gc_c)
        g_first = gc_c[C - 1:C, :]
        g_second = gc_c[L - 1:L, :]
        st[h, j] = dict(k=k, kb=kb, beta=beta, gam=gam, egc=egc, q_dec=q * egc,
                        g_last=(g_first, g_second),
                        k_dec=k * jnp.exp(jnp.where(second, g_second, g_first) - gc_c),
                        kq=_dot_nt(jnp.concatenate([kb, q], axis=0), k))
    yield
    for key in units:
        u = st[key]
        u["aqk"] = u["kq"][L:] * u["gam"]
        pw = jnp.where(strict, -(u["kq"][:L] * u["gam"]), 0.0)
        u["t_inv"] = eye + pw
        u["pw"] = _dot(pw, pw)
    yield
    span = 4
    while span < C:
        for key in units:
            u = st[key]
            both = _dot(jnp.concatenate([u["pw"], u["t_inv"]], axis=0), u["pw"])
            u["pw"] = both[:L]
            u["t_inv"] = u["t_inv"] + both[L:]
        span *= 2
        yield
    for key in units:
        u = st[key]
        u["t_inv"] = u["t_inv"] + _dot(u["t_inv"], u["pw"])
    yield
    for h, j in units:
        u = st[h, j]
        rs = slice(j * L, (j + 1) * L)
        v = gdo_ref[2 * GD_HEADS + h, rs, :]
        uw = _dot(u["t_inv"], jnp.concatenate([v * u["beta"], u["kb"] * u["egc"]], axis=1))
        u["u"] = uw[:, :GD_DV]
        u["w"] = uw[:, GD_DV:]

    yield
    state = [gd_s[h] for h in range(GD_HEADS)]
    outs = [[] for _ in range(GD_HEADS)]
    for j in range(ts // L):
        for half in range(2):
            hs = slice(half * C, (half + 1) * C)
            wq = [_dot(jnp.concatenate([st[h, j]["w"][hs], st[h, j]["q_dec"][hs]], axis=0), state[h])
                  for h in range(GD_HEADS)]
            v_new = [st[h, j]["u"][hs] - wq[h][:C] for h in range(GD_HEADS)]
            for h in range(GD_HEADS):
                v_pad = jnp.concatenate([v_new[h], zeros_c] if half == 0 else [zeros_c, v_new[h]],
                                        axis=0)
                outs[h].append(wq[h][C:] + _dot(st[h, j]["aqk"][hs], v_pad))
            for h in range(GD_HEADS):
                u = st[h, j]
                state[h] = (state[h] * jnp.exp(u["g_last"][half])
                            + _dot_tn(u["k_dec"][hs], v_new[h]))
            yield
    for h in range(GD_HEADS):
        gd_s[h] = state[h]
        o = jnp.concatenate(outs[h], axis=0)
        on = o * lax.rsqrt(jnp.mean(o * o, axis=-1, keepdims=True) + NORM_EPS)
        y_gd = on * gdnw_ref[0:1, :] * p_ref[:, OFF_GD_G + h * GD_DV:OFF_GD_G + (h + 1) * GD_DV]
        y_ref[:, RG_WIDTH + ML_WIDTH + h * GD_DV:RG_WIDTH + ML_WIDTH + (h + 1) * GD_DV] = (
            y_gd.astype(y_ref.dtype))
    project(RG_WIDTH + ML_WIDTH)


def _interleave(order, stages):
    live = dict(stages)

    def step(key):
        if key in live:
            try:
                next(live[key])
            except StopIteration:
                del live[key]

    for key in order:
        step(key)
    while live:
        for key in list(live):
            step(key)


def _layer_kernel(xn_ref, xc_ref, nw_ref, wa_ref, wb_ref, rgp_ref, rgw_ref, gp_ref, gdcw_ref,
                  wo_ref, fw_ref, mlnw_ref, gdnw_ref, cs_ref, o_ref,
                  p_s, gdo_s, rgo_s, yrg_s, hn_s, raw_s, raw_z, xc_s, rg_tail, gd_tail,
                  y_ref, rg_h, ml_c, ml_m, gd_s,
                  *, blocks_per_seq, final_norm):
    i = pl.program_id(0)
    slot = lax.rem(i, 2)
    p_new = p_s.at[slot]
    p_cur = p_s.at[1 - slot]

    @pl.when(i == 0)
    def _():
        p_s[1] = jnp.zeros(p_s.shape[1:], p_s.dtype)
        gdo_s[1] = jnp.zeros(gdo_s.shape[1:], gdo_s.dtype)
        rgo_s[1] = jnp.zeros(rgo_s.shape[1:], rgo_s.dtype)

    @pl.when(lax.rem(i, blocks_per_seq) == 0)
    def _():
        rg_tail[...] = jnp.zeros_like(rg_tail)
        gd_tail[...] = jnp.zeros_like(gd_tail)

    @pl.when(lax.rem(jnp.maximum(i - 1, 0), blocks_per_seq) == 0)
    def _():
        rg_h[...] = jnp.zeros_like(rg_h)
        ml_c[...] = jnp.zeros_like(ml_c)
        ml_m[...] = jnp.zeros_like(ml_m)
        gd_s[...] = jnp.zeros_like(gd_s)

    o_ref[...] = xc_ref[...]

    def project(k0):
        o_ref[...] += jnp.dot(y_ref[:, k0:k0 + W], wo_ref[k0:k0 + W, :],
                              preferred_element_type=F32)

    gates, csum_gd, csum_ml, rows = _gate_tables(p_cur, cs_ref)
    _interleave(LAYER_ORDER, dict(
        P=_inproj_stream(xn_ref, nw_ref, wa_ref, wb_ref, rgp_ref, rgw_ref, gp_ref, gdcw_ref,
                         p_new, gdo_s.at[slot], rgo_s.at[slot], hn_s, raw_s, raw_z, xc_s,
                         rg_tail, gd_tail),
        G=_gdn(p_cur, gdo_s.at[1 - slot], gdnw_ref, y_ref, gd_s, gates, csum_gd, rows, project),
        M=_mlstm(p_cur, mlnw_ref, y_ref, ml_c, ml_m, gates, csum_ml, rows, project),
        R=_rglru(rgo_s.at[1 - slot], yrg_s, y_ref, rg_h, project)))
    if final_norm:
        xn = o_ref[...]
        r = lax.rsqrt(jnp.mean(xn * xn, axis=-1, keepdims=True) + NORM_EPS)
        o_ref[...] = (xn * r) * fw_ref[...]


def _cumsum_matrix(ts):
    r = jnp.arange(ts)[:, None]
    c = jnp.arange(ts)[None, :]
    m64 = (r >= c) & (r // GD_CHUNK == c // GD_CHUNK)
    m128 = (r >= c) & (r // PAIR == c // PAIR)
    return jnp.concatenate([m64, m128], axis=0).astype(BF16)


def _layer(x2, norm_w, w_a, w_b, rgp, rgw, gp, gdcw, w_out, final_w, mlnw, gdnw, cs, layer, seq,
           final_norm):
    t = x2.shape[0]
    ts = TS_BLOCK
    nb = t // ts
    per_layer = lambda a, **kw: pl.BlockSpec((None,) + a.shape[1:],
                                             lambda i: (layer,) + (0,) * (a.ndim - 1), **kw)
    once = dict(pipeline_mode=pl.Buffered(1))
    return pl.pallas_call(
        functools.partial(_layer_kernel, blocks_per_seq=seq // ts, final_norm=final_norm),
        grid=(nb + 1,),
        in_specs=[pl.BlockSpec((ts, D_MODEL), lambda i: (jnp.minimum(i, nb - 1), 0)),
                  pl.BlockSpec((ts, D_MODEL), lambda i: (jnp.maximum(i - 1, 0), 0)),
                  per_layer(norm_w),
                  pl.BlockSpec((None, D_MODEL, WCOL_SPLIT), lambda i: (layer, 0, 0), **once),
                  per_layer(w_b, **once), per_layer(rgp), per_layer(rgw), per_layer(gp),
                  per_layer(gdcw), per_layer(w_out, **once),
                  pl.BlockSpec(final_w.shape, lambda i: (0, 0)),
                  per_layer(mlnw), per_layer(gdnw), pl.BlockSpec(cs.shape, lambda i: (0, 0))],
        out_specs=pl.BlockSpec((ts, D_MODEL), lambda i: (jnp.maximum(i - 1, 0), 0)),
        out_shape=jax.ShapeDtypeStruct((t, D_MODEL), F32),
        scratch_shapes=[pltpu.VMEM((2, ts, D_PROJ), F32),
                        pltpu.VMEM((2, GD_CONV // LANES, ts, LANES), F32),
                        pltpu.VMEM((2, 3 * RG_TILES, ts, LANES), F32),
                        pltpu.VMEM((RG_TILES, ts, LANES), F32),
                        pltpu.VMEM((ts, D_MODEL), BF16),
                        pltpu.VMEM((RAW_TILES, ts, LANES), F32),
                        pltpu.VMEM((ts, W), F32),
                        pltpu.VMEM((RG_WIDTH // LANES, ts, LANES), F32),
                        pltpu.VMEM((SUBLANES, RG_WIDTH), F32),
                        pltpu.VMEM((SUBLANES, GD_CONV), F32),
                        pltpu.VMEM((ts, D_MIX), BF16),
                        pltpu.VMEM((SUBLANES, RG_WIDTH), F32),
                        pltpu.VMEM((ML_HEADS, ML_DH, 2 * ML_DH), F32),
                        pltpu.VMEM((ML_HEADS, SUBLANES, LANES), F32),
                        pltpu.VMEM((GD_HEADS, GD_DK, GD_DV), F32)],
        compiler_params=pltpu.CompilerParams(dimension_semantics=("arbitrary",),
                                             vmem_limit_bytes=VMEM_LIMIT),
        name="layer",
    )(x2, x2, norm_w, w_a, w_b, rgp, rgw, gp, gdcw, w_out, final_w, mlnw, gdnw, cs)


def _w_in_tail(w_in):
    pad = jnp.zeros(w_in.shape[:-1] + (LANES - GATE_LANES,), w_in.dtype)
    return jnp.concatenate([
        w_in[..., SRC_GD:SRC_GD_GATES], w_in[..., SRC_ML_GATES:SRC_GD],
        w_in[..., SRC_GD_GATES:], pad], axis=-1)


def _rg_gate_blockdiag(gate_w):
    depth = gate_w.shape[0]
    nb = RG_BLOCKS // 2
    eye = jnp.eye(nb, dtype=gate_w.dtype)
    halves = []
    for hf in range(2):
        blk = gate_w[:, :, hf * nb:(hf + 1) * nb]
        bd = jnp.einsum('lgncd,nm->lgncmd', blk, eye)
        bd = bd.reshape(depth, 2, nb * RG_BLOCK, nb * RG_BLOCK)
        halves.append(jnp.concatenate([bd[:, 0], bd[:, 1]], axis=-1))
    return jnp.stack(halves, axis=1)


def _gate_params(ml_gate_b, gd_dt_bias, gd_a_log):
    depth = ml_gate_b.shape[0]
    zeros = lambda n: jnp.zeros((depth, n), F32)
    row0 = jnp.concatenate([ml_gate_b[:, 0], ml_gate_b[:, 1], gd_dt_bias,
                            zeros(LANES - LANE_GD_B)], axis=-1)
    row1 = jnp.concatenate([zeros(LANE_GD_A), gd_a_log, zeros(LANES - LANE_GD_B)], axis=-1)
    rest = jnp.zeros((depth, SUBLANES - 2, LANES), F32)
    return jnp.concatenate([row0[:, None], row1[:, None], rest], axis=1)


def kernel(x, norm_w, w_in, rg_conv_w, rg_conv_b, rg_gate_w, rg_gate_b, rg_lambda, ml_gate_b,
           ml_norm_w, gd_conv_w, gd_a_log, gd_dt_bias, gd_norm_w, w_out, final_norm_w):
    bsz, seq, _ = x.shape
    depth = w_in.shape[0]
    t = bsz * seq
    w_a = w_in.astype(BF16)
    w_b = _w_in_tail(w_a)
    w_o = w_out.astype(BF16)
    rgp = jnp.concatenate([rg_conv_w, rg_conv_b[:, None], rg_gate_b, rg_lambda[:, None]],
                          axis=1)
    rgw = _rg_gate_blockdiag(rg_gate_w).astype(BF16)
    gp = _gate_params(ml_gate_b, gd_dt_bias, gd_a_log)
    cs = _cumsum_matrix(TS_BLOCK)

    x2 = x.reshape(t, D_MODEL)
    for l in range(depth):
        x2 = _layer(x2, norm_w[:, None], w_a, w_b, rgp, rgw, gp, gd_conv_w, w_o, final_norm_w[None],
                    ml_norm_w[:, None], gd_norm_w[:, None], cs, l, seq, l == depth - 1)
    return x2.reshape(bsz, seq, D_MODEL)
```

```python
import functools

import jax
import jax.numpy as jnp
from jax import lax
from jax.experimental import pallas as pl
from jax.experimental.pallas import tpu as pltpu

F32 = jnp.float32
BF16 = jnp.bfloat16

D_MODEL = 1024
CONV_K = 4
CONV_CLASSES = 4
NORM_EPS = 1e-6
RG_WIDTH = 512
RG_BLOCKS = 8
RG_BLOCK = RG_WIDTH // RG_BLOCKS
RG_C = 8.0
ML_HEADS = 4
ML_DH = 128
ML_WIDTH = ML_HEADS * ML_DH
GD_HEADS = 4
GD_DK = 128
GD_DV = 128
GD_QK = GD_HEADS * GD_DK
GD_WIDTH = GD_HEADS * GD_DV
GD_CONV = 2 * GD_QK + GD_WIDTH
D_MIX = RG_WIDTH + ML_WIDTH + GD_WIDTH
GD_CHUNK = 64
PAIR = 2 * GD_CHUNK
SUBLANES = 8
LANES = 128

W = 512
WCOL_RG_X, WCOL_RG_Z = 0 * W, 1 * W
WCOL_ML_Q, WCOL_ML_K, WCOL_ML_V, WCOL_ML_O, WCOL_ML_Z = 2 * W, 3 * W, 4 * W, 5 * W, 6 * W
WCOL_GD_QKV, WCOL_GD_Z = 7 * W, 10 * W
WCOL_GATES = 11 * W
WCOL_SPLIT = WCOL_GD_QKV
D_PROJ = WCOL_GATES + LANES
OFF_RG_A, OFF_RG_B, OFF_RG_G = 0 * W, 1 * W, 2 * W
OFF_ML_Q, OFF_ML_K, OFF_ML_V, OFF_ML_G = 3 * W, 4 * W, 5 * W, 6 * W
OFF_GD_Q, OFF_GD_K, OFF_GD_V, OFF_GD_G = 7 * W, 8 * W, 9 * W, 10 * W
OFF_GATES = 11 * W
LANE_ML_I, LANE_ML_F, LANE_GD_A, LANE_GD_B = 0, 4, 8, 12
GATE_LANES = 16
RAW_RG, RAW_GD = 0, RG_WIDTH // LANES
RAW_TILES = RAW_GD + GD_CONV // LANES
RG_TILES = RG_WIDTH // LANES
RGT_A, RGT_B, RGT_G = 0, RG_TILES, 2 * RG_TILES
RG_CLASSES = 4
SRC_ML_GATES = 2 * RG_WIDTH + 5 * ML_WIDTH
SRC_GD = SRC_ML_GATES + 2 * ML_HEADS
SRC_GD_GATES = SRC_GD + 2 * GD_QK + 2 * GD_WIDTH

VMEM_LIMIT = 56 * 1024 * 1024
TS_BLOCK = 256
LAYER_ORDER = "GPP" * 9 + "MRPG" * 5


def _dot(a, b):
    return jnp.dot(a.astype(BF16), b.astype(BF16), preferred_element_type=F32)


def _dot_nt(a, b):
    return lax.dot_general(a.astype(BF16), b.astype(BF16), (((1,), (1,)), ((), ())),
                           preferred_element_type=F32)


def _dot_tn(a, b):
    return lax.dot_general(a.astype(BF16), b.astype(BF16), (((0,), (0,)), ((), ())),
                           preferred_element_type=F32)


def _split3(x):
    hi = x.astype(BF16)
    r1 = x - hi.astype(F32)
    mid = r1.astype(BF16)
    lo = (r1 - mid.astype(F32)).astype(BF16)
    return hi, mid, lo


def _softplus(x):
    return jnp.maximum(x, 0.0) + jnp.log1p(jnp.exp(-jnp.abs(x)))


def _sigmoid(x):
    return 1.0 / (1.0 + jnp.exp(-x))


def _silu(x):
    return x * _sigmoid(x)


def _causal_conv(x_ref, tile, tail_ref, tcol0, cw_ref, out_ref, out_tile, post):
    ts = x_ref.shape[1]
    nc = CONV_CLASSES
    n = ts // nc
    tcols = slice(tcol0, tcol0 + LANES)
    tail = tail_ref[:, tcols]
    new_tail = x_ref[tile, ts - SUBLANES:ts, :]
    cls = [x_ref[tile, pl.ds(r, n, stride=nc), :] for r in range(nc)]
    first = lax.broadcasted_iota(jnp.int32, (n, LANES), 0) == 0
    prev = {r: jnp.where(first, tail[SUBLANES - nc + r:SUBLANES - nc + r + 1, :],
                         pltpu.roll(cls[r], 1, 0))
            for r in range(nc - CONV_K + 1, nc)}
    for r in range(nc):
        acc = None
        for k in range(CONV_K):
            j = CONV_K - 1 - k
            tap = cls[r - j] if r >= j else prev[r - j + nc]
            term = tap * cw_ref[k:k + 1, tcols]
            acc = term if acc is None else acc + term
        out_ref[out_tile, pl.ds(r, n, stride=nc), :] = post(acc)
    tail_ref[:, tcols] = new_tail


def _inproj_stream(x_ref, nw_ref, wa_ref, wb_ref, rgp_ref, rgw_ref, gp_ref, gdcw_ref, p_ref,
                   gdo_ref, rgo_ref, hn_s, raw_s, raw_z, xc_s, rg_tail, gd_tail):
    x = x_ref[...]
    r = lax.rsqrt(jnp.mean(x * x, axis=-1, keepdims=True) + NORM_EPS)
    hn_s[...] = ((x * r) * nw_ref[...]).astype(BF16)

    H = W // 2

    def proj(col, width):
        if col < WCOL_SPLIT:
            w = wa_ref[:, col:col + width]
        else:
            w = wb_ref[:, col - WCOL_SPLIT:col - WCOL_SPLIT + width]
        return jnp.dot(hn_s[...], w, preferred_element_type=F32)

    def mm(dst_ref, dst, src, scale=None):
        def run():
            y = proj(src, H)
            dst_ref[:, dst:dst + H] = y if scale is None else y * scale
        return run

    def mm_raw(tile0, src):
        def run():
            y = proj(src, H)
            for i in range(H // LANES):
                raw_s[tile0 + i] = y[:, i * LANES:(i + 1) * LANES]
        return run

    def xc_half(c):
        return jnp.concatenate([xc_s[c // LANES + i] for i in range(H // LANES)], axis=1)

    def mm_rg_gates(c):
        def run():
            g = jnp.dot(xc_half(c).astype(BF16), rgw_ref[c // H],
                        preferred_element_type=F32)
            p_ref[:, OFF_RG_B + c:OFF_RG_B + c + H] = g[:, :H]
            raw_z[:, c:c + H] = g[:, H:]
        return run

    def ep_rg_conv(c):
        def run():
            for t in range(c // LANES, (c + H) // LANES):
                bias = rgp_ref[4:5, t * LANES:(t + 1) * LANES]
                _causal_conv(raw_s, RAW_RG + t, rg_tail, t * LANES, rgp_ref, xc_s, t,
                             lambda y, bias=bias: y + bias)
        return run

    def ep_rg_ab(c):
        def run():
            gr = p_ref[:, OFF_RG_B + c:OFF_RG_B + c + H] + rgp_ref[5:6, c:c + H]
            gi = raw_z[:, c:c + H] + rgp_ref[6:7, c:c + H]
            log_a = (-RG_C * _sigmoid(gr)) * _softplus(-rgp_ref[7:8, c:c + H])
            a = jnp.exp(log_a)
            b = jnp.sqrt(-jnp.tanh(log_a) * (a * a + 1.0)) * (_sigmoid(gi) * xc_half(c))
            for i in range(H // LANES):
                rgo_ref[RGT_A + c // LANES + i] = a[:, i * LANES:(i + 1) * LANES]
                rgo_ref[RGT_B + c // LANES + i] = b[:, i * LANES:(i + 1) * LANES]
        return run

    def ep_rg_gate(c):
        def run():
            g = _silu(p_ref[:, OFF_RG_G + c:OFF_RG_G + c + H])
            for i in range(H // LANES):
                rgo_ref[RGT_G + c // LANES + i] = g[:, i * LANES:(i + 1) * LANES]
        return run

    def ep_silu(off):
        def run():
            p_ref[:, off:off + H] = _silu(p_ref[:, off:off + H])
        return run

    def ep_gd_conv(tcol, norm_scale):
        def post(y):
            y = _silu(y)
            if norm_scale is None:
                return y
            y = y * lax.rsqrt(jnp.sum(y * y, axis=-1, keepdims=True) + NORM_EPS)
            return y if norm_scale == 1.0 else y * norm_scale

        def run():
            for t in range(tcol // LANES, (tcol + H) // LANES):
                _causal_conv(raw_s, RAW_GD + t, gd_tail, t * LANES, gdcw_ref, gdo_ref, t, post)
        return run

    def ep_ml_gate(c):
        def run():
            p_ref[:, OFF_ML_G + c:OFF_ML_G + c + H] = (
                _sigmoid(p_ref[:, OFF_ML_G + c:OFF_ML_G + c + H]) * _silu(raw_z[:, c:c + H]))
        return run

    def mm_gates():
        p_ref[:, OFF_GATES:OFF_GATES + LANES] = proj(WCOL_GATES, LANES)

    def ep_gates():
        gates = p_ref[:, OFF_GATES:OFF_GATES + LANES] + gp_ref[0:1, :]
        lane = lax.broadcasted_iota(jnp.int32, gates.shape, 1)
        is_i = lane < LANE_ML_F
        is_f = (lane >= LANE_ML_F) & (lane < LANE_ML_F + ML_HEADS)
        is_a = (lane >= LANE_GD_A) & (lane < LANE_GD_A + GD_HEADS)
        log_f = -_softplus(-gates)
        g_gd = -jnp.exp(gp_ref[1:2, :]) * _softplus(gates)
        p_ref[:, OFF_GATES:OFF_GATES + LANES] = jnp.where(
            is_i, gates, jnp.where(is_f, log_f, jnp.where(is_a, g_gd, _sigmoid(gates))))

    units = []
    for c in range(0, W, H):
        units += [
            ([mm_raw(RAW_RG + c // LANES, WCOL_RG_X + c)], ep_rg_conv(c)),
            ([mm_raw(RAW_GD + c // LANES, WCOL_GD_QKV + c)], ep_gd_conv(c, GD_DK ** -0.5)),
            ([mm(p_ref, OFF_ML_Q + c, WCOL_ML_Q + c, ML_DH ** -0.5)], None),
            ([mm_rg_gates(c)], ep_rg_ab(c)),
            ([mm_raw(RAW_GD + (W + c) // LANES, WCOL_GD_QKV + W + c)], ep_gd_conv(W + c, 1.0)),
            ([mm(p_ref, OFF_ML_K + c, WCOL_ML_K + c)], None),
            ([mm_raw(RAW_GD + (2 * W + c) // LANES, WCOL_GD_QKV + 2 * W + c)],
             ep_gd_conv(2 * W + c, None)),
            ([mm(p_ref, OFF_ML_V + c, WCOL_ML_V + c)], None),
            ([mm(p_ref, OFF_RG_G + c, WCOL_RG_Z + c)], ep_rg_gate(c)),
            ([mm(p_ref, OFF_ML_G + c, WCOL_ML_O + c), mm(raw_z, c, WCOL_ML_Z + c)], ep_ml_gate(c)),
            ([mm(p_ref, OFF_GD_G + c, WCOL_GD_Z + c)], ep_silu(OFF_GD_G + c)),
        ]
    units.append(([mm_gates], ep_gates))
    pending = None
    for mms, ep in units:
        for run in mms:
            run()
        if pending is not None:
            pending()
        pending = ep
        yield
    if pending is not None:
        pending()


def _rglru(rgo_ref, yrg_s, y_ref, rg_h, project):
    nc = RG_CLASSES
    ts = rgo_ref.shape[1]
    n = ts // nc
    ng = n // SUBLANES
    tiles = range(RG_TILES)
    ld = lambda t, r: rgo_ref[t, pl.ds(r, n, stride=nc), :]
    pa, pb = {}, {}
    for lt in tiles:
        pa[lt, 0] = ld(RGT_A + lt, 0)
        pb[lt, 0] = ld(RGT_B + lt, 0)
        for r in range(1, nc):
            a = ld(RGT_A + lt, r)
            pb[lt, r] = a * pb[lt, r - 1] + ld(RGT_B + lt, r)
            pa[lt, r] = a * pa[lt, r - 1]
    yield
    sub = lax.broadcasted_iota(jnp.int32, (ng, SUBLANES, LANES), 1)
    a3 = {lt: pa[lt, nc - 1].reshape(ng, SUBLANES, LANES) for lt in tiles}
    b3 = {lt: pb[lt, nc - 1].reshape(ng, SUBLANES, LANES) for lt in tiles}
    d = 1
    while d < SUBLANES:
        for lt in tiles:
            a_sh = pltpu.roll(a3[lt], d, 1)
            b_sh = pltpu.roll(b3[lt], d, 1)
            valid = sub >= d
            b3[lt] = jnp.where(valid, a3[lt] * b_sh + b3[lt], b3[lt])
            a3[lt] = jnp.where(valid, a3[lt] * a_sh, a3[lt])
        d *= 2
    yield
    first = lax.broadcasted_iota(jnp.int32, (n, LANES), 0) == 0
    for lt in tiles:
        cols = slice(lt * LANES, (lt + 1) * LANES)
        carry_in = rg_h[0:1, cols]
        carry = carry_in
        groups = []
        for g in range(ng):
            hg = b3[lt][g] + a3[lt][g] * carry
            groups.append(hg)
            carry = hg[SUBLANES - 1:SUBLANES, :]
        rg_h[:, cols] = jnp.broadcast_to(carry, (SUBLANES, LANES))
        h_last = jnp.concatenate(groups, axis=0)
        h_in = jnp.where(first, carry_in, pltpu.roll(h_last, 1, 0))
        for r in range(nc):
            h = h_last if r == nc - 1 else pb[lt, r] + pa[lt, r] * h_in
            yrg_s[lt, pl.ds(r, n, stride=nc), :] = h * ld(RGT_G + lt, r)
    yield
    y_ref[:, 0:RG_WIDTH] = jnp.concatenate([yrg_s[lt] for lt in tiles], axis=1).astype(y_ref.dtype)
    project(0)


def _gate_tables(p_ref, cs_ref):
    ts = p_ref.shape[0]
    gates = p_ref[:, OFF_GATES:OFF_GATES + LANES]
    lane = lax.broadcasted_iota(jnp.int32, (ts, LANES), 1)
    is_i = lane < LANE_ML_F
    is_f = (lane >= LANE_ML_F) & (lane < LANE_ML_F + ML_HEADS)
    is_a = (lane >= LANE_GD_A) & (lane < LANE_GD_A + GD_HEADS)
    x = jnp.where(is_f | is_a, gates, 0.0)
    hi, mid, lo = _split3(x)
    packed = (hi.astype(F32) + pltpu.roll(mid.astype(F32), GATE_LANES, 1)
              + pltpu.roll(lo.astype(F32), 2 * GATE_LANES, 1))
    cs = jnp.dot(cs_ref[...], packed.astype(BF16), preferred_element_type=F32)
    cs = cs + pltpu.roll(cs, LANES - GATE_LANES, 1) + pltpu.roll(cs, LANES - 2 * GATE_LANES, 1)
    csum_gd = cs[:ts]
    csum_ml = cs[ts:]
    z = jnp.where(is_i, gates, jnp.where(is_f, csum_ml, jnp.where(is_a, csum_gd, 0.0)))
    zs = jnp.concatenate(_split3(z), axis=0)
    sel = (lax.broadcasted_iota(jnp.int32, (GATE_LANES, LANES), 0)
           == lax.broadcasted_iota(jnp.int32, (GATE_LANES, LANES), 1)).astype(BF16)
    zt = lax.dot_general(sel, zs, (((1,), (1,)), ((), ())), preferred_element_type=F32)
    rows = zt[:, :ts] + zt[:, ts:2 * ts] + zt[:, 2 * ts:]
    return gates, csum_gd, csum_ml, rows


def _mlstm(p_ref, mlnw_ref, y_ref, ml_c, ml_m, gates, csum_ml, rows, project):
    ts = p_ref.shape[0]
    L = PAIR
    r_i = lax.broadcasted_iota(jnp.int32, (L, L), 0)
    c_i = lax.broadcasted_iota(jnp.int32, (L, L), 1)
    causal = r_i >= c_i
    ones_blk = jnp.ones((L, ML_DH), F32)
    units = [(h, c) for c in range(ts // L) for h in range(ML_HEADS)]

    st = {}
    for h, c in units:
        rs = slice(c * L, (c + 1) * L)
        q = p_ref[rs, OFF_ML_Q + h * ML_DH:OFF_ML_Q + (h + 1) * ML_DH]
        k = p_ref[rs, OFF_ML_K + h * ML_DH:OFF_ML_K + (h + 1) * ML_DH]
        li_c = gates[rs, LANE_ML_I + h:LANE_ML_I + h + 1]
        bq_c = csum_ml[rs, LANE_ML_F + h:LANE_ML_F + h + 1]
        r_row = (rows[LANE_ML_I + h:LANE_ML_I + h + 1, rs]
                 - rows[LANE_ML_F + h:LANE_ML_F + h + 1, rs])
        dm = jnp.where(causal, bq_c + r_row, -1e30)
        mx = jnp.max(dm, axis=-1, keepdims=True)
        g_last = bq_c[L - 1:L, :]
        ws = g_last - bq_c + li_c
        mws = jnp.max(ws, axis=0, keepdims=True)
        st[h, c] = dict(q=q, k=k, bq_c=bq_c, mx=mx, g_last=g_last, mws=mws,
                        pm=jnp.exp(dm - mx), wk=jnp.exp(ws - mws), qk=_dot_nt(q, k))
    yield
    for h, c in units:
        u = st[h, c]
        rs = slice(c * L, (c + 1) * L)
        v = p_ref[rs, OFF_ML_V + h * ML_DH:OFF_ML_V + (h + 1) * ML_DH]
        v_aug = jnp.concatenate([v, ones_blk], axis=1)
        u["intra"] = _dot(u["qk"] * u["pm"], v_aug)
        u["kv"] = _dot_tn(u["k"] * u["wk"], v_aug)

    yield
    c_aug = [ml_c[h] for h in range(ML_HEADS)]
    m_prev = [ml_m[h][0:1, 0:1] for h in range(ML_HEADS)]
    outs = [[] for _ in range(ML_HEADS)]
    for c in range(ts // L):
        qc = [_dot(st[h, c]["q"], c_aug[h]) for h in range(ML_HEADS)]
        for h in range(ML_HEADS):
            u = st[h, c]
            m_inter = u["bq_c"] + m_prev[h]
            m_t = jnp.maximum(m_inter, u["mx"])
            na = jnp.exp(u["mx"] - m_t) * u["intra"] + jnp.exp(m_inter - m_t) * qc[h]
            outs[h].append(na[:, :ML_DH] / jnp.maximum(jnp.abs(na[:, ML_DH:]), jnp.exp(-m_t)))
            m_new = jnp.maximum(u["g_last"] + m_prev[h], u["mws"])
            c_aug[h] = (jnp.exp(u["g_last"] + m_prev[h] - m_new) * c_aug[h]
                        + jnp.exp(u["mws"] - m_new) * u["kv"])
            m_prev[h] = m_new
        yield
    for h in range(ML_HEADS):
        sl = slice(h * ML_DH, (h + 1) * ML_DH)
        ml_c[h] = c_aug[h]
        ml_m[h] = jnp.broadcast_to(m_prev[h], ml_m.shape[1:])
        hh = jnp.concatenate(outs[h], axis=0)
        hn = hh * lax.rsqrt(jnp.mean(hh * hh, axis=-1, keepdims=True) + NORM_EPS)
        y_ml = (hn * mlnw_ref[0:1, sl]) * p_ref[:, OFF_ML_G + h * ML_DH:OFF_ML_G + (h + 1) * ML_DH]
        y_ref[:, RG_WIDTH + h * ML_DH:RG_WIDTH + (h + 1) * ML_DH] = y_ml.astype(y_ref.dtype)
    project(RG_WIDTH)


def _gdn(p_ref, gdo_ref, gdnw_ref, y_ref, gd_s, gates, csum_gd, rows, project):
    ts = p_ref.shape[0]
    L = PAIR
    C = GD_CHUNK
    r_i = lax.broadcasted_iota(jnp.int32, (L, L), 0)
    c_i = lax.broadcasted_iota(jnp.int32, (L, L), 1)
    same = (r_i >= C) == (c_i >= C)
    incl = same & (r_i >= c_i)
    strict = same & (r_i > c_i)
    eye = (r_i == c_i).astype(F32)
    second = lax.broadcasted_iota(jnp.int32, (L, 1), 0) >= C
    zeros_c = jnp.zeros((C, GD_DV), F32)
    units = [(h, j) for j in range(ts // L) for h in range(GD_HEADS)]

    st = {}
    for h, j in units:
        rs = slice(j * L, (j + 1) * L)
        q = gdo_ref[h, rs, :]
        k = gdo_ref[GD_HEADS + h, rs, :]
        beta = gates[rs, LANE_GD_B + h:LANE_GD_B + h + 1]
        gc_c = csum_gd[rs, LANE_GD_A + h:LANE_GD_A + h + 1]
        gc_r = rows[LANE_GD_A + h:LANE_GD_A + h + 1, rs]
        gam = jnp.where(incl, jnp.exp(jnp.where(incl, gc_c - gc_r, 0.0)), 0.0)
        kb = k * beta
        egc = jnp.exp(gc_c)
        g_first = gc_c[C - 1:C, :]
        g_second = gc_c[L - 1:L, :]
        st[h, j] = dict(q=q, k=k, kb=kb, beta=beta, gam=gam, egc=egc, q_dec=q * egc,
                        g_last=(g_first, g_second),
                        k_dec=k * jnp.exp(jnp.where(second, g_second, g_first) - gc_c))

    zeros_l = jnp.zeros((L, L), F32)
    side = lambda a, b: jnp.concatenate([a, b], axis=1)
    bdiag = lambda a, b: jnp.concatenate([side(a, zeros_l), side(zeros_l, b)], axis=0)
    pairs = [(st[a, j], st[a + 1, j]) for j in range(ts // L) for a in range(0, GD_HEADS, 2)]

    for ua, ub in pairs:
        lhs = jnp.concatenate([side(ua["kb"], ub["kb"]), side(ua["q"], ub["q"])], axis=0)
        kq = _dot_nt(lhs, bdiag(ua["k"], ub["k"]))
        for u, cs_ in ((ua, slice(0, L)), (ub, slice(L, 2 * L))):
            u["aqk"] = kq[L:, cs_] * u["gam"]
            u["pw"] = jnp.where(strict, -(kq[:L, cs_] * u["gam"]), 0.0)
            u["t_inv"] = eye + u["pw"]
    yield
    for ua, ub in pairs:
        sq = _dot(side(ua["pw"], ub["pw"]), bdiag(ua["pw"], ub["pw"]))
        ua["pw"], ub["pw"] = sq[:, :L], sq[:, L:]
    yield
    span = 4
    while span < C:
        for ua, ub in pairs:
            lhs = jnp.concatenate([side(ua["pw"], ub["pw"]), side(ua["t_inv"], ub["t_inv"])],
                                  axis=0)
            both = _dot(lhs, bdiag(ua["pw"], ub["pw"]))
            ua["pw"], ub["pw"] = both[:L, :L], both[:L, L:]
            ua["t_inv"] = ua["t_inv"] + both[L:, :L]
            ub["t_inv"] = ub["t_inv"] + both[L:, L:]
        span *= 2
        yield
    for ua, ub in pairs:
        last = _dot(side(ua["t_inv"], ub["t_inv"]), bdiag(ua["pw"], ub["pw"]))
        ua["t_inv"] = ua["t_inv"] + last[:, :L]
        ub["t_inv"] = ub["t_inv"] + last[:, L:]
    yield
    for h, j in units:
        u = st[h, j]
        rs = slice(j * L, (j + 1) * L)
        v = gdo_ref[2 * GD_HEADS + h, rs, :]
        uw = _dot(u["t_inv"], jnp.concatenate([v * u["beta"], u["kb"] * u["egc"]], axis=1))
        u["u"] = uw[:, :GD_DV]
        u["w"] = uw[:, GD_DV:]

    yield
    state = [gd_s[h] for h in range(GD_HEADS)]
    outs = [[] for _ in range(GD_HEADS)]
    for j in range(ts // L):
        for half in range(2):
            hs = slice(half * C, (half + 1) * C)
            wq = [_dot(jnp.concatenate([st[h, j]["w"][hs], st[h, j]["q_dec"][hs]], axis=0), state[h])
                  for h in range(GD_HEADS)]
            v_new = [st[h, j]["u"][hs] - wq[h][:C] for h in range(GD_HEADS)]
            for h in range(GD_HEADS):
                v_pad = jnp.concatenate([v_new[h], zeros_c] if half == 0 else [zeros_c, v_new[h]],
                                        axis=0)
                outs[h].append(wq[h][C:] + _dot(st[h, j]["aqk"][hs], v_pad))
            for h in range(GD_HEADS):
                u = st[h, j]
                state[h] = (state[h] * jnp.exp(u["g_last"][half])
                            + _dot_tn(u["k_dec"][hs], v_new[h]))
            yield
    for h in range(GD_HEADS):
        gd_s[h] = state[h]
        o = jnp.concatenate(outs[h], axis=0)
        on = o * lax.rsqrt(jnp.mean(o * o, axis=-1, keepdims=True) + NORM_EPS)
        y_gd = on * gdnw_ref[0:1, :] * p_ref[:, OFF_GD_G + h * GD_DV:OFF_GD_G + (h + 1) * GD_DV]
        y_ref[:, RG_WIDTH + ML_WIDTH + h * GD_DV:RG_WIDTH + ML_WIDTH + (h + 1) * GD_DV] = (
            y_gd.astype(y_ref.dtype))
    project(RG_WIDTH + ML_WIDTH)


def _interleave(order, stages):
    live = dict(stages)

    def step(key):
        if key in live:
            try:
                next(live[key])
            except StopIteration:
                del live[key]

    for key in order:
        step(key)
    while live:
        for key in list(live):
            step(key)


def _layer_kernel(xn_ref, xc_ref, nw_ref, wa_ref, wb_ref, rgp_ref, rgw_ref, gp_ref, gdcw_ref,
                  wo_ref, fw_ref, mlnw_ref, gdnw_ref, cs_ref, o_ref,
                  p_s, gdo_s, rgo_s, yrg_s, hn_s, raw_s, raw_z, xc_s, rg_tail, gd_tail,
                  y_ref, rg_h, ml_c, ml_m, gd_s,
                  *, blocks_per_seq, final_norm):
    i = pl.program_id(0)
    slot = lax.rem(i, 2)
    p_new = p_s.at[slot]
    p_cur = p_s.at[1 - slot]

    @pl.when(i == 0)
    def _():
        p_s[1] = jnp.zeros(p_s.shape[1:], p_s.dtype)
        gdo_s[1] = jnp.zeros(gdo_s.shape[1:], gdo_s.dtype)
        rgo_s[1] = jnp.zeros(rgo_s.shape[1:], rgo_s.dtype)

    @pl.when(lax.rem(i, blocks_per_seq) == 0)
    def _():
        rg_tail[...] = jnp.zeros_like(rg_tail)
        gd_tail[...] = jnp.zeros_like(gd_tail)

    @pl.when(lax.rem(jnp.maximum(i - 1, 0), blocks_per_seq) == 0)
    def _():
        rg_h[...] = jnp.zeros_like(rg_h)
        ml_c[...] = jnp.zeros_like(ml_c)
        ml_m[...] = jnp.zeros_like(ml_m)
        gd_s[...] = jnp.zeros_like(gd_s)

    o_ref[...] = xc_ref[...]

    def project(k0):
        o_ref[...] += jnp.dot(y_ref[:, k0:k0 + W], wo_ref[k0:k0 + W, :],
                              preferred_element_type=F32)

    gates, csum_gd, csum_ml, rows = _gate_tables(p_cur, cs_ref)
    _interleave(LAYER_ORDER, dict(
        P=_inproj_stream(xn_ref, nw_ref, wa_ref, wb_ref, rgp_ref, rgw_ref, gp_ref, gdcw_ref,
                         p_new, gdo_s.at[slot], rgo_s.at[slot], hn_s, raw_s, raw_z, xc_s,
                         rg_tail, gd_tail),
        G=_gdn(p_cur, gdo_s.at[1 - slot], gdnw_ref, y_ref, gd_s, gates, csum_gd, rows, project),
        M=_mlstm(p_cur, mlnw_ref, y_ref, ml_c, ml_m, gates, csum_ml, rows, project),
        R=_rglru(rgo_s.at[1 - slot], yrg_s, y_ref, rg_h, project)))
    if final_norm:
        xn = o_ref[...]
        r = lax.rsqrt(jnp.mean(xn * xn, axis=-1, keepdims=True) + NORM_EPS)
        o_ref[...] = (xn * r) * fw_ref[...]


def _cumsum_matrix(ts):
    r = jnp.arange(ts)[:, None]
    c = jnp.arange(ts)[None, :]
    m64 = (r >= c) & (r // GD_CHUNK == c // GD_CHUNK)
    m128 = (r >= c) & (r // PAIR == c // PAIR)
    return jnp.concatenate([m64, m128], axis=0).astype(BF16)


def _layer(x2, norm_w, w_a, w_b, rgp, rgw, gp, gdcw, w_out, final_w, mlnw, gdnw, cs, layer, seq,
           final_norm):
    t = x2.shape[0]
    ts = TS_BLOCK
    nb = t // ts
    per_layer = lambda a, **kw: pl.BlockSpec((None,) + a.shape[1:],
                                             lambda i: (layer,) + (0,) * (a.ndim - 1), **kw)
    once = dict(pipeline_mode=pl.Buffered(1))
    return pl.pallas_call(
        functools.partial(_layer_kernel, blocks_per_seq=seq // ts, final_norm=final_norm),
        grid=(nb + 1,),
        in_specs=[pl.BlockSpec((ts, D_MODEL), lambda i: (jnp.minimum(i, nb - 1), 0)),
                  pl.BlockSpec((ts, D_MODEL), lambda i: (jnp.maximum(i - 1, 0), 0)),
                  per_layer(norm_w),
                  pl.BlockSpec((None, D_MODEL, WCOL_SPLIT), lambda i: (layer, 0, 0), **once),
                  per_layer(w_b, **once), per_layer(rgp), per_layer(rgw), per_layer(gp),
                  per_layer(gdcw), per_layer(w_out, **once),
                  pl.BlockSpec(final_w.shape, lambda i: (0, 0)),
                  per_layer(mlnw), per_layer(gdnw), pl.BlockSpec(cs.shape, lambda i: (0, 0))],
        out_specs=pl.BlockSpec((ts, D_MODEL), lambda i: (jnp.maximum(i - 1, 0), 0)),
        out_shape=jax.ShapeDtypeStruct((t, D_MODEL), F32),
        scratch_shapes=[pltpu.VMEM((2, ts, D_PROJ), F32),
                        pltpu.VMEM((2, GD_CONV // LANES, ts, LANES), F32),
                        pltpu.VMEM((2, 3 * RG_TILES, ts, LANES), F32),
                        pltpu.VMEM((RG_TILES, ts, LANES), F32),
                        pltpu.VMEM((ts, D_MODEL), BF16),
                        pltpu.VMEM((RAW_TILES, ts, LANES), F32),
                        pltpu.VMEM((ts, W), F32),
                        pltpu.VMEM((RG_WIDTH // LANES, ts, LANES), F32),
                        pltpu.VMEM((SUBLANES, RG_WIDTH), F32),
                        pltpu.VMEM((SUBLANES, GD_CONV), F32),
                        pltpu.VMEM((ts, D_MIX), BF16),
                        pltpu.VMEM((SUBLANES, RG_WIDTH), F32),
                        pltpu.VMEM((ML_HEADS, ML_DH, 2 * ML_DH), F32),
                        pltpu.VMEM((ML_HEADS, SUBLANES, LANES), F32),
                        pltpu.VMEM((GD_HEADS, GD_DK, GD_DV), F32)],
        compiler_params=pltpu.CompilerParams(dimension_semantics=("arbitrary",),
                                             vmem_limit_bytes=VMEM_LIMIT),
        name="layer",
    )(x2, x2, norm_w, w_a, w_b, rgp, rgw, gp, gdcw, w_out, final_w, mlnw, gdnw, cs)


def _w_in_tail(w_in):
    pad = jnp.zeros(w_in.shape[:-1] + (LANES - GATE_LANES,), w_in.dtype)
    return jnp.concatenate([
        w_in[..., SRC_GD:SRC_GD_GATES], w_in[..., SRC_ML_GATES:SRC_GD],
        w_in[..., SRC_GD_GATES:], pad], axis=-1)


def _rg_gate_blockdiag(gate_w):
    depth = gate_w.shape[0]
    nb = RG_BLOCKS // 2
    eye = jnp.eye(nb, dtype=gate_w.dtype)
    halves = []
    for hf in range(2):
        blk = gate_w[:, :, hf * nb:(hf + 1) * nb]
        bd = jnp.einsum('lgncd,nm->lgncmd', blk, eye)
        bd = bd.reshape(depth, 2, nb * RG_BLOCK, nb * RG_BLOCK)
        halves.append(jnp.concatenate([bd[:, 0], bd[:, 1]], axis=-1))
    return jnp.stack(halves, axis=1)


def _gate_params(ml_gate_b, gd_dt_bias, gd_a_log):
    depth = ml_gate_b.shape[0]
    zeros = lambda n: jnp.zeros((depth, n), F32)
    row0 = jnp.concatenate([ml_gate_b[:, 0], ml_gate_b[:, 1], gd_dt_bias,
                            zeros(LANES - LANE_GD_B)], axis=-1)
    row1 = jnp.concatenate([zeros(LANE_GD_A), gd_a_log, zeros(LANES - LANE_GD_B)], axis=-1)
    rest = jnp.zeros((depth, SUBLANES - 2, LANES), F32)
    return jnp.concatenate([row0[:, None], row1[:, None], rest], axis=1)


def kernel(x, norm_w, w_in, rg_conv_w, rg_conv_b, rg_gate_w, rg_gate_b, rg_lambda, ml_gate_b,
           ml_norm_w, gd_conv_w, gd_a_log, gd_dt_bias, gd_norm_w, w_out, final_norm_w):
    bsz, seq, _ = x.shape
    depth = w_in.shape[0]
    t = bsz * seq
    w_a = w_in.astype(BF16)
    w_b = _w_in_tail(w_a)
    w_o = w_out.astype(BF16)
    rgp = jnp.concatenate([rg_conv_w, rg_conv_b[:, None], rg_gate_b, rg_lambda[:, None]],
                          axis=1)
    rgw = _rg_gate_blockdiag(rg_gate_w).astype(BF16)
    gp = _gate_params(ml_gate_b, gd_dt_bias, gd_a_log)
    cs = _cumsum_matrix(TS_BLOCK)

    x2 = x.reshape(t, D_MODEL)
    for l in range(depth):
        x2 = _layer(x2, norm_w[:, None], w_a, w_b, rgp, rgw, gp, gd_conv_w, w_o, final_norm_w[None],
                    ml_norm_w[:, None], gd_norm_w[:, None], cs, l, seq, l == depth - 1)
    return x2.reshape(bsz, seq, D_MODEL)
```

```python
import functools

import jax
import jax.numpy as jnp
from jax import lax
from jax.experimental import pallas as pl
from jax.experimental.pallas import tpu as pltpu

F32 = jnp.float32
BF16 = jnp.bfloat16

D_MODEL = 1024
CONV_K = 4
CONV_CLASSES = 4
NORM_EPS = 1e-6
RG_WIDTH = 512
RG_BLOCKS = 8
RG_BLOCK = RG_WIDTH // RG_BLOCKS
RG_C = 8.0
ML_HEADS = 4
ML_DH = 128
ML_WIDTH = ML_HEADS * ML_DH
GD_HEADS = 4
GD_DK = 128
GD_DV = 128
GD_QK = GD_HEADS * GD_DK
GD_WIDTH = GD_HEADS * GD_DV
GD_CONV = 2 * GD_QK + GD_WIDTH
D_MIX = RG_WIDTH + ML_WIDTH + GD_WIDTH
GD_CHUNK = 64
PAIR = 2 * GD_CHUNK
SUBLANES = 8
LANES = 128

W = 512
WCOL_RG_X, WCOL_RG_Z = 0 * W, 1 * W
WCOL_ML_Q, WCOL_ML_K, WCOL_ML_V, WCOL_ML_O, WCOL_ML_Z = 2 * W, 3 * W, 4 * W, 5 * W, 6 * W
WCOL_GD_QKV, WCOL_GD_Z = 7 * W, 10 * W
WCOL_GATES = 11 * W
WCOL_SPLIT = WCOL_GD_QKV
D_PROJ = WCOL_GATES + LANES
OFF_RG_A, OFF_RG_B, OFF_RG_G = 0 * W, 1 * W, 2 * W
OFF_ML_Q, OFF_ML_K, OFF_ML_V, OFF_ML_G = 3 * W, 4 * W, 5 * W, 6 * W
OFF_GD_Q, OFF_GD_K, OFF_GD_V, OFF_GD_G = 7 * W, 8 * W, 9 * W, 10 * W
OFF_GATES = 11 * W
LANE_ML_I, LANE_ML_F, LANE_GD_A, LANE_GD_B = 0, 4, 8, 12
GATE_LANES = 16
RAW_RG, RAW_GD = 0, RG_WIDTH // LANES
RAW_TILES = RAW_GD + GD_CONV // LANES
RG_TILES = RG_WIDTH // LANES
RGT_A, RGT_B, RGT_G = 0, RG_TILES, 2 * RG_TILES
RG_CLASSES = 4
SRC_ML_GATES = 2 * RG_WIDTH + 5 * ML_WIDTH
SRC_GD = SRC_ML_GATES + 2 * ML_HEADS
SRC_GD_GATES = SRC_GD + 2 * GD_QK + 2 * GD_WIDTH

VMEM_LIMIT = 56 * 1024 * 1024
TS_BLOCK = 256
LAYER_ORDER = "GPP" * 9 + "MRPG" * 5


def _dot(a, b):
    return jnp.dot(a.astype(BF16), b.astype(BF16), preferred_element_type=F32)


def _dot_nt(a, b):
    return lax.dot_general(a.astype(BF16), b.astype(BF16), (((1,), (1,)), ((), ())),
                           preferred_element_type=F32)


def _dot_tn(a, b):
    return lax.dot_general(a.astype(BF16), b.astype(BF16), (((0,), (0,)), ((), ())),
                           preferred_element_type=F32)


def _split3(x):
    hi = x.astype(BF16)
    r1 = x - hi.astype(F32)
    mid = r1.astype(BF16)
    lo = (r1 - mid.astype(F32)).astype(BF16)
    return hi, mid, lo


def _softplus(x):
    return jnp.maximum(x, 0.0) + jnp.log1p(jnp.exp(-jnp.abs(x)))


def _sigmoid(x):
    return 1.0 / (1.0 + jnp.exp(-x))


def _silu(x):
    return x * _sigmoid(x)


def _causal_conv(x_ref, tile, tail_ref, tcol0, cw_ref, out_ref, out_tile, post):
    ts = x_ref.shape[1]
    nc = CONV_CLASSES
    n = ts // nc
    tcols = slice(tcol0, tcol0 + LANES)
    tail = tail_ref[:, tcols]
    new_tail = x_ref[tile, ts - SUBLANES:ts, :]
    cls = [x_ref[tile, pl.ds(r, n, stride=nc), :] for r in range(nc)]
    first = lax.broadcasted_iota(jnp.int32, (n, LANES), 0) == 0
    prev = {r: jnp.where(first, tail[SUBLANES - nc + r:SUBLANES - nc + r + 1, :],
                         pltpu.roll(cls[r], 1, 0))
            for r in range(nc - CONV_K + 1, nc)}
    for r in range(nc):
        acc = None
        for k in range(CONV_K):
            j = CONV_K - 1 - k
            tap = cls[r - j] if r >= j else prev[r - j + nc]
            term = tap * cw_ref[k:k + 1, tcols]
            acc = term if acc is None else acc + term
        out_ref[out_tile, pl.ds(r, n, stride=nc), :] = post(acc)
    tail_ref[:, tcols] = new_tail


def _inproj_stream(x_ref, nw_ref, wa_ref, wb_ref, rgp_ref, rgw_ref, gp_ref, gdcw_ref, p_ref,
                   gdo_ref, rgo_ref, hn_s, raw_s, raw_z, xc_s, rg_tail, gd_tail):
    x = x_ref[...]
    r = lax.rsqrt(jnp.mean(x * x, axis=-1, keepdims=True) + NORM_EPS)
    hn_s[...] = ((x * r) * nw_ref[...]).astype(BF16)

    H = W // 2

    def proj(col, width):
        if col < WCOL_SPLIT:
            w = wa_ref[:, col:col + width]
        else:
            w = wb_ref[:, col - WCOL_SPLIT:col - WCOL_SPLIT + width]
        return jnp.dot(hn_s[...], w, preferred_element_type=F32)

    def mm(dst_ref, dst, src, scale=None):
        def run():
            y = proj(src, H)
            dst_ref[:, dst:dst + H] = y if scale is None else y * scale
        return run

    def mm_raw(tile0, src):
        def run():
            y = proj(src, H)
            for i in range(H // LANES):
                raw_s[tile0 + i] = y[:, i * LANES:(i + 1) * LANES]
        return run

    def xc_half(c):
        return jnp.concatenate([xc_s[c // LANES + i] for i in range(H // LANES)], axis=1)

    def mm_rg_gates(c):
        def run():
            g = jnp.dot(xc_half(c).astype(BF16), rgw_ref[c // H],
                        preferred_element_type=F32)
            p_ref[:, OFF_RG_B + c:OFF_RG_B + c + H] = g[:, :H]
            raw_z[:, c:c + H] = g[:, H:]
        return run

    def ep_rg_conv(c):
        def run():
            for t in range(c // LANES, (c + H) // LANES):
                bias = rgp_ref[4:5, t * LANES:(t + 1) * LANES]
                _causal_conv(raw_s, RAW_RG + t, rg_tail, t * LANES, rgp_ref, xc_s, t,
                             lambda y, bias=bias: y + bias)
        return run

    def ep_rg_ab(c):
        def run():
            gr = p_ref[:, OFF_RG_B + c:OFF_RG_B + c + H] + rgp_ref[5:6, c:c + H]
            gi = raw_z[:, c:c + H] + rgp_ref[6:7, c:c + H]
            log_a = (-RG_C * _sigmoid(gr)) * _softplus(-rgp_ref[7:8, c:c + H])
            a = jnp.exp(log_a)
            b = jnp.sqrt(-jnp.tanh(log_a) * (a * a + 1.0)) * (_sigmoid(gi) * xc_half(c))
            for i in range(H // LANES):
                rgo_ref[RGT_A + c // LANES + i] = a[:, i * LANES:(i + 1) * LANES]
                rgo_ref[RGT_B + c // LANES + i] = b[:, i * LANES:(i + 1) * LANES]
        return run

    def ep_rg_gate(c):
        def run():
            g = _silu(p_ref[:, OFF_RG_G + c:OFF_RG_G + c + H])
            for i in range(H // LANES):
                rgo_ref[RGT_G + c // LANES + i] = g[:, i * LANES:(i + 1) * LANES]
        return run

    def ep_silu(off):
        def run():
            p_ref[:, off:off + H] = _silu(p_ref[:, off:off + H])
        return run

    def ep_gd_conv(tcol, norm_scale):
        def post(y):
            y = _silu(y)
            if norm_scale is None:
                return y
            y = y * lax.rsqrt(jnp.sum(y * y, axis=-1, keepdims=True) + NORM_EPS)
            return y if norm_scale == 1.0 else y * norm_scale

        def run():
            for t in range(tcol // LANES, (tcol + H) // LANES):
                _causal_conv(raw_s, RAW_GD + t, gd_tail, t * LANES, gdcw_ref, gdo_ref, t, post)
        return run

    def ep_ml_gate(c):
        def run():
            p_ref[:, OFF_ML_G + c:OFF_ML_G + c + H] = (
                _sigmoid(p_ref[:, OFF_ML_G + c:OFF_ML_G + c + H]) * _silu(raw_z[:, c:c + H]))
        return run

    def mm_gates():
        p_ref[:, OFF_GATES:OFF_GATES + LANES] = proj(WCOL_GATES, LANES)

    def ep_gates():
        gates = p_ref[:, OFF_GATES:OFF_GATES + LANES] + gp_ref[0:1, :]
        lane = lax.broadcasted_iota(jnp.int32, gates.shape, 1)
        is_i = lane < LANE_ML_F
        is_f = (lane >= LANE_ML_F) & (lane < LANE_ML_F + ML_HEADS)
        is_a = (lane >= LANE_GD_A) & (lane < LANE_GD_A + GD_HEADS)
        log_f = -_softplus(-gates)
        g_gd = -jnp.exp(gp_ref[1:2, :]) * _softplus(gates)
        p_ref[:, OFF_GATES:OFF_GATES + LANES] = jnp.where(
            is_i, gates, jnp.where(is_f, log_f, jnp.where(is_a, g_gd, _sigmoid(gates))))

    units = []
    for c in range(0, W, H):
        units += [
            ([mm_raw(RAW_RG + c // LANES, WCOL_RG_X + c)], ep_rg_conv(c)),
            ([mm_raw(RAW_GD + c // LANES, WCOL_GD_QKV + c)], ep_gd_conv(c, GD_DK ** -0.5)),
            ([mm(p_ref, OFF_ML_Q + c, WCOL_ML_Q + c, ML_DH ** -0.5)], None),
            ([mm_rg_gates(c)], ep_rg_ab(c)),
            ([mm_raw(RAW_GD + (W + c) // LANES, WCOL_GD_QKV + W + c)], ep_gd_conv(W + c, 1.0)),
            ([mm(p_ref, OFF_ML_K + c, WCOL_ML_K + c)], None),
            ([mm_raw(RAW_GD + (2 * W + c) // LANES, WCOL_GD_QKV + 2 * W + c)],
             ep_gd_conv(2 * W + c, None)),
            ([mm(p_ref, OFF_ML_V + c, WCOL_ML_V + c)], None),
            ([mm(p_ref, OFF_RG_G + c, WCOL_RG_Z + c)], ep_rg_gate(c)),
            ([mm(p_ref, OFF_ML_G + c, WCOL_ML_O + c), mm(raw_z, c, WCOL_ML_Z + c)], ep_ml_gate(c)),
            ([mm(p_ref, OFF_GD_G + c, WCOL_GD_Z + c)], ep_silu(OFF_GD_G + c)),
        ]
    units.append(([mm_gates], ep_gates))
    pending = None
    for mms, ep in units:
        for run in mms:
            run()
        if pending is not None:
            pending()
        pending = ep
        yield
    if pending is not None:
        pending()


def _rglru(rgo_ref, yrg_s, y_ref, rg_h, project):
    nc = RG_CLASSES
    ts = rgo_ref.shape[1]
    n = ts // nc
    ng = n // SUBLANES
    tiles = range(RG_TILES)
    ld = lambda t, r: rgo_ref[t, pl.ds(r, n, stride=nc), :]
    pa, pb = {}, {}
    for lt in tiles:
        pa[lt, 0] = ld(RGT_A + lt, 0)
        pb[lt, 0] = ld(RGT_B + lt, 0)
        for r in range(1, nc):
            a = ld(RGT_A + lt, r)
            pb[lt, r] = a * pb[lt, r - 1] + ld(RGT_B + lt, r)
            pa[lt, r] = a * pa[lt, r - 1]
    yield
    sub = lax.broadcasted_iota(jnp.int32, (ng, SUBLANES, LANES), 1)
    a3 = {lt: pa[lt, nc - 1].reshape(ng, SUBLANES, LANES) for lt in tiles}
    b3 = {lt: pb[lt, nc - 1].reshape(ng, SUBLANES, LANES) for lt in tiles}
    d = 1
    while d < SUBLANES:
        for lt in tiles:
            a_sh = pltpu.roll(a3[lt], d, 1)
            b_sh = pltpu.roll(b3[lt], d, 1)
            valid = sub >= d
            b3[lt] = jnp.where(valid, a3[lt] * b_sh + b3[lt], b3[lt])
            a3[lt] = jnp.where(valid, a3[lt] * a_sh, a3[lt])
        d *= 2
    yield
    first = lax.broadcasted_iota(jnp.int32, (n, LANES), 0) == 0
    for lt in tiles:
        cols = slice(lt * LANES, (lt + 1) * LANES)
        carry_in = rg_h[0:1, cols]
        carry = carry_in
        groups = []
        for g in range(ng):
            hg = b3[lt][g] + a3[lt][g] * carry
            groups.append(hg)
            carry = hg[SUBLANES - 1:SUBLANES, :]
        rg_h[:, cols] = jnp.broadcast_to(carry, (SUBLANES, LANES))
        h_last = jnp.concatenate(groups, axis=0)
        h_in = jnp.where(first, carry_in, pltpu.roll(h_last, 1, 0))
        for r in range(nc):
            h = h_last if r == nc - 1 else pb[lt, r] + pa[lt, r] * h_in
            yrg_s[lt, pl.ds(r, n, stride=nc), :] = h * ld(RGT_G + lt, r)
    yield
    y_ref[:, 0:RG_WIDTH] = jnp.concatenate([yrg_s[lt] for lt in tiles], axis=1).astype(y_ref.dtype)
    project(0)


def _gate_tables(p_ref, cs_ref):
    ts = p_ref.shape[0]
    gates = p_ref[:, OFF_GATES:OFF_GATES + LANES]
    lane = lax.broadcasted_iota(jnp.int32, (ts, LANES), 1)
    is_i = lane < LANE_ML_F
    is_f = (lane >= LANE_ML_F) & (lane < LANE_ML_F + ML_HEADS)
    is_a = (lane >= LANE_GD_A) & (lane < LANE_GD_A + GD_HEADS)
    x = jnp.where(is_f | is_a, gates, 0.0)
    hi, mid, lo = _split3(x)
    packed = (hi.astype(F32) + pltpu.roll(mid.astype(F32), GATE_LANES, 1)
              + pltpu.roll(lo.astype(F32), 2 * GATE_LANES, 1))
    cs = jnp.dot(cs_ref[...], packed.astype(BF16), preferred_element_type=F32)
    cs = cs + pltpu.roll(cs, LANES - GATE_LANES, 1) + pltpu.roll(cs, LANES - 2 * GATE_LANES, 1)
    csum_gd = cs[:ts]
    csum_ml = cs[ts:]
    z = jnp.where(is_i, gates, jnp.where(is_f, csum_ml, jnp.where(is_a, csum_gd, 0.0)))
    zs = jnp.concatenate(_split3(z), axis=0)
    sel = (lax.broadcasted_iota(jnp.int32, (GATE_LANES, LANES), 0)
           == lax.broadcasted_iota(jnp.int32, (GATE_LANES, LANES), 1)).astype(BF16)
    zt = lax.dot_general(sel, zs, (((1,), (1,)), ((), ())), preferred_element_type=F32)
    rows = zt[:, :ts] + zt[:, ts:2 * ts] + zt[:, 2 * ts:]
    return gates, csum_gd, csum_ml, rows


def _mlstm(p_ref, mlnw_ref, y_ref, ml_c, ml_m, gates, csum_ml, rows, project):
    ts = p_ref.shape[0]
    L = PAIR
    r_i = lax.broadcasted_iota(jnp.int32, (L, L), 0)
    c_i = lax.broadcasted_iota(jnp.int32, (L, L), 1)
    causal = r_i >= c_i
    ones_blk = jnp.ones((L, ML_DH), F32)
    units = [(h, c) for c in range(ts // L) for h in range(ML_HEADS)]

    st = {}
    for h, c in units:
        rs = slice(c * L, (c + 1) * L)
        q = p_ref[rs, OFF_ML_Q + h * ML_DH:OFF_ML_Q + (h + 1) * ML_DH]
        k = p_ref[rs, OFF_ML_K + h * ML_DH:OFF_ML_K + (h + 1) * ML_DH]
        li_c = gates[rs, LANE_ML_I + h:LANE_ML_I + h + 1]
        bq_c = csum_ml[rs, LANE_ML_F + h:LANE_ML_F + h + 1]
        r_row = (rows[LANE_ML_I + h:LANE_ML_I + h + 1, rs]
                 - rows[LANE_ML_F + h:LANE_ML_F + h + 1, rs])
        dm = jnp.where(causal, bq_c + r_row, -1e30)
        mx = jnp.max(dm, axis=-1, keepdims=True)
        g_last = bq_c[L - 1:L, :]
        ws = g_last - bq_c + li_c
        mws = jnp.max(ws, axis=0, keepdims=True)
        st[h, c] = dict(q=q, k=k, bq_c=bq_c, mx=mx, g_last=g_last, mws=mws,
                        pm=jnp.exp(dm - mx), wk=jnp.exp(ws - mws), qk=_dot_nt(q, k))
    yield
    for h, c in units:
        u = st[h, c]
        rs = slice(c * L, (c + 1) * L)
        v = p_ref[rs, OFF_ML_V + h * ML_DH:OFF_ML_V + (h + 1) * ML_DH]
        v_aug = jnp.concatenate([v, ones_blk], axis=1)
        u["intra"] = _dot(u["qk"] * u["pm"], v_aug)
        u["kv"] = _dot_tn(u["k"] * u["wk"], v_aug)

    yield
    c_aug = [ml_c[h] for h in range(ML_HEADS)]
    m_prev = [ml_m[h][0:1, 0:1] for h in range(ML_HEADS)]
    outs = [[] for _ in range(ML_HEADS)]
    for c in range(ts // L):
        qc = [_dot(st[h, c]["q"], c_aug[h]) for h in range(ML_HEADS)]
        for h in range(ML_HEADS):
            u = st[h, c]
            m_inter = u["bq_c"] + m_prev[h]
            m_t = jnp.maximum(m_inter, u["mx"])
            na = jnp.exp(u["mx"] - m_t) * u["intra"] + jnp.exp(m_inter - m_t) * qc[h]
            outs[h].append(na[:, :ML_DH] / jnp.maximum(jnp.abs(na[:, ML_DH:]), jnp.exp(-m_t)))
            m_new = jnp.maximum(u["g_last"] + m_prev[h], u["mws"])
            c_aug[h] = (jnp.exp(u["g_last"] + m_prev[h] - m_new) * c_aug[h]
                        + jnp.exp(u["mws"] - m_new) * u["kv"])
            m_prev[h] = m_new
        yield
    for h in range(ML_HEADS):
        sl = slice(h * ML_DH, (h + 1) * ML_DH)
        ml_c[h] = c_aug[h]
        ml_m[h] = jnp.broadcast_to(m_prev[h], ml_m.shape[1:])
        hh = jnp.concatenate(outs[h], axis=0)
        hn = hh * lax.rsqrt(jnp.mean(hh * hh, axis=-1, keepdims=True) + NORM_EPS)
        y_ml = (hn * mlnw_ref[0:1, sl]) * p_ref[:, OFF_ML_G + h * ML_DH:OFF_ML_G + (h + 1) * ML_DH]
        y_ref[:, RG_WIDTH + h * ML_DH:RG_WIDTH + (h + 1) * ML_DH] = y_ml.astype(y_ref.dtype)
    project(RG_WIDTH)


def _gdn(p_ref, gdo_ref, gdnw_ref, y_ref, gd_s, gates, csum_gd, rows, project):
    ts = p_ref.shape[0]
    L = PAIR
    C = GD_CHUNK
    r_i = lax.broadcasted_iota(jnp.int32, (L, L), 0)
    c_i = lax.broadcasted_iota(jnp.int32, (L, L), 1)
    same = (r_i >= C) == (c_i >= C)
    incl = same & (r_i >= c_i)
    strict = same & (r_i > c_i)
    eye = (r_i == c_i).astype(F32)
    second = lax.broadcasted_iota(jnp.int32, (L, 1), 0) >= C
    zeros_c = jnp.zeros((C, GD_DV), F32)
    units = [(h, j) for j in range(ts // L) for h in range(GD_HEADS)]

    st = {}
    for h, j in units:
        rs = slice(j * L, (j + 1) * L)
        q = gdo_ref[h, rs, :]
        k = gdo_ref[GD_HEADS + h, rs, :]
        beta = gates[rs, LANE_GD_B + h:LANE_GD_B + h + 1]
        gc_c = csum_gd[rs, LANE_GD_A + h:LANE_GD_A + h + 1]
        gc_r = rows[LANE_GD_A + h:LANE_GD_A + h + 1, rs]
        gam = jnp.where(incl, jnp.exp(jnp.where(incl, gc_c - gc_r, 0.0)), 0.0)
        kb = k * beta
        egc = jnp.exp(gc_c)
        g_first = gc_c[C - 1:C, :]
        g_second = gc_c[L - 1:L, :]
        st[h, j] = dict(k=k, kb=kb, beta=beta, gam=gam, egc=egc, q_dec=q * egc,
                        g_last=(g_first, g_second),
                        k_dec=k * jnp.exp(jnp.where(second, g_second, g_first) - gc_c),
                        kq=_dot_nt(jnp.concatenate([kb, q], axis=0), k))
    yield
    for key in units:
        u = st[key]
        u["aqk"] = u["kq"][L:] * u["gam"]
        pw = jnp.where(strict, -(u["kq"][:L] * u["gam"]), 0.0)
        u["t_inv"] = eye + pw
        u["pw"] = _dot(pw, pw)
    yield
    span = 4
    while span < C:
        for key in units:
            u = st[key]
            both = _dot(jnp.concatenate([u["pw"], u["t_inv"]], axis=0), u["pw"])
            u["pw"] = both[:L]
            u["t_inv"] = u["t_inv"] + both[L:]
        span *= 2
        yield
    for key in units:
        u = st[key]
        u["t_inv"] = u["t_inv"] + _dot(u["t_inv"], u["pw"])
    yield
    for h, j in units:
        u = st[h, j]
        rs = slice(j * L, (j + 1) * L)
        v = gdo_ref[2 * GD_HEADS + h, rs, :]
        uw = _dot(u["t_inv"], jnp.concatenate([v * u["beta"], u["kb"] * u["egc"]], axis=1))
        u["u"] = uw[:, :GD_DV]
        u["w"] = uw[:, GD_DV:]

    yield
    state = [gd_s[h] for h in range(GD_HEADS)]
    outs = [[] for _ in range(GD_HEADS)]
    for j in range(ts // L):
        for half in range(2):
            hs = slice(half * C, (half + 1) * C)
            wq = [_dot(jnp.concatenate([st[h, j]["w"][hs], st[h, j]["q_dec"][hs]], axis=0), state[h])
                  for h in range(GD_HEADS)]
            v_new = [st[h, j]["u"][hs] - wq[h][:C] for h in range(GD_HEADS)]
            for h in range(GD_HEADS):
                v_pad = jnp.concatenate([v_new[h], zeros_c] if half == 0 else [zeros_c, v_new[h]],
                                        axis=0)
                outs[h].append(wq[h][C:] + _dot(st[h, j]["aqk"][hs], v_pad))
            for h in range(GD_HEADS):
                u = st[h, j]
                state[h] = (state[h] * jnp.exp(u["g_last"][half])
                            + _dot_tn(u["k_dec"][hs], v_new[h]))
            yield
    for h in range(GD_HEADS):
        gd_s[h] = state[h]
        o = jnp.concatenate(outs[h], axis=0)
        on = o * lax.rsqrt(jnp.mean(o * o, axis=-1, keepdims=True) + NORM_EPS)
        y_gd = on * gdnw_ref[0:1, :] * p_ref[:, OFF_GD_G + h * GD_DV:OFF_GD_G + (h + 1) * GD_DV]
        y_ref[:, RG_WIDTH + ML_WIDTH + h * GD_DV:RG_WIDTH + ML_WIDTH + (h + 1) * GD_DV] = (
            y_gd.astype(y_ref.dtype))
    project(RG_WIDTH + ML_WIDTH)


def _interleave(order, stages):
    live = dict(stages)

    def step(key):
        if key in live:
            try:
                next(live[key])
            except StopIteration:
                del live[key]

    for key in order:
        step(key)
    while live:
        for key in list(live):
            step(key)


def _layer_kernel(xn_ref, xc_ref, nw_ref, wa_ref, wb_ref, rgp_ref, rgw_ref, gp_ref, gdcw_ref,
                  wo_ref, fw_ref, mlnw_ref, gdnw_ref, cs_ref, o_ref,
                  p_s, gdo_s, rgo_s, yrg_s, hn_s, raw_s, raw_z, xc_s, rg_tail, gd_tail,
                  y_ref, rg_h, ml_c, ml_m, gd_s,
                  *, blocks_per_seq, final_norm):
    i = pl.program_id(0)
    slot = lax.rem(i, 2)
    p_new = p_s.at[slot]
    p_cur = p_s.at[1 - slot]

    @pl.when(i == 0)
    def _():
        p_s[1] = jnp.zeros(p_s.shape[1:], p_s.dtype)
        gdo_s[1] = jnp.zeros(gdo_s.shape[1:], gdo_s.dtype)
        rgo_s[1] = jnp.zeros(rgo_s.shape[1:], rgo_s.dtype)

    @pl.when(lax.rem(i, blocks_per_seq) == 0)
    def _():
        rg_tail[...] = jnp.zeros_like(rg_tail)
        gd_tail[...] = jnp.zeros_like(gd_tail)

    @pl.when(lax.rem(jnp.maximum(i - 1, 0), blocks_per_seq) == 0)
    def _():
        rg_h[...] = jnp.zeros_like(rg_h)
        ml_c[...] = jnp.zeros_like(ml_c)
        ml_m[...] = jnp.zeros_like(ml_m)
        gd_s[...] = jnp.zeros_like(gd_s)

    def project(k0):
        pending.discard(k0)
        if pending:
            return
        for c in range(0, D_MODEL, 2 * LANES):
            o_ref[:, c:c + 2 * LANES] = xc_ref[:, c:c + 2 * LANES] + jnp.dot(
                y_ref[...], wo_ref[:, c:c + 2 * LANES], preferred_element_type=F32)

    pending = {0, RG_WIDTH, RG_WIDTH + ML_WIDTH}

    gates, csum_gd, csum_ml, rows = _gate_tables(p_cur, cs_ref)
    _interleave(LAYER_ORDER, dict(
        P=_inproj_stream(xn_ref, nw_ref, wa_ref, wb_ref, rgp_ref, rgw_ref, gp_ref, gdcw_ref,
                         p_new, gdo_s.at[slot], rgo_s.at[slot], hn_s, raw_s, raw_z, xc_s,
                         rg_tail, gd_tail),
        G=_gdn(p_cur, gdo_s.at[1 - slot], gdnw_ref, y_ref, gd_s, gates, csum_gd, rows, project),
        M=_mlstm(p_cur, mlnw_ref, y_ref, ml_c, ml_m, gates, csum_ml, rows, project),
        R=_rglru(rgo_s.at[1 - slot], yrg_s, y_ref, rg_h, project)))
    if final_norm:
        xn = o_ref[...]
        r = lax.rsqrt(jnp.mean(xn * xn, axis=-1, keepdims=True) + NORM_EPS)
        o_ref[...] = (xn * r) * fw_ref[...]


def _cumsum_matrix(ts):
    r = jnp.arange(ts)[:, None]
    c = jnp.arange(ts)[None, :]
    m64 = (r >= c) & (r // GD_CHUNK == c // GD_CHUNK)
    m128 = (r >= c) & (r // PAIR == c // PAIR)
    return jnp.concatenate([m64, m128], axis=0).astype(BF16)


def _layer(x2, norm_w, w_a, w_b, rgp, rgw, gp, gdcw, w_out, final_w, mlnw, gdnw, cs, layer, seq,
           final_norm):
    t = x2.shape[0]
    ts = TS_BLOCK
    nb = t // ts
    per_layer = lambda a, **kw: pl.BlockSpec((None,) + a.shape[1:],
                                             lambda i: (layer,) + (0,) * (a.ndim - 1), **kw)
    once = dict(pipeline_mode=pl.Buffered(1))
    return pl.pallas_call(
        functools.partial(_layer_kernel, blocks_per_seq=seq // ts, final_norm=final_norm),
        grid=(nb + 1,),
        in_specs=[pl.BlockSpec((ts, D_MODEL), lambda i: (jnp.minimum(i, nb - 1), 0)),
                  pl.BlockSpec((ts, D_MODEL), lambda i: (jnp.maximum(i - 1, 0), 0)),
                  per_layer(norm_w),
                  pl.BlockSpec((None, D_MODEL, WCOL_SPLIT), lambda i: (layer, 0, 0), **once),
                  per_layer(w_b, **once), per_layer(rgp), per_layer(rgw), per_layer(gp),
                  per_layer(gdcw), per_layer(w_out, **once),
                  pl.BlockSpec(final_w.shape, lambda i: (0, 0)),
                  per_layer(mlnw), per_layer(gdnw), pl.BlockSpec(cs.shape, lambda i: (0, 0))],
        out_specs=pl.BlockSpec((ts, D_MODEL), lambda i: (jnp.maximum(i - 1, 0), 0)),
        out_shape=jax.ShapeDtypeStruct((t, D_MODEL), F32),
        scratch_shapes=[pltpu.VMEM((2, ts, D_PROJ), F32),
                        pltpu.VMEM((2, GD_CONV // LANES, ts, LANES), F32),
                        pltpu.VMEM((2, 3 * RG_TILES, ts, LANES), F32),
                        pltpu.VMEM((RG_TILES, ts, LANES), F32),
                        pltpu.VMEM((ts, D_MODEL), BF16),
                        pltpu.VMEM((RAW_TILES, ts, LANES), F32),
                        pltpu.VMEM((ts, W), F32),
                        pltpu.VMEM((RG_WIDTH // LANES, ts, LANES), F32),
                        pltpu.VMEM((SUBLANES, RG_WIDTH), F32),
                        pltpu.VMEM((SUBLANES, GD_CONV), F32),
                        pltpu.VMEM((ts, D_MIX), BF16),
                        pltpu.VMEM((SUBLANES, RG_WIDTH), F32),
                        pltpu.VMEM((ML_HEADS, ML_DH, 2 * ML_DH), F32),
                        pltpu.VMEM((ML_HEADS, SUBLANES, LANES), F32),
                        pltpu.VMEM((GD_HEADS, GD_DK, GD_DV), F32)],
        compiler_params=pltpu.CompilerParams(dimension_semantics=("arbitrary",),
                                             vmem_limit_bytes=VMEM_LIMIT),
        name="layer",
    )(x2, x2, norm_w, w_a, w_b, rgp, rgw, gp, gdcw, w_out, final_w, mlnw, gdnw, cs)


def _w_in_tail(w_in):
    pad = jnp.zeros(w_in.shape[:-1] + (LANES - GATE_LANES,), w_in.dtype)
    return jnp.concatenate([
        w_in[..., SRC_GD:SRC_GD_GATES], w_in[..., SRC_ML_GATES:SRC_GD],
        w_in[..., SRC_GD_GATES:], pad], axis=-1)


def _rg_gate_blockdiag(gate_w):
    depth = gate_w.shape[0]
    nb = RG_BLOCKS // 2
    eye = jnp.eye(nb, dtype=gate_w.dtype)
    halves = []
    for hf in range(2):
        blk = gate_w[:, :, hf * nb:(hf + 1) * nb]
        bd = jnp.einsum('lgncd,nm->lgncmd', blk, eye)
        bd = bd.reshape(depth, 2, nb * RG_BLOCK, nb * RG_BLOCK)
        halves.append(jnp.concatenate([bd[:, 0], bd[:, 1]], axis=-1))
    return jnp.stack(halves, axis=1)


def _gate_params(ml_gate_b, gd_dt_bias, gd_a_log):
    depth = ml_gate_b.shape[0]
    zeros = lambda n: jnp.zeros((depth, n), F32)
    row0 = jnp.concatenate([ml_gate_b[:, 0], ml_gate_b[:, 1], gd_dt_bias,
                            zeros(LANES - LANE_GD_B)], axis=-1)
    row1 = jnp.concatenate([zeros(LANE_GD_A), gd_a_log, zeros(LANES - LANE_GD_B)], axis=-1)
    rest = jnp.zeros((depth, SUBLANES - 2, LANES), F32)
    return jnp.concatenate([row0[:, None], row1[:, None], rest], axis=1)


def kernel(x, norm_w, w_in, rg_conv_w, rg_conv_b, rg_gate_w, rg_gate_b, rg_lambda, ml_gate_b,
           ml_norm_w, gd_conv_w, gd_a_log, gd_dt_bias, gd_norm_w, w_out, final_norm_w):
    bsz, seq, _ = x.shape
    depth = w_in.shape[0]
    t = bsz * seq
    w_a = w_in.astype(BF16)
    w_b = _w_in_tail(w_a)
    w_o = w_out.astype(BF16)
    rgp = jnp.concatenate([rg_conv_w, rg_conv_b[:, None], rg_gate_b, rg_lambda[:, None]],
                          axis=1)
    rgw = _rg_gate_blockdiag(rg_gate_w).astype(BF16)
    gp = _gate_params(ml_gate_b, gd_dt_bias, gd_a_log)
    cs = _cumsum_matrix(TS_BLOCK)

    x2 = x.reshape(t, D_MODEL)
    for l in range(depth):
        x2 = _layer(x2, norm_w[:, None], w_a, w_b, rgp, rgw, gp, gd_conv_w, w_o, final_norm_w[None],
                    ml_norm_w[:, None], gd_norm_w[:, None], cs, l, seq, l == depth - 1)
    return x2.reshape(bsz, seq, D_MODEL)
```
